```python
import math
import jax
import jax.numpy as jnp
from jax import lax
import numpy as np

D_MODEL = 1024
BATCH = 1
SEQ = 16384
DEPTH = 4

CHUNK = 64
Q_BLOCK = 128
PLE_DIM = 256

A_HEADS = 8
A_HEAD_DIM = 64
A_LEFT_CHUNKS = 8
A_BAND = (A_LEFT_CHUNKS + 1) * CHUNK
A_REL_CLIP = 128
A_WIDTH = A_HEADS * A_HEAD_DIM

B_HEADS = 8
B_NOPE_DIM = 64
B_ROPE_DIM = 32
B_V_DIM = 64
B_Q_RANK = 512
B_KV_RANK = 256
ROPE_BASE = 10000.0

C_HEADS = 8
C_HEAD_DIM = 64
C_QK = 2 * C_HEADS * C_HEAD_DIM
C_OUT = C_HEADS * 2 * C_HEAD_DIM
C_IN = 2 * C_QK + C_OUT

T5_BUCKETS = 32
T5_MAX_DIST = 128

N_EXPERTS = 32
TOP_K = 4
D_FF = 1024
SWIGLU_ALPHA = 1.702
SWIGLU_LIMIT = 7.0

DEEPNORM_ALPHA = (2.0 * DEPTH) ** 0.25
DEEPNORM_BETA = (8.0 * DEPTH) ** -0.25
LN_EPS = 1e-5
RMS_EPS = 1e-6
N_AB = (DEPTH + 1) // 2
N_C = DEPTH // 2

AB_IN = 3 * A_WIDTH + B_Q_RANK + B_KV_RANK + B_ROPE_DIM
AB_OUT = A_WIDTH + B_HEADS * B_V_DIM
AB_SPLITS = [A_WIDTH, 2 * A_WIDTH, 3 * A_WIDTH, 3 * A_WIDTH + B_Q_RANK, 3 * A_WIDTH + B_Q_RANK + B_KV_RANK]
C_SPLITS = [C_QK, 2 * C_QK]

F32 = jnp.float32
NEG_INF = -1e30

kernel_name = 'hybrid_chunk_causal_moe_trunk'


def layer_norm(x, g, b):
    xf = x.astype(F32)
    mu = jnp.mean(xf, axis=-1, keepdims=True)
    var = jnp.mean(jnp.square(xf - mu), axis=-1, keepdims=True)
    return ((xf - mu) * lax.rsqrt(var + LN_EPS) * g.astype(F32) + b.astype(F32)).astype(x.dtype)


def rms_norm(x, g):
    xf = x.astype(F32)
    inv = lax.rsqrt(jnp.mean(jnp.square(xf), axis=-1, keepdims=True) + RMS_EPS)
    return (xf * inv * g.astype(F32)).astype(x.dtype)


def rope(x, pos):
    d = x.shape[-1]
    inv_freq = ROPE_BASE ** (-jnp.arange(0, d, 2, dtype=F32) / d)
    ang = pos.astype(F32)[:, None] * inv_freq[None, :]
    cos = jnp.cos(ang)[None, :, None, :]
    sin = jnp.sin(ang)[None, :, None, :]
    xf = x.astype(F32)
    x1, x2 = xf[..., : d // 2], xf[..., d // 2:]
    return jnp.concatenate([x1 * cos - x2 * sin, x2 * cos + x1 * sin], axis=-1).astype(x.dtype)


def masked_softmax(logits, mask):
    return jax.nn.softmax(jnp.where(mask, logits, NEG_INF), axis=-1)


def block_chunk_mask(block_idx, s):
    q_pos = block_idx * Q_BLOCK + jnp.arange(Q_BLOCK)
    k_pos = jnp.arange(s)
    return (k_pos[None, :] // CHUNK) <= (q_pos[:, None] // CHUNK)


def sweep_query_blocks(block_fn, q_arrays):
    b, s = q_arrays[0].shape[:2]
    nb = s // Q_BLOCK
    xs = tuple(jnp.swapaxes(q.reshape(b, nb, Q_BLOCK, *q.shape[2:]), 0, 1) for q in q_arrays)
    out = lax.map(lambda args: block_fn(args[0], *args[1:]), (jnp.arange(nb), *xs))
    return jnp.swapaxes(out, 0, 1).reshape(b, s, *out.shape[3:])


def t5_bucket(rel):
    half = T5_BUCKETS // 2
    max_exact = half // 2
    ret = jnp.where(rel > 0, half, 0)
    n = jnp.abs(rel)
    large = max_exact + (jnp.log(jnp.maximum(n, max_exact).astype(F32) / max_exact)
                         / math.log(T5_MAX_DIST / max_exact) * (half - max_exact)).astype(jnp.int32)
    large = jnp.minimum(large, half - 1)
    return ret + jnp.where(n < max_exact, n, large)


def chunked_band_attention(q, k, v, rel_bias):
    b, s, h, dh = q.shape
    nc = s // CHUNK
    qc = q.reshape(b, nc, CHUNK, h, dh)
    pad = ((0, 0), (A_LEFT_CHUNKS, 0), (0, 0), (0, 0), (0, 0))
    kc = jnp.pad(k.reshape(b, nc, CHUNK, h, dh), pad)
    vc = jnp.pad(v.reshape(b, nc, CHUNK, h, dh), pad)
    k_band = jnp.concatenate([kc[:, j:j + nc] for j in range(A_LEFT_CHUNKS + 1)], axis=2)
    v_band = jnp.concatenate([vc[:, j:j + nc] for j in range(A_LEFT_CHUNKS + 1)], axis=2)
    logits = jnp.einsum('bcqhd,bckhd->bchqk', qc, k_band).astype(F32) * (dh ** -0.5)
    qi = jnp.arange(CHUNK)
    ki = jnp.arange(A_BAND)
    rel = A_LEFT_CHUNKS * CHUNK + qi[:, None] - ki[None, :]
    bias = rel_bias[jnp.clip(rel, -A_REL_CLIP, A_REL_CLIP) + A_REL_CLIP]
    logits = logits + jnp.transpose(bias, (2, 0, 1)).astype(F32)[None, None]
    valid = (jnp.arange(nc)[:, None] - A_LEFT_CHUNKS + ki[None, :] // CHUNK) >= 0
    probs = masked_softmax(logits, valid[None, :, None, None, :])
    out = jnp.einsum('bchqk,bckhd->bcqhd', probs.astype(v.dtype), v_band)
    return out.reshape(b, s, h, dh)


def latent_attention(c_q, c_kv, k_r, q_norm_g, kv_norm_g, w_uq, w_ukv, pos):
    b, s, _ = c_q.shape
    q = (rms_norm(c_q, q_norm_g) @ w_uq).reshape(b, s, B_HEADS, B_NOPE_DIM + B_ROPE_DIM)
    q_nope = q[..., :B_NOPE_DIM]
    q_rope = rope(q[..., B_NOPE_DIM:], pos)
    kv = (rms_norm(c_kv, kv_norm_g) @ w_ukv).reshape(b, s, B_HEADS, B_NOPE_DIM + B_V_DIM)
    k_nope = kv[..., :B_NOPE_DIM]
    v = kv[..., B_NOPE_DIM:]
    k_rope = rope(k_r[:, :, None, :], pos)[:, :, 0]
    scale = (B_NOPE_DIM + B_ROPE_DIM) ** -0.5

    def block(bi, qn, qr):
        logits = (jnp.einsum('bqhd,bkhd->bhqk', qn, k_nope).astype(F32)
                  + jnp.einsum('bqhd,bkd->bhqk', qr, k_rope).astype(F32)) * scale
        probs = masked_softmax(logits, block_chunk_mask(bi, s)[None, None])
        return jnp.einsum('bhqk,bkhd->bqhd', probs.astype(v.dtype), v)

    return sweep_query_blocks(block, (q_nope, q_rope))


def lambda_init(layer_idx):
    return 0.8 - 0.6 * math.exp(-0.3 * layer_idx)


def differential_attention(q, k, v, lam_params, subln_g, t5_table, lam_init):
    b, s = q.shape[:2]
    lp = lam_params.astype(F32)
    lam = jnp.exp(jnp.sum(lp[0] * lp[1])) - jnp.exp(jnp.sum(lp[2] * lp[3])) + lam_init
    scale = C_HEAD_DIM ** -0.5

    def block(bi, qb):
        logits = jnp.einsum('bqhnd,bkhnd->bhnqk', qb, k).astype(F32) * scale
        rel = jnp.arange(s)[None, :] - (bi * Q_BLOCK + jnp.arange(Q_BLOCK))[:, None]
        bias = jnp.transpose(t5_table[t5_bucket(rel)], (2, 0, 1)).astype(F32)
        probs = masked_softmax(logits + bias[None, :, None], block_chunk_mask(bi, s)[None, None, None])
        attn = probs[:, :, 0] - lam * probs[:, :, 1]
        return jnp.einsum('bhqk,bkhd->bqhd', attn.astype(v.dtype), v)

    o = sweep_query_blocks(block, (q,))
    return rms_norm(o, subln_g) * (1.0 - lam_init)


def moe_ffn(x, w_router, b_router, w_in, b_in, w_out, b_out):
    logits = (x @ w_router).astype(F32) + b_router.astype(F32)
    top_val, top_idx = lax.top_k(logits, TOP_K)
    top_w = jax.nn.softmax(top_val, axis=-1)
    gates = jnp.einsum('bsk,bske->ebs', top_w, jax.nn.one_hot(top_idx, N_EXPERTS, dtype=F32))

    def expert(acc, params):
        wi, bi, wo, bo, g = params
        h = x @ wi + bi
        glu = jnp.minimum(h[..., :D_FF], SWIGLU_LIMIT)
        lin = jnp.clip(h[..., D_FF:], -SWIGLU_LIMIT, SWIGLU_LIMIT)
        y = (glu * jax.nn.sigmoid(SWIGLU_ALPHA * glu) * (lin + 1.0)) @ wo + bo
        return acc + g[..., None].astype(x.dtype) * y, None

    out, _ = lax.scan(expert, jnp.zeros_like(x), (w_in, b_in, w_out, b_out, gates))
    return out


def _normal(key, shape, scale):
    return jax.random.normal(key, shape, F32) * scale


def setup_inputs(seed: int = 0) -> dict:
    key = jax.random.key(seed)
    ks = jax.random.split(key, 26)
    d = D_MODEL
    return {
        'x': _normal(ks[0], (BATCH, SEQ, d), 1.0),
        'p': _normal(ks[1], (DEPTH, BATCH, SEQ, PLE_DIM), 1.0),
        'ab_w_in': _normal(ks[2], (N_AB, d, AB_IN), d ** -0.5),
        'ab_rel_bias': _normal(ks[3], (N_AB, 2 * A_REL_CLIP + 1, A_HEADS), 0.2),
        'ab_q_norm': 1.0 + _normal(ks[4], (N_AB, B_Q_RANK), 0.01),
        'ab_kv_norm': 1.0 + _normal(ks[5], (N_AB, B_KV_RANK), 0.01),
        'ab_w_uq': _normal(ks[6], (N_AB, B_Q_RANK, B_HEADS * (B_NOPE_DIM + B_ROPE_DIM)), B_Q_RANK ** -0.5),
        'ab_w_ukv': _normal(ks[7], (N_AB, B_KV_RANK, B_HEADS * (B_NOPE_DIM + B_V_DIM)), B_KV_RANK ** -0.5),
        'ab_w_out': _normal(ks[8], (N_AB, AB_OUT, d), DEEPNORM_BETA * AB_OUT ** -0.5),
        'c_w_in': _normal(ks[9], (N_C, d, C_IN), d ** -0.5),
        'c_lambda': _normal(ks[10], (N_C, 4, C_HEAD_DIM), 0.1),
        'c_subln': 1.0 + _normal(ks[11], (N_C, 2 * C_HEAD_DIM), 0.01),
        'c_w_out': _normal(ks[12], (N_C, C_OUT, d), DEEPNORM_BETA * C_OUT ** -0.5),
        't5_table': _normal(ks[13], (T5_BUCKETS, C_HEADS), 0.2),
        'ln_mix_g': 1.0 + _normal(ks[14], (DEPTH, d), 0.01),
        'ln_mix_b': _normal(ks[15], (DEPTH, d), 0.01),
        'ln_ffn_g': 1.0 + _normal(ks[16], (DEPTH, d), 0.01),
        'ln_ffn_b': _normal(ks[17], (DEPTH, d), 0.01),
        'router_w': _normal(ks[18], (DEPTH, d, N_EXPERTS), d ** -0.5),
        'router_b': _normal(ks[19], (DEPTH, N_EXPERTS), 0.01),
        'exp_w_in': _normal(ks[20], (DEPTH, N_EXPERTS, d, 2 * D_FF), d ** -0.5),
        'exp_b_in': _normal(ks[21], (DEPTH, N_EXPERTS, 2 * D_FF), 0.01),
        'exp_w_out': _normal(ks[22], (DEPTH, N_EXPERTS, D_FF, d), DEEPNORM_BETA * D_FF ** -0.5),
        'exp_b_out': _normal(ks[23], (DEPTH, N_EXPERTS, d), 0.01),
        'ple_gate_w': _normal(ks[24], (DEPTH, d, d), d ** -0.5),
        'ple_proj_w': _normal(ks[25], (DEPTH, PLE_DIM, d), DEEPNORM_BETA * PLE_DIM ** -0.5),
    }


def reference(x, p, ab_w_in, ab_rel_bias, ab_q_norm, ab_kv_norm, ab_w_uq, ab_w_ukv, ab_w_out,
              c_w_in, c_lambda, c_subln, c_w_out, t5_table,
              ln_mix_g, ln_mix_b, ln_ffn_g, ln_ffn_b,
              router_w, router_b, exp_w_in, exp_b_in, exp_w_out, exp_b_out,
              ple_gate_w, ple_proj_w):
    b, s, _ = x.shape
    pos = jnp.arange(s)
    for i in range(DEPTH):
        j = i // 2
        if i % 2 == 0:
            h = x @ ab_w_in[j]
            qa, ka, va, c_q, c_kv, k_r = jnp.split(h, AB_SPLITS, axis=-1)
            o_a = chunked_band_attention(qa.reshape(b, s, A_HEADS, A_HEAD_DIM),
                                         ka.reshape(b, s, A_HEADS, A_HEAD_DIM),
                                         va.reshape(b, s, A_HEADS, A_HEAD_DIM),
                                         ab_rel_bias[j])
            o_b = latent_attention(c_q, c_kv, k_r, ab_q_norm[j], ab_kv_norm[j],
                                   ab_w_uq[j], ab_w_ukv[j], pos)
            mixed = jnp.concatenate([o_a.reshape(b, s, -1), o_b.reshape(b, s, -1)], axis=-1) @ ab_w_out[j]
        else:
            h = x @ c_w_in[j]
            qc, kc, vc = jnp.split(h, C_SPLITS, axis=-1)
            o_c = differential_attention(qc.reshape(b, s, C_HEADS, 2, C_HEAD_DIM),
                                         kc.reshape(b, s, C_HEADS, 2, C_HEAD_DIM),
                                         vc.reshape(b, s, C_HEADS, 2 * C_HEAD_DIM),
                                         c_lambda[j], c_subln[j], t5_table, lambda_init(i))
            mixed = o_c.reshape(b, s, C_OUT) @ c_w_out[j]
        x = layer_norm(DEEPNORM_ALPHA * x + mixed, ln_mix_g[i], ln_mix_b[i])
        ffn = moe_ffn(x, router_w[i], router_b[i], exp_w_in[i], exp_b_in[i], exp_w_out[i], exp_b_out[i])
        ple = jax.nn.sigmoid(x @ ple_gate_w[i]) * (p[i] @ ple_proj_w[i])
        x = layer_norm(DEEPNORM_ALPHA * x + ffn + ple, ln_ffn_g[i], ln_ffn_b[i])
    return x
```

```python
import functools
import math

import jax
import jax.numpy as jnp
import numpy as np
from jax import lax
from jax.experimental import pallas as pl
from jax.experimental.pallas import tpu as pltpu

F32 = jnp.float32
BF16 = jnp.bfloat16
I32 = jnp.int32

D_MODEL = 1024
DEPTH = 4
CHUNK = 64
PLE_DIM = 256

A_HEADS = 8
A_HEAD_DIM = 64
A_LEFT_CHUNKS = 8
A_REL_CLIP = 128
A_WIDTH = A_HEADS * A_HEAD_DIM

B_HEADS = 8
B_NOPE_DIM = 64
B_ROPE_DIM = 32
B_V_DIM = 64
B_Q_RANK = 512
B_KV_RANK = 256
ROPE_BASE = 10000.0

C_HEADS = 8
C_HEAD_DIM = 64
C_QK = 2 * C_HEADS * C_HEAD_DIM

T5_BUCKETS = 32
T5_MAX_DIST = 128

N_EXPERTS = 32
TOP_K = 4
D_FF = 1024
SWIGLU_ALPHA = 1.702
SWIGLU_LIMIT = 7.0

DEEPNORM_ALPHA = (2.0 * DEPTH) ** 0.25
LN_EPS = 1e-5
RMS_EPS = 1e-6
NEG_INF = -1e30

LANES = 128
VMEM_LIMIT = 56 * 1024 * 1024

ROW_TILE = 512
ATT_TILE = 512
BAND_GROUP = 256
BAND_WINDOW = BAND_GROUP + A_LEFT_CHUNKS * CHUNK
ROUTER_TILE = 512
DISPATCH_TILE = 1024
EXPERT_TILE = 256
COMBINE_TILE = 256
CAST_ROWS = 128

NT_DIMS = (((1,), (1,)), ((), ()))


def _params(*sem):
    return pltpu.CompilerParams(dimension_semantics=sem, vmem_limit_bytes=VMEM_LIMIT)


def _full(shape):
    n = len(shape)
    return pl.BlockSpec(shape, lambda *_: (0,) * n)


def _layer_norm(y, g, b):
    mu = jnp.mean(y, axis=-1, keepdims=True)
    d = y - mu
    var = jnp.mean(d * d, axis=-1, keepdims=True)
    return d * lax.rsqrt(var + LN_EPS) * g + b


def _rms_norm(c, g):
    inv = lax.rsqrt(jnp.mean(c * c, axis=-1, keepdims=True) + RMS_EPS)
    return c * inv * g


def _prep_ab_kernel(x_ref, wa_ref, wcq_ref, wckv_ref, wkr_ref, wkrr_ref, qn_ref, kvn_ref,
                    wuqa_ref, wuqb_ref, wukn_ref, wuv_ref, place_ref, ctab_ref, stab_ref,
                    cos_ref, sin_ref,
                    qa_ref, ka_ref, va_ref, qb_ref, kb_ref, vb_ref):
    xb = x_ref[...].astype(BF16)
    ha = jnp.dot(xb, wa_ref[...], preferred_element_type=F32)
    qa_ref[...] = (ha[:, :A_WIDTH] * (A_HEAD_DIM ** -0.5)).astype(BF16)
    ka_ref[...] = ha[:, A_WIDTH:2 * A_WIDTH].astype(BF16)
    va_ref[...] = ha[:, 2 * A_WIDTH:].astype(BF16)

    cq = jnp.dot(xb, wcq_ref[...], preferred_element_type=F32)
    cqn = _rms_norm(cq, qn_ref[...]).astype(BF16)
    qa_part = jnp.dot(cqn, wuqa_ref[...], preferred_element_type=F32)
    qb_part = jnp.dot(cqn, wuqb_ref[...], preferred_element_type=F32)
    ct = ctab_ref[...]
    st = stab_ref[...]
    scale = (B_NOPE_DIM + B_ROPE_DIM) ** -0.5
    for h in range(B_HEADS):
        sl = slice(h * LANES, (h + 1) * LANES)
        qb_ref[:, sl] = ((qa_part[:, sl] * ct + qb_part[:, sl] * st) * scale).astype(BF16)

    ckv = jnp.dot(xb, wckv_ref[...], preferred_element_type=F32)
    ckvn = _rms_norm(ckv, kvn_ref[...]).astype(BF16)
    kr = jnp.dot(xb, wkr_ref[...], preferred_element_type=F32)
    krr = jnp.dot(xb, wkrr_ref[...], preferred_element_type=F32)
    k_rope = (kr * cos_ref[...] + krr * sin_ref[...]).astype(BF16)
    kb = jnp.dot(ckvn, wukn_ref[...], preferred_element_type=F32)
    kb = kb + jnp.dot(k_rope, place_ref[...], preferred_element_type=F32)
    kb_ref[...] = kb.astype(BF16)
    vb_ref[...] = jnp.dot(ckvn, wuv_ref[...], preferred_element_type=F32).astype(BF16)


def _rot_half_cols(w):
    half = w.shape[-1] // 2
    return jnp.concatenate([-w[..., half:], w[..., :half]], axis=-1)


def _prep_ab(x, w_in, q_norm, kv_norm, w_uq, w_ukv):
    s = x.shape[0]
    splits = [3 * A_WIDTH, 3 * A_WIDTH + B_Q_RANK, 3 * A_WIDTH + B_Q_RANK + B_KV_RANK]
    wa = w_in[:, :splits[0]].astype(BF16)
    wcq = w_in[:, splits[0]:splits[1]].astype(BF16)
    wckv = w_in[:, splits[1]:splits[2]].astype(BF16)
    wkr_f = w_in[:, splits[2]:]
    wkr = wkr_f.astype(BF16)
    wkrr = _rot_half_cols(wkr_f).astype(BF16)

    qd = B_NOPE_DIM + B_ROPE_DIM
    w_uq3 = w_uq.reshape(B_Q_RANK, B_HEADS, qd)
    zpad = jnp.zeros((B_Q_RANK, B_HEADS, LANES - qd), F32)
    wuqa = jnp.concatenate([w_uq3, zpad], axis=-1).reshape(B_Q_RANK, B_HEADS * LANES).astype(BF16)
    wuqb = jnp.concatenate([jnp.zeros((B_Q_RANK, B_HEADS, B_NOPE_DIM), F32),
                            _rot_half_cols(w_uq3[..., B_NOPE_DIM:]), zpad],
                           axis=-1).reshape(B_Q_RANK, B_HEADS * LANES).astype(BF16)
    w_ukv3 = w_ukv.reshape(B_KV_RANK, B_HEADS, B_NOPE_DIM + B_V_DIM)
    wukn = jnp.concatenate([w_ukv3[..., :B_NOPE_DIM],
                            jnp.zeros((B_KV_RANK, B_HEADS, LANES - B_NOPE_DIM), F32)],
                           axis=-1).reshape(B_KV_RANK, B_HEADS * LANES).astype(BF16)
    wuv = w_ukv3[..., B_NOPE_DIM:].reshape(B_KV_RANK, B_HEADS * B_V_DIM).astype(BF16)
    place = np.zeros((B_ROPE_DIM, B_HEADS * LANES), np.float32)
    for h in range(B_HEADS):
        for r in range(B_ROPE_DIM):
            place[r, h * LANES + B_NOPE_DIM + r] = 1.0
    place = jnp.asarray(place, BF16)

    pos = jnp.arange(s, dtype=F32)
    inv_freq = ROPE_BASE ** (-jnp.arange(0, B_ROPE_DIM, 2, dtype=F32) / B_ROPE_DIM)
    ang = pos[:, None] * inv_freq[None, :]
    cos = jnp.concatenate([jnp.cos(ang)] * 2, axis=-1)
    sin = jnp.concatenate([jnp.sin(ang)] * 2, axis=-1)
    ctab = jnp.concatenate([jnp.ones((s, B_NOPE_DIM), F32), cos, jnp.zeros((s, LANES - qd), F32)], axis=-1)
    stab = jnp.concatenate([jnp.zeros((s, B_NOPE_DIM), F32), sin, jnp.zeros((s, LANES - qd), F32)], axis=-1)

    t = ROW_TILE
    row = lambda w: pl.BlockSpec((t, w), lambda i: (i, 0))
    consts = [wa, wcq, wckv, wkr, wkrr, q_norm.reshape(1, -1), kv_norm.reshape(1, -1),
              wuqa, wuqb, wukn, wuv, place]
    out_widths = [A_WIDTH, A_WIDTH, A_WIDTH, B_HEADS * LANES, B_HEADS * LANES, B_HEADS * B_V_DIM]
    return pl.pallas_call(
        _prep_ab_kernel,
        grid=(s // t,),
        in_specs=[row(D_MODEL)] + [_full(c.shape) for c in consts]
        + [row(LANES), row(LANES), row(B_ROPE_DIM), row(B_ROPE_DIM)],
        out_specs=[row(w) for w in out_widths],
        out_shape=[jax.ShapeDtypeStruct((s, w), BF16) for w in out_widths],
        compiler_params=_params("arbitrary"),
        name="prep_ab",
    )(x, *consts, ctab, stab, cos, sin)


def _attn_a_kernel(q_ref, kp_ref, kc_ref, vp_ref, vc_ref, bias_ref, o_ref, kbuf, vbuf):
    j = pl.program_id(1)
    t = ATT_TILE
    kbuf[0:t, :] = kp_ref[...]
    kbuf[t:2 * t, :] = kc_ref[...]
    vbuf[0:t, :] = vp_ref[...]
    vbuf[t:2 * t, :] = vc_ref[...]
    lo = lax.broadcasted_iota(I32, (1, LANES), 1) < A_HEAD_DIM
    col = lax.broadcasted_iota(I32, (1, BAND_WINDOW), 1)
    for g in range(t // BAND_GROUP):
        r0 = g * BAND_GROUP
        q = q_ref[r0:r0 + BAND_GROUP, :]
        k = kbuf[r0:r0 + BAND_WINDOW, :]
        v = vbuf[r0:r0 + BAND_WINDOW, :]
        valid = col + r0 >= jnp.where(j > 0, 0, t)
        outs = []
        for hh in range(2):
            qm = jnp.where(lo if hh == 0 else jnp.logical_not(lo), q, jnp.zeros_like(q))
            sc = lax.dot_general(qm, k, NT_DIMS, preferred_element_type=F32)
            sc = jnp.where(valid, sc + bias_ref[hh], NEG_INF)
            m = jnp.max(sc, axis=-1, keepdims=True)
            p = jnp.exp(sc - m)
            l = jnp.sum(p, axis=-1, keepdims=True)
            o = jnp.dot(p.astype(BF16), v, preferred_element_type=F32)
            outs.append(o / l)
        o_ref[r0:r0 + BAND_GROUP, :] = jnp.where(lo, outs[0], outs[1]).astype(BF16)


def _band_bias(rel_bias):
    qi = np.arange(BAND_GROUP)[:, None]
    kr = np.arange(BAND_WINDOW)[None, :]
    rel = qi + A_LEFT_CHUNKS * CHUNK - kr
    dchunk = (qi // CHUNK + A_LEFT_CHUNKS) - kr // CHUNK
    allowed = (dchunk >= 0) & (dchunk <= A_LEFT_CHUNKS)
    idx = np.clip(rel, -A_REL_CLIP, A_REL_CLIP) + A_REL_CLIP
    table = jnp.transpose(rel_bias[idx], (2, 0, 1)).astype(F32)
    return jnp.where(jnp.asarray(allowed)[None], table, NEG_INF)


def _attn_a(qa, ka, va, rel_bias):
    s = qa.shape[0]
    t = ATT_TILE
    bias = _band_bias(rel_bias)
    cur = pl.BlockSpec((t, LANES), lambda p, j: (j, p))
    prev = pl.BlockSpec((t, LANES), lambda p, j: (jnp.maximum(j - 1, 0), p))
    return pl.pallas_call(
        _attn_a_kernel,
        grid=(A_HEADS // 2, s // t),
        in_specs=[cur, prev, cur, prev, cur,
                  pl.BlockSpec((2, BAND_GROUP, BAND_WINDOW), lambda p, j: (p, 0, 0))],
        out_specs=cur,
        out_shape=jax.ShapeDtypeStruct((s, A_WIDTH), BF16),
        scratch_shapes=[pltpu.VMEM((2 * t, LANES), BF16), pltpu.VMEM((2 * t, LANES), BF16)],
        compiler_params=_params("arbitrary", "arbitrary"),
        name="attn_a",
    )(qa, ka, ka, va, va, bias)


def _flash_update(q, k, v, bias, m_ref, l_ref, acc_ref):
    sc = lax.dot_general(q, k, NT_DIMS, preferred_element_type=F32)
    if bias is not None:
        sc = sc + bias
    m_prev = m_ref[...]
    m_new = jnp.maximum(m_prev, jnp.max(sc, axis=-1, keepdims=True))
    alpha = jnp.exp(m_prev - m_new)
    p = jnp.exp(sc - m_new)
    l_ref[...] = alpha * l_ref[...] + jnp.sum(p, axis=-1, keepdims=True)
    acc_ref[...] = alpha * acc_ref[...] + jnp.dot(p.astype(BF16), v, preferred_element_type=F32)
    m_ref[...] = m_new


def _chunk_causal_bias(t):
    r = np.arange(t)
    ok = (r[None, :] // CHUNK) <= (r[:, None] // CHUNK)
    return jnp.asarray(np.where(ok, 0.0, NEG_INF), F32)


def _attn_b_kernel(q_ref, k_ref, v_ref, diag_ref, o_ref, m0, l0, a0, m1, l1, a1):
    i = pl.program_id(1)
    t = ATT_TILE
    state = ((m0, l0, a0), (m1, l1, a1))
    for m_ref, l_ref, acc_ref in state:
        m_ref[...] = jnp.full(m_ref.shape, NEG_INF, F32)
        l_ref[...] = jnp.zeros(l_ref.shape, F32)
        acc_ref[...] = jnp.zeros(acc_ref.shape, F32)
    qs = (q_ref[:, :LANES], q_ref[:, LANES:])

    def step(kt, bias):
        r0 = pl.multiple_of(kt * t, t)
        v = v_ref[pl.ds(r0, t), :]
        for hh in range(2):
            k = k_ref[pl.ds(r0, t), hh * LANES:(hh + 1) * LANES]
            _flash_update(qs[hh], k, v, bias, *state[hh])

    def body(kt, carry):
        step(kt, None)
        return carry

    lax.fori_loop(0, i, body, 0)
    step(i, diag_ref[...])
    lo = lax.broadcasted_iota(I32, (1, LANES), 1) < B_V_DIM
    o_ref[...] = jnp.where(lo, a0[...] / l0[...], a1[...] / l1[...]).astype(BF16)


def _attn_b(qb, kb, vb):
    s = qb.shape[0]
    t = ATT_TILE
    stat = pltpu.VMEM((t, 1), F32)
    acc = pltpu.VMEM((t, LANES), F32)
    return pl.pallas_call(
        _attn_b_kernel,
        grid=(B_HEADS // 2, s // t),
        in_specs=[pl.BlockSpec((t, 2 * LANES), lambda p, i: (i, p)),
                  pl.BlockSpec((s, 2 * LANES), lambda p, i: (0, p)),
                  pl.BlockSpec((s, LANES), lambda p, i: (0, p)),
                  _full((t, t))],
        out_specs=pl.BlockSpec((t, LANES), lambda p, i: (i, p)),
        out_shape=jax.ShapeDtypeStruct((s, B_HEADS * B_V_DIM), BF16),
        scratch_shapes=[stat, stat, acc, stat, stat, acc],
        compiler_params=_params("arbitrary", "arbitrary"),
        name="attn_b",
    )(qb, kb, vb, _chunk_causal_bias(t))


def _attn_c_kernel(lam_ref, q_ref, k_ref, v_ref, d0_ref, d1_ref, g_ref, o_ref, m_ref, l_ref, acc_ref,
                   *, out_scale):
    i = pl.program_id(1)
    t = ATT_TILE
    m_ref[...] = jnp.full(m_ref.shape, NEG_INF, F32)
    l_ref[...] = jnp.zeros(l_ref.shape, F32)
    acc_ref[...] = jnp.zeros(acc_ref.shape, F32)
    q = q_ref[...]
    lo = lax.broadcasted_iota(I32, (1, LANES), 1) < C_HEAD_DIM
    zero = jnp.zeros_like(q)
    q2 = jnp.concatenate([jnp.where(lo, q, zero), jnp.where(lo, zero, q)], axis=0)

    def step(kt, bias):
        r0 = pl.multiple_of(kt * t, t)
        _flash_update(q2, k_ref[pl.ds(r0, t), :], v_ref[pl.ds(r0, t), :], bias, m_ref, l_ref, acc_ref)

    def body(kt, carry):
        step(kt, None)
        return carry

    lax.fori_loop(0, jnp.maximum(i - 1, 0), body, 0)

    @pl.when(i > 0)
    def _():
        d1 = d1_ref[0]
        step(i - 1, jnp.concatenate([d1, d1], axis=0))

    d0 = d0_ref[0]
    step(i, jnp.concatenate([d0, d0], axis=0))

    o = acc_ref[...] / l_ref[...]
    o = o[:t] - lam_ref[0] * o[t:]
    inv = lax.rsqrt(jnp.mean(o * o, axis=-1, keepdims=True) + RMS_EPS)
    o_ref[...] = (o * inv * g_ref[...] * out_scale).astype(BF16)


def _t5_bucket(rel):
    half = T5_BUCKETS // 2
    max_exact = half // 2
    ret = jnp.where(rel > 0, half, 0)
    n = jnp.abs(rel)
    large = max_exact + (jnp.log(jnp.maximum(n, max_exact).astype(F32) / max_exact)
                         / math.log(T5_MAX_DIST / max_exact) * (half - max_exact)).astype(I32)
    large = jnp.minimum(large, half - 1)
    return ret + jnp.where(n < max_exact, n, large)


def _t5_tiles(t5_table, t):
    assert t >= T5_MAX_DIST
    far_bucket = T5_BUCKETS // 2 - 1
    r = jnp.arange(t, dtype=I32)
    rel0 = r[None, :] - r[:, None]
    rel1 = rel0 - t
    tab = t5_table.astype(F32) - t5_table[far_bucket].astype(F32)[None, :]
    d0 = jnp.transpose(tab[_t5_bucket(rel0)], (2, 0, 1))
    d1 = jnp.transpose(tab[_t5_bucket(rel1)], (2, 0, 1))
    ok = (r[None, :] // CHUNK) <= (r[:, None] // CHUNK)
    d0 = jnp.where(ok[None], d0, NEG_INF)
    return d0, d1


def _attn_c(q, k, v, lam, subln_g, t5_table, lam_init):
    s = q.shape[0]
    t = ATT_TILE
    d0, d1 = _t5_tiles(t5_table, t)
    grid_spec = pltpu.PrefetchScalarGridSpec(
        num_scalar_prefetch=0,
        grid=(C_HEADS, s // t),
        in_specs=[pl.BlockSpec(memory_space=pltpu.SMEM),
                  pl.BlockSpec((t, LANES), lambda h, i: (i, h)),
                  pl.BlockSpec((s, LANES), lambda h, i: (0, h)),
                  pl.BlockSpec((s, LANES), lambda h, i: (0, h)),
                  pl.BlockSpec((1, t, t), lambda h, i: (h, 0, 0)),
                  pl.BlockSpec((1, t, t), lambda h, i: (h, 0, 0)),
                  _full((1, LANES))],
        out_specs=pl.BlockSpec((t, LANES), lambda h, i: (i, h)),
        scratch_shapes=[pltpu.VMEM((2 * t, 1), F32), pltpu.VMEM((2 * t, 1), F32),
                        pltpu.VMEM((2 * t, LANES), F32)],
    )
    return pl.pallas_call(
        functools.partial(_attn_c_kernel, out_scale=1.0 - lam_init),
        grid_spec=grid_spec,
        out_shape=jax.ShapeDtypeStruct((s, C_HEADS * 2 * C_HEAD_DIM), BF16),
        compiler_params=_params("arbitrary", "arbitrary"),
        name="attn_c",
    )(lam.reshape(1), q, k, v, d0, d1, subln_g.reshape(1, -1))


def _prep_c_kernel(x_ref, w_ref, q_ref, k_ref, v_ref):
    xb = x_ref[...].astype(BF16)
    h = jnp.dot(xb, w_ref[...], preferred_element_type=F32)
    q_ref[...] = (h[:, :C_QK] * (C_HEAD_DIM ** -0.5)).astype(BF16)
    k_ref[...] = h[:, C_QK:2 * C_QK].astype(BF16)
    v_ref[...] = h[:, 2 * C_QK:].astype(BF16)


def _prep_c(x, w_in):
    s = x.shape[0]
    t = ROW_TILE
    w = w_in.astype(BF16)
    row = pl.BlockSpec((t, D_MODEL), lambda i: (i, 0))
    return pl.pallas_call(
        _prep_c_kernel,
        grid=(s // t,),
        in_specs=[row, _full(w.shape)],
        out_specs=[row, row, row],
        out_shape=[jax.ShapeDtypeStruct((s, C_QK), BF16)] * 3,
        compiler_params=_params("arbitrary"),
        name="prep_c",
    )(x, w)


def _post_attn_kernel(*refs, n_in):
    x_ref = refs[0]
    o_refs = refs[1:1 + n_in]
    w_refs = refs[1 + n_in:1 + 2 * n_in]
    g_ref, b_ref, out_ref = refs[1 + 2 * n_in:]
    y = DEEPNORM_ALPHA * x_ref[...]
    for o_ref, w_ref in zip(o_refs, w_refs):
        y = y + jnp.dot(o_ref[...], w_ref[...], preferred_element_type=F32)
    out_ref[...] = _layer_norm(y, g_ref[...], b_ref[...])


def _post_attn(x, outs, weights, g, b):
    s = x.shape[0]
    t = ROW_TILE
    row = lambda w: pl.BlockSpec((t, w), lambda i: (i, 0))
    ws = [w.astype(BF16) for w in weights]
    return pl.pallas_call(
        functools.partial(_post_attn_kernel, n_in=len(outs)),
        grid=(s // t,),
        in_specs=[row(D_MODEL)] + [row(o.shape[1]) for o in outs] + [_full(w.shape) for w in ws]
        + [_full((1, D_MODEL))] * 2,
        out_specs=row(D_MODEL),
        out_shape=jax.ShapeDtypeStruct((s, D_MODEL), F32),
        compiler_params=_params("arbitrary"),
        name="post_attn",
    )(x, *outs, *ws, g.reshape(1, -1), b.reshape(1, -1))


def _router_kernel(x_ref, wr_ref, br_ref, tri_ref, idx_ref, rank_ref, w_ref, cnt_ref, carry):
    step = pl.program_id(0)

    @pl.when(step == 0)
    def _():
        carry[...] = jnp.zeros(carry.shape, F32)

    t = ROUTER_TILE
    logits = lax.dot_general(wr_ref[...], x_ref[...], NT_DIMS, preferred_element_type=F32,
                             precision=lax.Precision.HIGHEST) + br_ref[...]
    eid = lax.broadcasted_iota(I32, (N_EXPERTS, t), 0).astype(F32)
    work = logits
    vals, sels = [], []
    for _ in range(TOP_K):
        m = jnp.max(work, axis=0, keepdims=True)
        sel = jnp.min(jnp.where(work == m, eid, float(N_EXPERTS)), axis=0, keepdims=True)
        vals.append(m)
        sels.append(sel)
        work = jnp.where(eid == sel, -jnp.inf, work)
    exps = [jnp.exp(v - vals[0]) for v in vals]
    denom = exps[0] + exps[1] + exps[2] + exps[3]

    chosen = jnp.zeros((N_EXPERTS, t), F32)
    for sel in sels:
        chosen = chosen + (eid == sel).astype(F32)
    before = jnp.dot(chosen.astype(BF16), tri_ref[...], preferred_element_type=F32) + carry[...]
    for kk, sel in enumerate(sels):
        rank = jnp.sum(jnp.where(eid == sel, before, 0.0), axis=0, keepdims=True)
        rank_ref[kk:kk + 1, :] = rank.astype(I32)
        idx_ref[kk:kk + 1, :] = sel.astype(I32)
    carry[...] = carry[...] + jnp.sum(chosen, axis=1, keepdims=True)
    cnt_ref[...] = jnp.broadcast_to(carry[...], cnt_ref.shape)

    sub = lax.broadcasted_iota(I32, (LANES, t), 0)
    wrows = jnp.zeros((LANES, t), F32)
    for kk in range(TOP_K):
        wrows = jnp.where(sub == kk, exps[kk] / denom, wrows)
    w_ref[...] = wrows.T


def _router(x1, router_w, router_b):
    s = x1.shape[0]
    t = ROUTER_TILE
    tri = jnp.asarray(np.triu(np.ones((t, t), np.float32), k=1), BF16)
    tok = pl.BlockSpec((TOP_K, t), lambda i: (0, i))
    return pl.pallas_call(
        _router_kernel,
        grid=(s // t,),
        in_specs=[pl.BlockSpec((t, D_MODEL), lambda i: (i, 0)),
                  _full((N_EXPERTS, D_MODEL)), _full((N_EXPERTS, 1)), _full((t, t))],
        out_specs=[tok, tok, pl.BlockSpec((t, LANES), lambda i: (i, 0)), _full((N_EXPERTS, LANES))],
        out_shape=[jax.ShapeDtypeStruct((TOP_K, s), I32), jax.ShapeDtypeStruct((TOP_K, s), I32),
                   jax.ShapeDtypeStruct((s, LANES), F32), jax.ShapeDtypeStruct((N_EXPERTS, LANES), F32)],
        scratch_shapes=[pltpu.VMEM((N_EXPERTS, 1), F32)],
        compiler_params=_params("arbitrary"),
        name="router",
    )(x1, router_w.T, router_b.reshape(-1, 1), tri)


def _dispatch_kernel(pos_hbm, x_hbm, init_hbm, xs_hbm, pos_smem, pos_sem, row_sem):
    del init_hbm
    i = pl.program_id(0)
    n = DISPATCH_TILE * TOP_K
    cp = pltpu.make_async_copy(pos_hbm.at[pl.ds(pl.multiple_of(i * n, n), n)], pos_smem, pos_sem)
    cp.start()
    cp.wait()

    def row_copy(e):
        t = i * DISPATCH_TILE + e // TOP_K
        return pltpu.make_async_copy(x_hbm.at[pl.ds(t, 1)], xs_hbm.at[pl.ds(pos_smem[e], 1)], row_sem)

    def issue(e, carry):
        row_copy(e).start()
        return carry

    def drain(e, carry):
        row_copy(e).wait()
        return carry

    lax.fori_loop(0, n, issue, 0)
    lax.fori_loop(0, n, drain, 0)


def _dispatch(pos_flat, x1, n_rows):
    s = x1.shape[0]
    init = jnp.zeros((n_rows, D_MODEL), F32)
    any_spec = pl.BlockSpec(memory_space=pl.ANY)
    return pl.pallas_call(
        _dispatch_kernel,
        grid=(s // DISPATCH_TILE,),
        in_specs=[any_spec, any_spec, any_spec],
        out_specs=any_spec,
        out_shape=jax.ShapeDtypeStruct((n_rows, D_MODEL), F32),
        scratch_shapes=[pltpu.SMEM((DISPATCH_TILE * TOP_K,), I32),
                        pltpu.SemaphoreType.DMA, pltpu.SemaphoreType.DMA],
        input_output_aliases={2: 0},
        compiler_params=_params("arbitrary"),
        name="dispatch",
    )(pos_flat, x1, init)


def _gmm_kernel(te_ref, nt_ref, xs_ref, wi_ref, bi_ref, wo_ref, bo_ref, y_ref, wi_bf, wo_bf):
    i = pl.program_id(0)
    prev = te_ref[jnp.maximum(i - 1, 0)]
    fresh = jnp.logical_or(i == 0, te_ref[i] != prev)

    @pl.when(jnp.logical_and(i < nt_ref[0], fresh))
    def _():
        def cast(c, carry):
            rows = pl.ds(pl.multiple_of(c * CAST_ROWS, CAST_ROWS), CAST_ROWS)
            wi_bf[rows, :] = wi_ref[0, rows, :].astype(BF16)
            wo_bf[rows, :] = wo_ref[0, rows, :].astype(BF16)
            return carry

        lax.fori_loop(0, D_MODEL // CAST_ROWS, cast, 0)

    @pl.when(i < nt_ref[0])
    def _():
        xb = xs_ref[...].astype(BF16)
        h = jnp.dot(xb, wi_bf[...], preferred_element_type=F32) + bi_ref[0]
        glu = jnp.minimum(h[:, :D_FF], SWIGLU_LIMIT)
        lin = jnp.clip(h[:, D_FF:], -SWIGLU_LIMIT, SWIGLU_LIMIT)
        act = glu * jax.nn.sigmoid(SWIGLU_ALPHA * glu) * (lin + 1.0)
        y_ref[...] = jnp.dot(act.astype(BF16), wo_bf[...], preferred_element_type=F32) + bo_ref[0]


def _gmm(layer, tile_expert, n_tiles, xs, w_in, b_in, w_out, b_out):
    n_rows = xs.shape[0]
    tm = EXPERT_TILE
    last = lambda i, nt: jnp.minimum(i, nt[0] - 1)
    expert = lambda i, te, nt: layer * N_EXPERTS + te[last(i, nt)]
    grid_spec = pltpu.PrefetchScalarGridSpec(
        num_scalar_prefetch=2,
        grid=(n_rows // tm,),
        in_specs=[pl.BlockSpec((tm, D_MODEL), lambda i, te, nt: (last(i, nt), 0)),
                  pl.BlockSpec((1, D_MODEL, 2 * D_FF), lambda i, te, nt: (expert(i, te, nt), 0, 0)),
                  pl.BlockSpec((1, 1, 2 * D_FF), lambda i, te, nt: (expert(i, te, nt), 0, 0)),
                  pl.BlockSpec((1, D_FF, D_MODEL), lambda i, te, nt: (expert(i, te, nt), 0, 0)),
                  pl.BlockSpec((1, 1, D_MODEL), lambda i, te, nt: (expert(i, te, nt), 0, 0))],
        out_specs=pl.BlockSpec((tm, D_MODEL), lambda i, te, nt: (last(i, nt), 0)),
        scratch_shapes=[pltpu.VMEM((D_MODEL, 2 * D_FF), BF16), pltpu.VMEM((D_FF, D_MODEL), BF16)],
    )
    return pl.pallas_call(
        _gmm_kernel,
        grid_spec=grid_spec,
        out_shape=jax.ShapeDtypeStruct((n_rows, D_MODEL), F32),
        compiler_params=_params("arbitrary"),
        name="expert_mlp",
    )(tile_expert, n_tiles, xs, w_in, b_in, w_out, b_out)


def _combine_kernel(pos_hbm, y_hbm, x_ref, w_ref, p_ref, wg_ref, wp_ref, g_ref, b_ref, out_ref,
                    pos_smem, ybuf, pos_sem, row_sem):
    i = pl.program_id(0)
    t = COMBINE_TILE
    n = t * TOP_K
    cp = pltpu.make_async_copy(pos_hbm.at[pl.ds(pl.multiple_of(i * n, n), n)], pos_smem, pos_sem)
    cp.start()
    cp.wait()

    def row_copy(e):
        tok = e // TOP_K
        kk = e % TOP_K
        return pltpu.make_async_copy(y_hbm.at[pl.ds(pos_smem[e], 1)], ybuf.at[kk, pl.ds(tok, 1)], row_sem)

    def issue(e, carry):
        row_copy(e).start()
        return carry

    def drain(e, carry):
        row_copy(e).wait()
        return carry

    lax.fori_loop(0, n, issue, 0)

    x = x_ref[...]
    gate = jax.nn.sigmoid(jnp.dot(x.astype(BF16), wg_ref[...], preferred_element_type=F32))
    proj = jnp.dot(p_ref[...].astype(BF16), wp_ref[...], preferred_element_type=F32)
    y = DEEPNORM_ALPHA * x + gate * proj

    lax.fori_loop(0, n, drain, 0)
    w = w_ref[...]
    for kk in range(TOP_K):
        y = y + w[:, kk:kk + 1] * ybuf[kk]
    out_ref[...] = _layer_norm(y, g_ref[...], b_ref[...])


def _combine(pos_flat, y, x1, wcol, p, gate_w, proj_w, g, b):
    s = x1.shape[0]
    t = COMBINE_TILE
    any_spec = pl.BlockSpec(memory_space=pl.ANY)
    row = lambda w: pl.BlockSpec((t, w), lambda i: (i, 0))
    return pl.pallas_call(
        _combine_kernel,
        grid=(s // t,),
        in_specs=[any_spec, any_spec, row(D_MODEL), row(LANES), row(PLE_DIM),
                  _full((D_MODEL, D_MODEL)), _full((PLE_DIM, D_MODEL)),
                  _full((1, D_MODEL)), _full((1, D_MODEL))],
        out_specs=row(D_MODEL),
        out_shape=jax.ShapeDtypeStruct((s, D_MODEL), F32),
        scratch_shapes=[pltpu.SMEM((t * TOP_K,), I32), pltpu.VMEM((TOP_K, t, D_MODEL), F32),
                        pltpu.SemaphoreType.DMA, pltpu.SemaphoreType.DMA],
        compiler_params=_params("arbitrary"),
        name="combine",
    )(pos_flat, y, x1, wcol, p, gate_w.astype(BF16), proj_w.astype(BF16), g.reshape(1, -1), b.reshape(1, -1))


def _moe_block(layer, x1, p, router_w, router_b, w_in, b_in, w_out, b_out, gate_w, proj_w, g, b):
    s = x1.shape[0]
    tm = EXPERT_TILE
    n_tiles_max = s * TOP_K // tm + N_EXPERTS
    idx, rank, wcol, cnt = _router(x1, router_w, router_b)
    counts = cnt[:, 0].astype(I32)
    tiles_per = (counts + tm - 1) // tm
    tile_end = jnp.cumsum(tiles_per)
    offsets = (tile_end - tiles_per) * tm
    pos = offsets[idx] + rank
    pos_flat = pos.T.reshape(-1)
    tile_expert = jnp.minimum(jnp.searchsorted(tile_end, jnp.arange(n_tiles_max, dtype=I32), side="right"),
                              N_EXPERTS - 1).astype(I32)
    n_tiles = tile_end[-1:].astype(I32)
    xs = _dispatch(pos_flat, x1, n_tiles_max * tm)
    y = _gmm(layer, tile_expert, n_tiles, xs, w_in, b_in, w_out, b_out)
    return _combine(pos_flat, y, x1, wcol, p, gate_w, proj_w, g, b)


def _lambda_init(layer_idx):
    return 0.8 - 0.6 * math.exp(-0.3 * layer_idx)


def kernel(x, p, ab_w_in, ab_rel_bias, ab_q_norm, ab_kv_norm, ab_w_uq, ab_w_ukv, ab_w_out, c_w_in, c_lambda, c_subln, c_w_out, t5_table, ln_mix_g, ln_mix_b, ln_ffn_g, ln_ffn_b, router_w, router_b, exp_w_in, exp_b_in, exp_w_out, exp_b_out, ple_gate_w, ple_proj_w):
    batch, s, _ = x.shape
    assert batch == 1 and s % DISPATCH_TILE == 0
    xr = x[0]
    w_in_all = exp_w_in.reshape(DEPTH * N_EXPERTS, D_MODEL, 2 * D_FF)
    b_in_all = exp_b_in.reshape(DEPTH * N_EXPERTS, 1, 2 * D_FF)
    w_out_all = exp_w_out.reshape(DEPTH * N_EXPERTS, D_FF, D_MODEL)
    b_out_all = exp_b_out.reshape(DEPTH * N_EXPERTS, 1, D_MODEL)
    for i in range(DEPTH):
        j = i // 2
        if i % 2 == 0:
            qa, ka, va, qb, kb, vb = _prep_ab(xr, ab_w_in[j], ab_q_norm[j], ab_kv_norm[j], ab_w_uq[j], ab_w_ukv[j])
            o_a = _attn_a(qa, ka, va, ab_rel_bias[j])
            o_b = _attn_b(qb, kb, vb)
            x1 = _post_attn(xr, [o_a, o_b], [ab_w_out[j][:A_WIDTH], ab_w_out[j][A_WIDTH:]],
                            ln_mix_g[i], ln_mix_b[i])
        else:
            q, k, v = _prep_c(xr, c_w_in[j])
            lp = c_lambda[j].astype(F32)
            lam_init = _lambda_init(i)
            lam = jnp.exp(jnp.sum(lp[0] * lp[1])) - jnp.exp(jnp.sum(lp[2] * lp[3])) + lam_init
            o_c = _attn_c(q, k, v, lam, c_subln[j], t5_table, lam_init)
            x1 = _post_attn(xr, [o_c], [c_w_out[j]], ln_mix_g[i], ln_mix_b[i])
        xr = _moe_block(i, x1, p[i, 0], router_w[i], router_b[i], w_in_all, b_in_all, w_out_all, b_out_all,
                        ple_gate_w[i], ple_proj_w[i], ln_ffn_g[i], ln_ffn_b[i])
    return xr[None]
```

```python
import functools
import math

import jax
import jax.numpy as jnp
import numpy as np
from jax import lax
from jax.experimental import pallas as pl
from jax.experimental.pallas import tpu as pltpu

F32 = jnp.float32
BF16 = jnp.bfloat16
I32 = jnp.int32

D_MODEL = 1024
DEPTH = 4
CHUNK = 64
PLE_DIM = 256

A_HEADS = 8
A_HEAD_DIM = 64
A_LEFT_CHUNKS = 8
A_REL_CLIP = 128
A_WIDTH = A_HEADS * A_HEAD_DIM

B_HEADS = 8
B_NOPE_DIM = 64
B_ROPE_DIM = 32
B_V_DIM = 64
B_Q_RANK = 512
B_KV_RANK = 256
ROPE_BASE = 10000.0

C_HEADS = 8
C_HEAD_DIM = 64
C_QK = 2 * C_HEADS * C_HEAD_DIM

T5_BUCKETS = 32
T5_MAX_DIST = 128

N_EXPERTS = 32
TOP_K = 4
D_FF = 1024
SWIGLU_ALPHA = 1.702
SWIGLU_LIMIT = 7.0

DEEPNORM_ALPHA = (2.0 * DEPTH) ** 0.25
LN_EPS = 1e-5
RMS_EPS = 1e-6
NEG_INF = -1e30

LANES = 128
VMEM_LIMIT = 56 * 1024 * 1024

ATT_TILE = 512
COL_CHUNK = 256
BAND_GROUP = 256
BAND_WINDOW = BAND_GROUP + A_LEFT_CHUNKS * CHUNK
ROUTER_TILE = 512
EXPERT_TILE = 256
SRC_GROUP = 4
COMBINE_TILE = 256
CAST_ROWS = 128

NT_DIMS = (((1,), (1,)), ((), ()))


def _params(*sem):
    return pltpu.CompilerParams(dimension_semantics=sem, vmem_limit_bytes=VMEM_LIMIT)


def _full(shape):
    n = len(shape)
    return pl.BlockSpec(shape, lambda *_: (0,) * n)


def _layer_norm(y, g, b):
    mu = jnp.mean(y, axis=-1, keepdims=True)
    d = y - mu
    var = jnp.mean(d * d, axis=-1, keepdims=True)
    return d * lax.rsqrt(var + LN_EPS) * g + b


def _rms_norm(c, g):
    inv = lax.rsqrt(jnp.mean(c * c, axis=-1, keepdims=True) + RMS_EPS)
    return c * inv * g


def _toeplitz(vals_pos, vals_neg, nr, nc):
    w = jnp.concatenate([vals_pos, vals_neg], axis=-1)
    lw = nc + nr
    assert w.shape[-1] == lw
    flat = jnp.tile(w, (1,) * (w.ndim - 1) + (nr,))[..., :nr * (lw - 1)]
    return flat.reshape(w.shape[:-1] + (nr, lw - 1))[..., :nc]


def _prep_ab_kernel(x_ref, wa_ref, wcq_ref, wckv_ref, wkr_ref, wkrr_ref, qn_ref, kvn_ref,
                    wuqa_ref, wuqb_ref, wukn_ref, wuv_ref, place_ref, ctab_ref, stab_ref,
                    cos_ref, sin_ref,
                    qa_ref, ka_ref, va_ref, qbt_ref, kb_ref, vbt_ref):
    xb = x_ref[...].astype(BF16)
    ha = jnp.dot(xb, wa_ref[...], preferred_element_type=F32)
    qa_ref[...] = (ha[:, :A_WIDTH] * (A_HEAD_DIM ** -0.5)).astype(BF16)
    ka_ref[...] = ha[:, A_WIDTH:2 * A_WIDTH].astype(BF16)
    va_ref[...] = ha[:, 2 * A_WIDTH:].astype(BF16)

    cq = jnp.dot(xb, wcq_ref[...], preferred_element_type=F32)
    cqn = _rms_norm(cq, qn_ref[...]).astype(BF16)
    qa_part = lax.dot_general(wuqa_ref[...], cqn, NT_DIMS, preferred_element_type=F32)
    qb_part = lax.dot_general(wuqb_ref[...], cqn, NT_DIMS, preferred_element_type=F32)
    ct = ctab_ref[...]
    st = stab_ref[...]
    scale = (B_NOPE_DIM + B_ROPE_DIM) ** -0.5
    for h in range(B_HEADS):
        sl = slice(h * LANES, (h + 1) * LANES)
        qbt_ref[0, sl, :] = ((qa_part[sl, :] * ct + qb_part[sl, :] * st) * scale).astype(BF16)

    ckv = jnp.dot(xb, wckv_ref[...], preferred_element_type=F32)
    ckvn = _rms_norm(ckv, kvn_ref[...]).astype(BF16)
    kr = jnp.dot(xb, wkr_ref[...], preferred_element_type=F32)
    krr = jnp.dot(xb, wkrr_ref[...], preferred_element_type=F32)
    k_rope = (kr * cos_ref[...] + krr * sin_ref[...]).astype(BF16)
    kb = jnp.dot(ckvn, wukn_ref[...], preferred_element_type=F32)
    kb = kb + jnp.dot(k_rope, place_ref[...], preferred_element_type=F32)
    kb_ref[...] = kb.astype(BF16)
    vbt_ref[0] = lax.dot_general(wuv_ref[...], ckvn, NT_DIMS, preferred_element_type=F32).astype(BF16)


def _rot_half_cols(w):
    half = w.shape[-1] // 2
    return jnp.concatenate([-w[..., half:], w[..., :half]], axis=-1)


def _prep_ab(x, w_in, q_norm, kv_norm, w_uq, w_ukv):
    s = x.shape[0]
    t = ATT_TILE
    splits = [3 * A_WIDTH, 3 * A_WIDTH + B_Q_RANK, 3 * A_WIDTH + B_Q_RANK + B_KV_RANK]
    wa = w_in[:, :splits[0]].astype(BF16)
    wcq = w_in[:, splits[0]:splits[1]].astype(BF16)
    wckv = w_in[:, splits[1]:splits[2]].astype(BF16)
    wkr_f = w_in[:, splits[2]:]
    wkr = wkr_f.astype(BF16)
    wkrr = _rot_half_cols(wkr_f).astype(BF16)

    qd = B_NOPE_DIM + B_ROPE_DIM
    w_uq3 = w_uq.reshape(B_Q_RANK, B_HEADS, qd)
    zpad = jnp.zeros((B_Q_RANK, B_HEADS, LANES - qd), F32)
    wuqa = jnp.concatenate([w_uq3, zpad], axis=-1).reshape(B_Q_RANK, B_HEADS * LANES).T.astype(BF16)
    wuqb = jnp.concatenate([jnp.zeros((B_Q_RANK, B_HEADS, B_NOPE_DIM), F32),
                            _rot_half_cols(w_uq3[..., B_NOPE_DIM:]), zpad],
                           axis=-1).reshape(B_Q_RANK, B_HEADS * LANES).T.astype(BF16)
    w_ukv3 = w_ukv.reshape(B_KV_RANK, B_HEADS, B_NOPE_DIM + B_V_DIM)
    wukn = jnp.concatenate([w_ukv3[..., :B_NOPE_DIM],
                            jnp.zeros((B_KV_RANK, B_HEADS, LANES - B_NOPE_DIM), F32)],
                           axis=-1).reshape(B_KV_RANK, B_HEADS * LANES).astype(BF16)
    wuv = w_ukv3[..., B_NOPE_DIM:].reshape(B_KV_RANK, B_HEADS * B_V_DIM).T.astype(BF16)
    place = np.zeros((B_ROPE_DIM, B_HEADS * LANES), np.float32)
    for h in range(B_HEADS):
        for r in range(B_ROPE_DIM):
            place[r, h * LANES + B_NOPE_DIM + r] = 1.0
    place = jnp.asarray(place, BF16)

    pos = jnp.arange(s, dtype=F32)
    inv_freq = ROPE_BASE ** (-jnp.arange(0, B_ROPE_DIM, 2, dtype=F32) / B_ROPE_DIM)
    ang = pos[:, None] * inv_freq[None, :]
    cos = jnp.concatenate([jnp.cos(ang)] * 2, axis=-1)
    sin = jnp.concatenate([jnp.sin(ang)] * 2, axis=-1)
    ctab = jnp.concatenate([jnp.ones((s, B_NOPE_DIM), F32), cos, jnp.zeros((s, LANES - qd), F32)], axis=-1).T
    stab = jnp.concatenate([jnp.zeros((s, B_NOPE_DIM), F32), sin, jnp.zeros((s, LANES - qd), F32)], axis=-1).T

    row = lambda w: pl.BlockSpec((t, w), lambda i: (i, 0))
    col = pl.BlockSpec((LANES, t), lambda i: (0, i))
    tr = lambda w: pl.BlockSpec((1, w, t), lambda i: (i, 0, 0))
    consts = [wa, wcq, wckv, wkr, wkrr, q_norm.reshape(1, -1), kv_norm.reshape(1, -1),
              wuqa, wuqb, wukn, wuv, place]
    sds = jax.ShapeDtypeStruct
    return pl.pallas_call(
        _prep_ab_kernel,
        grid=(s // t,),
        in_specs=[row(D_MODEL)] + [_full(c.shape) for c in consts]
        + [col, col, row(B_ROPE_DIM), row(B_ROPE_DIM)],
        out_specs=[row(A_WIDTH), row(A_WIDTH), row(A_WIDTH),
                   tr(B_HEADS * LANES), row(B_HEADS * LANES), tr(B_HEADS * B_V_DIM)],
        out_shape=[sds((s, A_WIDTH), BF16)] * 3
        + [sds((s // t, B_HEADS * LANES, t), BF16), sds((s, B_HEADS * LANES), BF16),
           sds((s // t, B_HEADS * B_V_DIM, t), BF16)],
        compiler_params=_params("arbitrary"),
        name="prep_ab",
    )(x, *consts, ctab, stab, cos, sin)


def _attn_a_kernel(q_ref, kp_ref, kc_ref, vp_ref, vc_ref, bias_ref, o_ref, kbuf, vbuf):
    j = pl.program_id(1)
    t = ATT_TILE
    kbuf[0:t, :] = kp_ref[...]
    kbuf[t:2 * t, :] = kc_ref[...]
    vbuf[0:t, :] = vp_ref[...]
    vbuf[t:2 * t, :] = vc_ref[...]
    lo = lax.broadcasted_iota(I32, (1, LANES), 1) < A_HEAD_DIM
    col = lax.broadcasted_iota(I32, (1, BAND_WINDOW), 1)
    for g in range(t // BAND_GROUP):
        r0 = g * BAND_GROUP
        q = q_ref[r0:r0 + BAND_GROUP, :]
        k = kbuf[r0:r0 + BAND_WINDOW, :]
        v = vbuf[r0:r0 + BAND_WINDOW, :]
        valid = col + r0 >= jnp.where(j > 0, 0, t)
        outs = []
        for hh in range(2):
            qm = jnp.where(lo if hh == 0 else jnp.logical_not(lo), q, jnp.zeros_like(q))
            sc = lax.dot_general(qm, k, NT_DIMS, preferred_element_type=F32)
            sc = jnp.where(valid, sc + bias_ref[hh], NEG_INF)
            m = jnp.max(sc, axis=-1, keepdims=True)
            p = jnp.exp(sc - m)
            l = jnp.sum(p, axis=-1, keepdims=True)
            o = jnp.dot(p.astype(BF16), v, preferred_element_type=F32)
            outs.append(o / l)
        o_ref[r0:r0 + BAND_GROUP, :] = jnp.where(lo, outs[0], outs[1]).astype(BF16)


def _band_bias(rel_bias):
    qi = np.arange(BAND_GROUP)[:, None]
    kr = np.arange(BAND_WINDOW)[None, :]
    dchunk = (qi // CHUNK + A_LEFT_CHUNKS) - kr // CHUNK
    allowed = (dchunk >= 0) & (dchunk <= A_LEFT_CHUNKS)
    d_pos = np.arange(BAND_WINDOW)
    d_neg = np.arange(-BAND_GROUP, 0)
    to_idx = lambda d: np.clip(A_LEFT_CHUNKS * CHUNK - d, -A_REL_CLIP, A_REL_CLIP) + A_REL_CLIP
    tab = rel_bias.astype(F32).T
    table = _toeplitz(tab[:, to_idx(d_pos)], tab[:, to_idx(d_neg)], BAND_GROUP, BAND_WINDOW)
    return jnp.where(jnp.asarray(allowed)[None], table, NEG_INF)


def _attn_a(qa, ka, va, rel_bias):
    s = qa.shape[0]
    t = ATT_TILE
    bias = _band_bias(rel_bias)
    cur = pl.BlockSpec((t, LANES), lambda p, j: (j, p))
    prev = pl.BlockSpec((t, LANES), lambda p, j: (jnp.maximum(j - 1, 0), p))
    return pl.pallas_call(
        _attn_a_kernel,
        grid=(A_HEADS // 2, s // t),
        in_specs=[cur, prev, cur, prev, cur,
                  pl.BlockSpec((2, BAND_GROUP, BAND_WINDOW), lambda p, j: (p, 0, 0))],
        out_specs=cur,
        out_shape=jax.ShapeDtypeStruct((s, A_WIDTH), BF16),
        scratch_shapes=[pltpu.VMEM((2 * t, LANES), BF16), pltpu.VMEM((2 * t, LANES), BF16)],
        compiler_params=_params("arbitrary", "arbitrary"),
        name="attn_a",
    )(qa, ka, ka, va, va, bias)


def _flash_update_t(score_fn, vt, bias_fn, m_ref, l_ref, acc_ref):
    n_chunks = m_ref.shape[1] // COL_CHUNK
    scores = [score_fn(0)]
    for c in range(n_chunks):
        if c + 1 < n_chunks:
            scores.append(score_fn(c + 1))
        cs = slice(c * COL_CHUNK, (c + 1) * COL_CHUNK)
        sc = scores[c]
        if bias_fn is not None:
            sc = sc + bias_fn((c * COL_CHUNK) % ATT_TILE)
        m_prev = m_ref[:, cs]
        m_new = jnp.maximum(m_prev, jnp.max(sc, axis=0, keepdims=True))
        alpha = jnp.exp(m_prev - m_new)
        p = jnp.exp(sc - m_new)
        l_ref[:, cs] = alpha * l_ref[:, cs] + jnp.sum(p, axis=0, keepdims=True)
        acc_ref[:, cs] = alpha * acc_ref[:, cs] + jnp.dot(vt, p.astype(BF16), preferred_element_type=F32)
        m_ref[:, cs] = m_new


def _init_flash(m_ref, l_ref, acc_ref):
    m_ref[...] = jnp.full(m_ref.shape, NEG_INF, F32)
    l_ref[...] = jnp.zeros(l_ref.shape, F32)
    acc_ref[...] = jnp.zeros(acc_ref.shape, F32)


def _chunk_causal_bias_t(t):
    r = np.arange(t)
    ok = (r[:, None] // CHUNK) <= (r[None, :] // CHUNK)
    return jnp.asarray(np.where(ok, 0.0, NEG_INF), F32)


def _attn_b_kernel(qt_ref, k_ref, vt_ref, diag_ref, o_ref, m_ref, l_ref, acc_ref):
    i = pl.program_id(1)
    t = ATT_TILE
    _init_flash(m_ref, l_ref, acc_ref)
    per_head = t // COL_CHUNK

    def step(kt, bias_fn):
        r0 = pl.multiple_of(kt * t, t)

        def score(c):
            hh = c // per_head
            c0 = (c % per_head) * COL_CHUNK
            k = k_ref[pl.ds(r0, t), hh * LANES:(hh + 1) * LANES]
            return jnp.dot(k, qt_ref[0, hh * LANES:(hh + 1) * LANES, c0:c0 + COL_CHUNK],
                           preferred_element_type=F32)

        _flash_update_t(score, vt_ref[kt], bias_fn, m_ref, l_ref, acc_ref)

    def body(kt, carry):
        step(kt, None)
        return carry

    lax.fori_loop(0, i, body, 0)
    step(i, lambda c0: diag_ref[:, c0:c0 + COL_CHUNK])
    o2 = acc_ref[...] / l_ref[...]
    ot = jnp.concatenate([o2[:B_V_DIM, :t], o2[B_V_DIM:, t:]], axis=0)
    o_ref[...] = ot.T.astype(BF16)


def _attn_b(qbt, kb, vbt):
    nb, _, t = qbt.shape
    s = nb * t
    return pl.pallas_call(
        _attn_b_kernel,
        grid=(B_HEADS // 2, nb),
        in_specs=[pl.BlockSpec((1, 2 * LANES, t), lambda p, i: (i, p, 0)),
                  pl.BlockSpec((s, 2 * LANES), lambda p, i: (0, p)),
                  pl.BlockSpec((nb, LANES, t), lambda p, i: (0, p, 0)),
                  _full((t, t))],
        out_specs=pl.BlockSpec((t, LANES), lambda p, i: (i, p)),
        out_shape=jax.ShapeDtypeStruct((s, B_HEADS * B_V_DIM), BF16),
        scratch_shapes=[pltpu.VMEM((1, 2 * t), F32), pltpu.VMEM((1, 2 * t), F32),
                        pltpu.VMEM((LANES, 2 * t), F32)],
        compiler_params=_params("arbitrary", "arbitrary"),
        name="attn_b",
    )(qbt, kb, vbt, _chunk_causal_bias_t(t))


def _attn_c_kernel(lam_ref, qt_ref, k_ref, vt_ref, d0_ref, d1_ref, g_ref, o_ref, q2_ref, m_ref, l_ref, acc_ref,
                   *, out_scale):
    i = pl.program_id(1)
    t = ATT_TILE
    _init_flash(m_ref, l_ref, acc_ref)
    d = C_HEAD_DIM
    zero = jnp.zeros((d, t), BF16)
    q2_ref[:d, :t] = qt_ref[0, :d, :]
    q2_ref[d:, :t] = zero
    q2_ref[:d, t:] = zero
    q2_ref[d:, t:] = qt_ref[0, d:, :]

    def step(kt, bias_fn):
        r0 = pl.multiple_of(kt * t, t)

        def score(c):
            return jnp.dot(k_ref[pl.ds(r0, t), :], q2_ref[:, c * COL_CHUNK:(c + 1) * COL_CHUNK],
                           preferred_element_type=F32)

        _flash_update_t(score, vt_ref[kt], bias_fn, m_ref, l_ref, acc_ref)

    def body(kt, carry):
        step(kt, None)
        return carry

    lax.fori_loop(0, jnp.maximum(i - 1, 0), body, 0)

    @pl.when(i > 0)
    def _():
        step(i - 1, lambda c0: d1_ref[0, :, c0:c0 + COL_CHUNK])

    step(i, lambda c0: d0_ref[0, :, c0:c0 + COL_CHUNK])

    o2 = acc_ref[...] / l_ref[...]
    ot = o2[:, :t] - lam_ref[0] * o2[:, t:]
    inv = lax.rsqrt(jnp.mean(ot * ot, axis=0, keepdims=True) + RMS_EPS)
    o_ref[...] = (ot * inv * g_ref[...] * out_scale).T.astype(BF16)


def _t5_bucket(rel):
    half = T5_BUCKETS // 2
    max_exact = half // 2
    ret = jnp.where(rel > 0, half, 0)
    n = jnp.abs(rel)
    large = max_exact + (jnp.log(jnp.maximum(n, max_exact).astype(F32) / max_exact)
                         / math.log(T5_MAX_DIST / max_exact) * (half - max_exact)).astype(I32)
    large = jnp.minimum(large, half - 1)
    return ret + jnp.where(n < max_exact, n, large)


def _t5_tiles_t(t5_table, t):
    assert t >= T5_MAX_DIST
    far_bucket = T5_BUCKETS // 2 - 1
    tab = (t5_table.astype(F32) - t5_table[far_bucket].astype(F32)[None, :]).T
    by_rel = lambda rel: tab[:, _t5_bucket(rel)]
    d_pos = jnp.arange(t, dtype=I32)
    d_neg = jnp.arange(-t, 0, dtype=I32)
    d0 = _toeplitz(by_rel(-d_pos), by_rel(-d_neg), t, t)
    d1 = _toeplitz(by_rel(-d_pos - t), by_rel(-d_neg - t), t, t)
    r = np.arange(t)
    ok = (r[:, None] // CHUNK) <= (r[None, :] // CHUNK)
    d0 = jnp.where(jnp.asarray(ok)[None], d0, NEG_INF)
    return d0, d1


def _attn_c(qt, k, vt, lam, subln_g, t5_table, lam_init):
    nb, _, t = qt.shape
    s = nb * t
    d0, d1 = _t5_tiles_t(t5_table, t)
    return pl.pallas_call(
        functools.partial(_attn_c_kernel, out_scale=1.0 - lam_init),
        grid=(C_HEADS, nb),
        in_specs=[pl.BlockSpec(memory_space=pltpu.SMEM),
                  pl.BlockSpec((1, LANES, t), lambda h, i: (i, h, 0)),
                  pl.BlockSpec((s, LANES), lambda h, i: (0, h)),
                  pl.BlockSpec((nb, LANES, t), lambda h, i: (0, h, 0)),
                  pl.BlockSpec((1, t, t), lambda h, i: (h, 0, 0)),
                  pl.BlockSpec((1, t, t), lambda h, i: (h, 0, 0)),
                  _full((LANES, 1))],
        out_specs=pl.BlockSpec((t, LANES), lambda h, i: (i, h)),
        out_shape=jax.ShapeDtypeStruct((s, C_HEADS * 2 * C_HEAD_DIM), BF16),
        scratch_shapes=[pltpu.VMEM((LANES, 2 * t), BF16), pltpu.VMEM((1, 2 * t), F32),
                        pltpu.VMEM((1, 2 * t), F32), pltpu.VMEM((LANES, 2 * t), F32)],
        compiler_params=_params("arbitrary", "arbitrary"),
        name="attn_c",
    )(lam.reshape(1), qt, k, vt, d0, d1, subln_g.reshape(-1, 1))


def _prep_c_kernel(x_ref, wq_ref, wk_ref, wv_ref, qt_ref, k_ref, vt_ref):
    xb = x_ref[...].astype(BF16)
    qt = lax.dot_general(wq_ref[...], xb, NT_DIMS, preferred_element_type=F32)
    qt_ref[0] = (qt * (C_HEAD_DIM ** -0.5)).astype(BF16)
    k_ref[...] = jnp.dot(xb, wk_ref[...], preferred_element_type=F32).astype(BF16)
    vt_ref[0] = lax.dot_general(wv_ref[...], xb, NT_DIMS, preferred_element_type=F32).astype(BF16)


def _prep_c(x, w_in):
    s = x.shape[0]
    t = ATT_TILE
    wq = w_in[:, :C_QK].T.astype(BF16)
    wk = w_in[:, C_QK:2 * C_QK].astype(BF16)
    wv = w_in[:, 2 * C_QK:].T.astype(BF16)
    row = pl.BlockSpec((t, D_MODEL), lambda i: (i, 0))
    tr = pl.BlockSpec((1, C_QK, t), lambda i: (i, 0, 0))
    sds = jax.ShapeDtypeStruct
    return pl.pallas_call(
        _prep_c_kernel,
        grid=(s // t,),
        in_specs=[row, _full(wq.shape), _full(wk.shape), _full(wv.shape)],
        out_specs=[tr, row, tr],
        out_shape=[sds((s // t, C_QK, t), BF16), sds((s, C_QK), BF16), sds((s // t, C_QK, t), BF16)],
        compiler_params=_params("arbitrary"),
        name="prep_c",
    )(x, wq, wk, wv)


def _post_attn_kernel(*refs, n_in):
    x_ref = refs[0]
    o_refs = refs[1:1 + n_in]
    w_refs = refs[1 + n_in:1 + 2 * n_in]
    g_ref, b_ref, out_ref = refs[1 + 2 * n_in:]
    y = DEEPNORM_ALPHA * x_ref[...]
    for o_ref, w_ref in zip(o_refs, w_refs):
        y = y + jnp.dot(o_ref[...], w_ref[...], preferred_element_type=F32)
    out_ref[...] = _layer_norm(y, g_ref[...], b_ref[...])


def _post_attn(x, outs, weights, g, b):
    s = x.shape[0]
    t = ATT_TILE
    row = lambda w: pl.BlockSpec((t, w), lambda i: (i, 0))
    ws = [w.astype(BF16) for w in weights]
    return pl.pallas_call(
        functools.partial(_post_attn_kernel, n_in=len(outs)),
        grid=(s // t,),
        in_specs=[row(D_MODEL)] + [row(o.shape[1]) for o in outs] + [_full(w.shape) for w in ws]
        + [_full((1, D_MODEL))] * 2,
        out_specs=row(D_MODEL),
        out_shape=jax.ShapeDtypeStruct((s, D_MODEL), F32),
        compiler_params=_params("arbitrary"),
        name="post_attn",
    )(x, *outs, *ws, g.reshape(1, -1), b.reshape(1, -1))


def _router_kernel(x_ref, wr_ref, br_ref, tri_ref, idx_ref, rank_ref, w_ref, cnt_ref, carry):
    step = pl.program_id(0)

    @pl.when(step == 0)
    def _():
        carry[...] = jnp.zeros(carry.shape, F32)

    t = ROUTER_TILE
    logits = lax.dot_general(wr_ref[...], x_ref[...], NT_DIMS, preferred_element_type=F32,
                             precision=lax.Precision.HIGHEST) + br_ref[...]
    eid = lax.broadcasted_iota(I32, (N_EXPERTS, t), 0).astype(F32)
    work = logits
    vals, sels = [], []
    for _ in range(TOP_K):
        m = jnp.max(work, axis=0, keepdims=True)
        sel = jnp.min(jnp.where(work == m, eid, float(N_EXPERTS)), axis=0, keepdims=True)
        vals.append(m)
        sels.append(sel)
        work = jnp.where(eid == sel, -jnp.inf, work)
    exps = [jnp.exp(v - vals[0]) for v in vals]
    denom = exps[0] + exps[1] + exps[2] + exps[3]

    chosen = jnp.zeros((N_EXPERTS, t), F32)
    for sel in sels:
        chosen = chosen + (eid == sel).astype(F32)
    before = jnp.dot(chosen.astype(BF16), tri_ref[...], preferred_element_type=F32) + carry[...]
    for kk, sel in enumerate(sels):
        rank = jnp.sum(jnp.where(eid == sel, before, 0.0), axis=0, keepdims=True)
        rank_ref[kk:kk + 1, :] = rank.astype(I32)
        idx_ref[kk:kk + 1, :] = sel.astype(I32)
    carry[...] = carry[...] + jnp.sum(chosen, axis=1, keepdims=True)
    cnt_ref[...] = jnp.broadcast_to(carry[...], cnt_ref.shape)

    sub = lax.broadcasted_iota(I32, (LANES, t), 0)
    wrows = jnp.zeros((LANES, t), F32)
    for kk in range(TOP_K):
        wrows = jnp.where(sub == kk, exps[kk] / denom, wrows)
    w_ref[...] = wrows.T


def _router(x1, router_w, router_b):
    s = x1.shape[0]
    t = ROUTER_TILE
    tri = jnp.asarray(np.triu(np.ones((t, t), np.float32), k=1), BF16)
    tok = pl.BlockSpec((TOP_K, t), lambda i: (0, i))
    return pl.pallas_call(
        _router_kernel,
        grid=(s // t,),
        in_specs=[pl.BlockSpec((t, D_MODEL), lambda i: (i, 0)),
                  _full((N_EXPERTS, D_MODEL)), _full((N_EXPERTS, 1)), _full((t, t))],
        out_specs=[tok, tok, pl.BlockSpec((t, LANES), lambda i: (i, 0)), _full((N_EXPERTS, LANES))],
        out_shape=[jax.ShapeDtypeStruct((TOP_K, s), I32), jax.ShapeDtypeStruct((TOP_K, s), I32),
                   jax.ShapeDtypeStruct((s, LANES), F32), jax.ShapeDtypeStruct((N_EXPERTS, LANES), F32)],
        scratch_shapes=[pltpu.VMEM((N_EXPERTS, 1), F32)],
        compiler_params=_params("arbitrary"),
        name="router",
    )(x1, router_w.T, router_b.reshape(-1, 1), tri)


def _gmm_kernel(te_ref, nt_ref, src_hbm, x_hbm, wi_ref, bi_ref, wo_ref, bo_ref, y_ref,
                wi_bf, wo_bf, xbuf, src_smem, src_sem, row_sem):
    i = pl.program_id(0)
    n_tiles = nt_ref[0]
    tm = EXPERT_TILE
    group = SRC_GROUP * tm

    def half_base(tile):
        return (lax.shift_right_logical(tile, SRC_GROUP.bit_length() - 1) & 1) * group

    def fetch_indices(tile):
        dst = pl.multiple_of(half_base(tile), group)
        cp = pltpu.make_async_copy(src_hbm.at[pl.ds(pl.multiple_of(tile * tm, group), group)],
                                   src_smem.at[pl.ds(dst, group)], src_sem)
        cp.start()
        cp.wait()

    def row_copy(tile, r):
        slot = tile & 1
        tok = src_smem[half_base(tile) + (tile & (SRC_GROUP - 1)) * tm + r]
        return pltpu.make_async_copy(x_hbm.at[pl.ds(tok, 1)], xbuf.at[slot, pl.ds(r, 1)], row_sem.at[slot])

    def issue_tile(tile):
        def body(r, carry):
            row_copy(tile, r).start()
            return carry

        lax.fori_loop(0, tm, body, 0, unroll=8)

    def wait_tile(tile):
        def body(r, carry):
            row_copy(tile, r).wait()
            return carry

        lax.fori_loop(0, tm, body, 0, unroll=8)

    @pl.when(i == 0)
    def _():
        fetch_indices(i)
        issue_tile(i)

    nxt = i + 1

    @pl.when(nxt < n_tiles)
    def _():
        @pl.when((nxt & (SRC_GROUP - 1)) == 0)
        def _():
            fetch_indices(nxt)

        issue_tile(nxt)

    prev = te_ref[jnp.maximum(i - 1, 0)]
    fresh = jnp.logical_or(i == 0, te_ref[i] != prev)

    @pl.when(jnp.logical_and(i < n_tiles, fresh))
    def _():
        def cast(c, carry):
            rows = pl.ds(pl.multiple_of(c * CAST_ROWS, CAST_ROWS), CAST_ROWS)
            wi_bf[rows, :] = wi_ref[0, rows, :].astype(BF16)
            wo_bf[rows, :] = wo_ref[0, rows, :].astype(BF16)
            return carry

        lax.fori_loop(0, D_MODEL // CAST_ROWS, cast, 0)

    @pl.when(i < n_tiles)
    def _():
        wait_tile(i)
        xb = xbuf[i & 1].astype(BF16)
        h = jnp.dot(xb, wi_bf[...], preferred_element_type=F32) + bi_ref[0]
        glu = jnp.minimum(h[:, :D_FF], SWIGLU_LIMIT)
        lin = jnp.clip(h[:, D_FF:], -SWIGLU_LIMIT, SWIGLU_LIMIT)
        act = glu * jax.nn.sigmoid(SWIGLU_ALPHA * glu) * (lin + 1.0)
        y_ref[...] = jnp.dot(act.astype(BF16), wo_bf[...], preferred_element_type=F32) + bo_ref[0]

    @pl.when(i >= n_tiles)
    def _():
        y_ref[...] = jnp.zeros(y_ref.shape, F32)


def _gmm(layer, tile_expert, n_tiles, src, x1, w_in, b_in, w_out, b_out):
    n_rows = src.shape[0]
    tm = EXPERT_TILE
    assert n_rows % (SRC_GROUP * tm) == 0
    last = lambda i, nt: jnp.minimum(i, nt[0] - 1)
    expert = lambda i, te, nt: layer * N_EXPERTS + te[last(i, nt)]
    any_spec = pl.BlockSpec(memory_space=pl.ANY)
    grid_spec = pltpu.PrefetchScalarGridSpec(
        num_scalar_prefetch=2,
        grid=(n_rows // tm,),
        in_specs=[any_spec, any_spec,
                  pl.BlockSpec((1, D_MODEL, 2 * D_FF), lambda i, te, nt: (expert(i, te, nt), 0, 0)),
                  pl.BlockSpec((1, 1, 2 * D_FF), lambda i, te, nt: (expert(i, te, nt), 0, 0)),
                  pl.BlockSpec((1, D_FF, D_MODEL), lambda i, te, nt: (expert(i, te, nt), 0, 0)),
                  pl.BlockSpec((1, 1, D_MODEL), lambda i, te, nt: (expert(i, te, nt), 0, 0))],
        out_specs=pl.BlockSpec((tm, D_MODEL), lambda i, te, nt: (i, 0)),
        scratch_shapes=[pltpu.VMEM((D_MODEL, 2 * D_FF), BF16), pltpu.VMEM((D_FF, D_MODEL), BF16),
                        pltpu.VMEM((2, tm, D_MODEL), F32), pltpu.SMEM((2 * SRC_GROUP * tm,), I32),
                        pltpu.SemaphoreType.DMA, pltpu.SemaphoreType.DMA((2,))],
    )
    return pl.pallas_call(
        _gmm_kernel,
        grid_spec=grid_spec,
        out_shape=jax.ShapeDtypeStruct((n_rows, D_MODEL), F32),
        compiler_params=_params("arbitrary"),
        name="expert_mlp",
    )(tile_expert, n_tiles, src, x1, w_in, b_in, w_out, b_out)


def _combine_kernel(pos_hbm, y_hbm, x_ref, w_ref, p_ref, wg_ref, wp_ref, g_ref, b_ref, out_ref,
                    pos_smem, ybuf, pos_sem, row_sem):
    i = pl.program_id(0)
    t = COMBINE_TILE
    n = t * TOP_K
    cp = pltpu.make_async_copy(pos_hbm.at[pl.ds(pl.multiple_of(i * n, n), n)], pos_smem, pos_sem)
    cp.start()
    cp.wait()

    def row_copy(tok, kk):
        return pltpu.make_async_copy(y_hbm.at[pl.ds(pos_smem[tok * TOP_K + kk], 1)],
                                     ybuf.at[kk, pl.ds(tok, 1)], row_sem)

    def issue(tok, carry):
        for kk in range(TOP_K):
            row_copy(tok, kk).start()
        return carry

    def drain(tok, carry):
        for kk in range(TOP_K):
            row_copy(tok, kk).wait()
        return carry

    lax.fori_loop(0, t, issue, 0, unroll=2)

    x = x_ref[...]
    gate = jax.nn.sigmoid(jnp.dot(x.astype(BF16), wg_ref[...], preferred_element_type=F32))
    proj = jnp.dot(p_ref[...].astype(BF16), wp_ref[...], preferred_element_type=F32)
    y = DEEPNORM_ALPHA * x + gate * proj

    lax.fori_loop(0, t, drain, 0, unroll=2)
    w = w_ref[...]
    for kk in range(TOP_K):
        y = y + w[:, kk:kk + 1] * ybuf[kk]
    out_ref[...] = _layer_norm(y, g_ref[...], b_ref[...])


def _combine(pos_flat, y, x1, wcol, p, gate_w, proj_w, g, b):
    s = x1.shape[0]
    t = COMBINE_TILE
    any_spec = pl.BlockSpec(memory_space=pl.ANY)
    row = lambda w: pl.BlockSpec((t, w), lambda i: (i, 0))
    return pl.pallas_call(
        _combine_kernel,
        grid=(s // t,),
        in_specs=[any_spec, any_spec, row(D_MODEL), row(LANES), row(PLE_DIM),
                  _full((D_MODEL, D_MODEL)), _full((PLE_DIM, D_MODEL)),
                  _full((1, D_MODEL)), _full((1, D_MODEL))],
        out_specs=row(D_MODEL),
        out_shape=jax.ShapeDtypeStruct((s, D_MODEL), F32),
        scratch_shapes=[pltpu.SMEM((t * TOP_K,), I32), pltpu.VMEM((TOP_K, t, D_MODEL), F32),
                        pltpu.SemaphoreType.DMA, pltpu.SemaphoreType.DMA],
        compiler_params=_params("arbitrary"),
        name="combine",
    )(pos_flat, y, x1, wcol, p, gate_w.astype(BF16), proj_w.astype(BF16), g.reshape(1, -1), b.reshape(1, -1))


def _moe_block(layer, x1, p, router_w, router_b, w_in, b_in, w_out, b_out, gate_w, proj_w, g, b):
    s = x1.shape[0]
    tm = EXPERT_TILE
    n_tiles_max = s * TOP_K // tm + N_EXPERTS
    idx, rank, wcol, cnt = _router(x1, router_w, router_b)
    counts = cnt[:, 0].astype(I32)
    tiles_per = (counts + tm - 1) // tm
    tile_end = jnp.cumsum(tiles_per)
    offsets = (tile_end - tiles_per) * tm
    experts = jnp.arange(N_EXPERTS, dtype=I32)
    pos = rank + jnp.sum(jnp.where(idx[..., None] == experts, offsets, 0), axis=-1)
    pos_flat = pos.T.reshape(-1)
    tile_ids = jnp.arange(n_tiles_max, dtype=I32)
    tile_expert = jnp.minimum(jnp.sum((tile_end[None, :] <= tile_ids[:, None]).astype(I32), axis=-1),
                              N_EXPERTS - 1)
    n_tiles = tile_end[-1:].astype(I32)
    src = jnp.zeros((n_tiles_max * tm,), I32).at[pos_flat].set(jnp.arange(s * TOP_K, dtype=I32) // TOP_K)
    y = _gmm(layer, tile_expert, n_tiles, src, x1, w_in, b_in, w_out, b_out)
    return _combine(pos_flat, y, x1, wcol, p, gate_w, proj_w, g, b)


def _lambda_init(layer_idx):
    return 0.8 - 0.6 * math.exp(-0.3 * layer_idx)


def kernel(x, p, ab_w_in, ab_rel_bias, ab_q_norm, ab_kv_norm, ab_w_uq, ab_w_ukv, ab_w_out, c_w_in, c_lambda, c_subln, c_w_out, t5_table, ln_mix_g, ln_mix_b, ln_ffn_g, ln_ffn_b, router_w, router_b, exp_w_in, exp_b_in, exp_w_out, exp_b_out, ple_gate_w, ple_proj_w):
    batch, s, _ = x.shape
    assert batch == 1 and s % (SRC_GROUP * EXPERT_TILE) == 0
    xr = x[0]
    w_in_all = exp_w_in.reshape(DEPTH * N_EXPERTS, D_MODEL, 2 * D_FF)
    b_in_all = exp_b_in.reshape(DEPTH * N_EXPERTS, 1, 2 * D_FF)
    w_out_all = exp_w_out.reshape(DEPTH * N_EXPERTS, D_FF, D_MODEL)
    b_out_all = exp_b_out.reshape(DEPTH * N_EXPERTS, 1, D_MODEL)
    for i in range(DEPTH):
        j = i // 2
        if i % 2 == 0:
            qa, ka, va, qbt, kb, vbt = _prep_ab(xr, ab_w_in[j], ab_q_norm[j], ab_kv_norm[j], ab_w_uq[j],
                                                ab_w_ukv[j])
            o_a = _attn_a(qa, ka, va, ab_rel_bias[j])
            o_b = _attn_b(qbt, kb, vbt)
            x1 = _post_attn(xr, [o_a, o_b], [ab_w_out[j][:A_WIDTH], ab_w_out[j][A_WIDTH:]],
                            ln_mix_g[i], ln_mix_b[i])
        else:
            qt, k, vt = _prep_c(xr, c_w_in[j])
            lp = c_lambda[j].astype(F32)
            lam_init = _lambda_init(i)
            lam = jnp.exp(jnp.sum(lp[0] * lp[1])) - jnp.exp(jnp.sum(lp[2] * lp[3])) + lam_init
            o_c = _attn_c(qt, k, vt, lam, c_subln[j], t5_table, lam_init)
            x1 = _post_attn(xr, [o_c], [c_w_out[j]], ln_mix_g[i], ln_mix_b[i])
        xr = _moe_block(i, x1, p[i, 0], router_w[i], router_b[i], w_in_all, b_in_all, w_out_all, b_out_all,
                        ple_gate_w[i], ple_proj_w[i], ln_ffn_g[i], ln_ffn_b[i])
    return xr[None]
```

```python
import functools
import math

import jax
import jax.numpy as jnp
import numpy as np
from jax import lax
from jax.experimental import pallas as pl
from jax.experimental.pallas import tpu as pltpu

F32 = jnp.float32
BF16 = jnp.bfloat16
I32 = jnp.int32

D_MODEL = 1024
DEPTH = 4
CHUNK = 64
PLE_DIM = 256

A_HEADS = 8
A_HEAD_DIM = 64
A_LEFT_CHUNKS = 8
A_REL_CLIP = 128
A_WIDTH = A_HEADS * A_HEAD_DIM

B_HEADS = 8
B_NOPE_DIM = 64
B_ROPE_DIM = 32
B_V_DIM = 64
B_Q_RANK = 512
B_KV_RANK = 256
ROPE_BASE = 10000.0

C_HEADS = 8
C_HEAD_DIM = 64
C_QK = 2 * C_HEADS * C_HEAD_DIM

T5_BUCKETS = 32
T5_MAX_DIST = 128

N_EXPERTS = 32
TOP_K = 4
D_FF = 1024
SWIGLU_ALPHA = 1.702
SWIGLU_LIMIT = 7.0

DEEPNORM_ALPHA = (2.0 * DEPTH) ** 0.25
LN_EPS = 1e-5
RMS_EPS = 1e-6
NEG_INF = -1e30

LANES = 128
VMEM_LIMIT = 56 * 1024 * 1024

ATT_TILE = 512
COL_CHUNK = 256
ROW_CHUNK = 64
SUM_ROWS = 16
LOG2E = math.log2(math.e)
BAND_GROUP = 256
BAND_WINDOW = BAND_GROUP + A_LEFT_CHUNKS * CHUNK
ROUTER_TILE = 512
EXPERT_TILE = 256
SRC_GROUP = 4
COMBINE_TILE = 256
CAST_ROWS = 128

NT_DIMS = (((1,), (1,)), ((), ()))


def _params(*sem):
    return pltpu.CompilerParams(dimension_semantics=sem, vmem_limit_bytes=VMEM_LIMIT)


def _full(shape):
    n = len(shape)
    return pl.BlockSpec(shape, lambda *_: (0,) * n)


def _layer_norm(y, g, b):
    mu = jnp.mean(y, axis=-1, keepdims=True)
    d = y - mu
    var = jnp.mean(d * d, axis=-1, keepdims=True)
    return d * lax.rsqrt(var + LN_EPS) * g + b


def _rms_norm(c, g):
    inv = lax.rsqrt(jnp.mean(c * c, axis=-1, keepdims=True) + RMS_EPS)
    return c * inv * g


def _toeplitz(vals_pos, vals_neg, nr, nc):
    w = jnp.concatenate([vals_pos, vals_neg], axis=-1)
    lw = nc + nr
    assert w.shape[-1] == lw
    flat = jnp.tile(w, (1,) * (w.ndim - 1) + (nr,))[..., :nr * (lw - 1)]
    return flat.reshape(w.shape[:-1] + (nr, lw - 1))[..., :nc]


def _prep_ab_kernel(x_ref, wa_ref, wcq_ref, wckv_ref, wkr_ref, wkrr_ref, qn_ref, kvn_ref,
                    wuqa_ref, wuqb_ref, wukn_ref, wuv_ref, place_ref, ctab_ref, stab_ref,
                    cos_ref, sin_ref,
                    qa_ref, ka_ref, va_ref, qbt_ref, kb_ref, vbt_ref):
    xb = x_ref[...].astype(BF16)
    ha = jnp.dot(xb, wa_ref[...], preferred_element_type=F32)
    qa_ref[...] = (ha[:, :A_WIDTH] * (A_HEAD_DIM ** -0.5)).astype(BF16)
    ka_ref[...] = ha[:, A_WIDTH:2 * A_WIDTH].astype(BF16)
    va_ref[...] = ha[:, 2 * A_WIDTH:].astype(BF16)

    cq = jnp.dot(xb, wcq_ref[...], preferred_element_type=F32)
    cqn = _rms_norm(cq, qn_ref[...]).astype(BF16)
    qa_part = lax.dot_general(wuqa_ref[...], cqn, NT_DIMS, preferred_element_type=F32)
    qb_part = lax.dot_general(wuqb_ref[...], cqn, NT_DIMS, preferred_element_type=F32)
    ct = ctab_ref[...]
    st = stab_ref[...]
    scale = (B_NOPE_DIM + B_ROPE_DIM) ** -0.5 * LOG2E
    for h in range(B_HEADS):
        sl = slice(h * LANES, (h + 1) * LANES)
        qbt_ref[0, sl, :] = ((qa_part[sl, :] * ct + qb_part[sl, :] * st) * scale).astype(BF16)

    ckv = jnp.dot(xb, wckv_ref[...], preferred_element_type=F32)
    ckvn = _rms_norm(ckv, kvn_ref[...]).astype(BF16)
    kr = jnp.dot(xb, wkr_ref[...], preferred_element_type=F32)
    krr = jnp.dot(xb, wkrr_ref[...], preferred_element_type=F32)
    k_rope = (kr * cos_ref[...] + krr * sin_ref[...]).astype(BF16)
    kb = jnp.dot(ckvn, wukn_ref[...], preferred_element_type=F32)
    kb = kb + jnp.dot(k_rope, place_ref[...], preferred_element_type=F32)
    kb_ref[...] = kb.astype(BF16)
    vbt_ref[0] = lax.dot_general(wuv_ref[...], ckvn, NT_DIMS, preferred_element_type=F32).astype(BF16)


def _rot_half_cols(w):
    half = w.shape[-1] // 2
    return jnp.concatenate([-w[..., half:], w[..., :half]], axis=-1)


def _prep_ab(x, w_in, q_norm, kv_norm, w_uq, w_ukv):
    s = x.shape[0]
    t = ATT_TILE
    splits = [3 * A_WIDTH, 3 * A_WIDTH + B_Q_RANK, 3 * A_WIDTH + B_Q_RANK + B_KV_RANK]
    wa = w_in[:, :splits[0]].astype(BF16)
    wcq = w_in[:, splits[0]:splits[1]].astype(BF16)
    wckv = w_in[:, splits[1]:splits[2]].astype(BF16)
    wkr_f = w_in[:, splits[2]:]
    wkr = wkr_f.astype(BF16)
    wkrr = _rot_half_cols(wkr_f).astype(BF16)

    qd = B_NOPE_DIM + B_ROPE_DIM
    w_uq3 = w_uq.reshape(B_Q_RANK, B_HEADS, qd)
    zpad = jnp.zeros((B_Q_RANK, B_HEADS, LANES - qd), F32)
    wuqa = jnp.concatenate([w_uq3, zpad], axis=-1).reshape(B_Q_RANK, B_HEADS * LANES).T.astype(BF16)
    wuqb = jnp.concatenate([jnp.zeros((B_Q_RANK, B_HEADS, B_NOPE_DIM), F32),
                            _rot_half_cols(w_uq3[..., B_NOPE_DIM:]), zpad],
                           axis=-1).reshape(B_Q_RANK, B_HEADS * LANES).T.astype(BF16)
    w_ukv3 = w_ukv.reshape(B_KV_RANK, B_HEADS, B_NOPE_DIM + B_V_DIM)
    wukn = jnp.concatenate([w_ukv3[..., :B_NOPE_DIM],
                            jnp.zeros((B_KV_RANK, B_HEADS, LANES - B_NOPE_DIM), F32)],
                           axis=-1).reshape(B_KV_RANK, B_HEADS * LANES).astype(BF16)
    wuv = w_ukv3[..., B_NOPE_DIM:].reshape(B_KV_RANK, B_HEADS * B_V_DIM).T.astype(BF16)
    place = np.zeros((B_ROPE_DIM, B_HEADS * LANES), np.float32)
    for h in range(B_HEADS):
        for r in range(B_ROPE_DIM):
            place[r, h * LANES + B_NOPE_DIM + r] = 1.0
    place = jnp.asarray(place, BF16)

    pos = jnp.arange(s, dtype=F32)
    inv_freq = ROPE_BASE ** (-jnp.arange(0, B_ROPE_DIM, 2, dtype=F32) / B_ROPE_DIM)
    ang = pos[:, None] * inv_freq[None, :]
    cos = jnp.concatenate([jnp.cos(ang)] * 2, axis=-1)
    sin = jnp.concatenate([jnp.sin(ang)] * 2, axis=-1)
    ctab = jnp.concatenate([jnp.ones((s, B_NOPE_DIM), F32), cos, jnp.zeros((s, LANES - qd), F32)], axis=-1).T
    stab = jnp.concatenate([jnp.zeros((s, B_NOPE_DIM), F32), sin, jnp.zeros((s, LANES - qd), F32)], axis=-1).T

    row = lambda w: pl.BlockSpec((t, w), lambda i: (i, 0))
    col = pl.BlockSpec((LANES, t), lambda i: (0, i))
    tr = lambda w: pl.BlockSpec((1, w, t), lambda i: (i, 0, 0))
    consts = [wa, wcq, wckv, wkr, wkrr, q_norm.reshape(1, -1), kv_norm.reshape(1, -1),
              wuqa, wuqb, wukn, wuv, place]
    sds = jax.ShapeDtypeStruct
    return pl.pallas_call(
        _prep_ab_kernel,
        grid=(s // t,),
        in_specs=[row(D_MODEL)] + [_full(c.shape) for c in consts]
        + [col, col, row(B_ROPE_DIM), row(B_ROPE_DIM)],
        out_specs=[row(A_WIDTH), row(A_WIDTH), row(A_WIDTH),
                   tr(B_HEADS * LANES), row(B_HEADS * LANES), tr(B_HEADS * B_V_DIM)],
        out_shape=[sds((s, A_WIDTH), BF16)] * 3
        + [sds((s // t, B_HEADS * LANES, t), BF16), sds((s, B_HEADS * LANES), BF16),
           sds((s // t, B_HEADS * B_V_DIM, t), BF16)],
        compiler_params=_params("arbitrary"),
        name="prep_ab",
    )(x, *consts, ctab, stab, cos, sin)


def _attn_a_kernel(q_ref, kp_ref, kc_ref, vp_ref, vc_ref, bias_ref, o_ref, kbuf, vbuf):
    j = pl.program_id(1)
    t = ATT_TILE
    kbuf[0:t, :] = kp_ref[...]
    kbuf[t:2 * t, :] = kc_ref[...]
    vbuf[0:t, :] = vp_ref[...]
    vbuf[t:2 * t, :] = vc_ref[...]
    lo = lax.broadcasted_iota(I32, (1, LANES), 1) < A_HEAD_DIM
    col = lax.broadcasted_iota(I32, (1, BAND_WINDOW), 1)
    for g in range(t // BAND_GROUP):
        r0 = g * BAND_GROUP
        q = q_ref[r0:r0 + BAND_GROUP, :]
        k = kbuf[r0:r0 + BAND_WINDOW, :]
        v = vbuf[r0:r0 + BAND_WINDOW, :]
        valid = col + r0 >= jnp.where(j > 0, 0, t)
        outs = []
        for hh in range(2):
            qm = jnp.where(lo if hh == 0 else jnp.logical_not(lo), q, jnp.zeros_like(q))
            sc = lax.dot_general(qm, k, NT_DIMS, preferred_element_type=F32)
            sc = jnp.where(valid, sc + bias_ref[hh], NEG_INF)
            m = jnp.max(sc, axis=-1, keepdims=True)
            p = jnp.exp(sc - m)
            l = jnp.sum(p, axis=-1, keepdims=True)
            o = jnp.dot(p.astype(BF16), v, preferred_element_type=F32)
            outs.append(o / l)
        o_ref[r0:r0 + BAND_GROUP, :] = jnp.where(lo, outs[0], outs[1]).astype(BF16)


def _band_bias(rel_bias):
    qi = np.arange(BAND_GROUP)[:, None]
    kr = np.arange(BAND_WINDOW)[None, :]
    dchunk = (qi // CHUNK + A_LEFT_CHUNKS) - kr // CHUNK
    allowed = (dchunk >= 0) & (dchunk <= A_LEFT_CHUNKS)
    d_pos = np.arange(BAND_WINDOW)
    d_neg = np.arange(-BAND_GROUP, 0)
    to_idx = lambda d: np.clip(A_LEFT_CHUNKS * CHUNK - d, -A_REL_CLIP, A_REL_CLIP) + A_REL_CLIP
    tab = rel_bias.astype(F32).T
    table = _toeplitz(tab[:, to_idx(d_pos)], tab[:, to_idx(d_neg)], BAND_GROUP, BAND_WINDOW)
    return jnp.where(jnp.asarray(allowed)[None], table, NEG_INF)


def _attn_a(qa, ka, va, rel_bias):
    s = qa.shape[0]
    t = ATT_TILE
    bias = _band_bias(rel_bias)
    cur = pl.BlockSpec((t, LANES), lambda p, j: (j, p))
    prev = pl.BlockSpec((t, LANES), lambda p, j: (jnp.maximum(j - 1, 0), p))
    return pl.pallas_call(
        _attn_a_kernel,
        grid=(A_HEADS // 2, s // t),
        in_specs=[cur, prev, cur, prev, cur,
                  pl.BlockSpec((2, BAND_GROUP, BAND_WINDOW), lambda p, j: (p, 0, 0))],
        out_specs=cur,
        out_shape=jax.ShapeDtypeStruct((s, A_WIDTH), BF16),
        scratch_shapes=[pltpu.VMEM((2 * t, LANES), BF16), pltpu.VMEM((2 * t, LANES), BF16)],
        compiler_params=_params("arbitrary", "arbitrary"),
        name="attn_a",
    )(qa, ka, ka, va, va, bias)


class _FlashState:
    def __init__(self, m, l, acc, s0, s1, p0, p1, a0, a1):
        self.m, self.l, self.acc = m, l, acc
        self.s, self.p, self.a = (s0, s1), (p0, p1), (a0, a1)
        self.n_chunks = m.shape[1] // COL_CHUNK


def _flash_scratch(n_cols):
    t, w = ATT_TILE, COL_CHUNK
    return [pltpu.VMEM((1, n_cols), F32), pltpu.VMEM((1, n_cols), F32), pltpu.VMEM((LANES, n_cols), F32),
            pltpu.VMEM((t, w), F32), pltpu.VMEM((t, w), F32), pltpu.VMEM((t, w), BF16), pltpu.VMEM((t, w), BF16),
            pltpu.VMEM((1, w), F32), pltpu.VMEM((1, w), F32)]


def _flash_start(st, score_fn):
    st.m[...] = jnp.full(st.m.shape, NEG_INF, F32)
    st.l[...] = jnp.zeros(st.l.shape, F32)
    st.acc[...] = jnp.zeros(st.acc.shape, F32)
    st.p[1][...] = jnp.zeros(st.p[1].shape, BF16)
    st.a[1][...] = jnp.ones(st.a[1].shape, F32)
    st.s[0][...] = score_fn(0, 0)


def _flash_values(st, vt_fn, kt, c, b):
    cs = slice(c * COL_CHUNK, (c + 1) * COL_CHUNK)
    lhs = jnp.concatenate([vt_fn(kt), jnp.ones((SUM_ROWS, ATT_TILE), BF16)], axis=0)
    pv = jnp.dot(lhs, st.p[b][...], preferred_element_type=F32)
    alpha = st.a[b][...]
    st.acc[:, cs] = alpha * st.acc[:, cs] + pv[:LANES]
    st.l[:, cs] = alpha * st.l[:, cs] + pv[LANES:LANES + 1]


def _flash_region(st, score_fn, vt_fn, tiles, has_next):
    units = [(ti, c) for ti in range(len(tiles)) for c in range(st.n_chunks)]
    assert len(units) % 2 == 0
    for u, (ti, c) in enumerate(units):
        b = u % 2
        kt, bias_fn = tiles[ti]
        if u + 1 < len(units):
            nti, nc = units[u + 1]
            st.s[1 - b][...] = score_fn(tiles[nti][0], nc)
        elif has_next:
            st.s[1 - b][...] = score_fn(tiles[-1][0] + 1, 0)
        if u > 0:
            pti, pc = units[u - 1]
            _flash_values(st, vt_fn, tiles[pti][0], pc, 1 - b)
        else:
            _flash_values(st, vt_fn, jnp.maximum(kt - 1, 0), st.n_chunks - 1, 1 - b)
        cs = slice(c * COL_CHUNK, (c + 1) * COL_CHUNK)
        c0 = (c * COL_CHUNK) % ATT_TILE

        def rows(r, s_ref=st.s[b], bias_fn=bias_fn, c0=c0):
            blk = s_ref[r * ROW_CHUNK:(r + 1) * ROW_CHUNK, :]
            if bias_fn is not None:
                blk = blk + bias_fn(c0, r)
            return blk

        n_rows = ATT_TILE // ROW_CHUNK
        part = rows(0)
        for r in range(1, n_rows):
            part = jnp.maximum(part, rows(r))
        m_prev = st.m[:, cs]
        m_new = jnp.maximum(m_prev, jnp.max(part, axis=0, keepdims=True))
        st.a[b][...] = jnp.exp2(m_prev - m_new)
        st.m[:, cs] = m_new
        for r in range(n_rows):
            st.p[b][r * ROW_CHUNK:(r + 1) * ROW_CHUNK, :] = jnp.exp2(rows(r) - m_new).astype(BF16)


def _flash_sweep(st, score_fn, vt_fn, n_plain, tail_regions, last_tile):
    _flash_start(st, score_fn)

    def pair(j, carry):
        _flash_region(st, score_fn, vt_fn, [(2 * j, None), (2 * j + 1, None)], True)
        return carry

    lax.fori_loop(0, lax.shift_right_logical(n_plain, 1), pair, 0)

    @pl.when((n_plain & 1) == 1)
    def _():
        _flash_region(st, score_fn, vt_fn, [(n_plain - 1, None)], True)

    tail_regions()
    _flash_values(st, vt_fn, last_tile, st.n_chunks - 1, 1)


def _chunk_causal_bias_t(t):
    r = np.arange(t)
    ok = (r[:, None] // CHUNK) <= (r[None, :] // CHUNK)
    return jnp.asarray(np.where(ok, 0.0, NEG_INF), F32)


def _attn_b_kernel(qt_ref, k_ref, vt_ref, diag_ref, o_ref, *scratch):
    i = pl.program_id(1)
    t = ATT_TILE
    st = _FlashState(*scratch)
    per_head = t // COL_CHUNK

    def score(kt, c):
        hh = c // per_head
        c0 = (c % per_head) * COL_CHUNK
        k = k_ref[pl.ds(pl.multiple_of(kt * t, t), t), hh * LANES:(hh + 1) * LANES]
        return jnp.dot(k, qt_ref[0, hh * LANES:(hh + 1) * LANES, c0:c0 + COL_CHUNK],
                       preferred_element_type=F32)

    vt_fn = lambda kt: vt_ref[kt]

    def tail():
        diag = lambda c0, r: diag_ref[r * ROW_CHUNK:(r + 1) * ROW_CHUNK, c0:c0 + COL_CHUNK]
        _flash_region(st, score, vt_fn, [(i, diag)], False)

    _flash_sweep(st, score, vt_fn, i, tail, i)
    o2 = st.acc[...] / st.l[...]
    ot = jnp.concatenate([o2[:B_V_DIM, :t], o2[B_V_DIM:, t:]], axis=0)
    o_ref[...] = ot.T.astype(BF16)


def _attn_b(qbt, kb, vbt):
    nb, _, t = qbt.shape
    s = nb * t
    return pl.pallas_call(
        _attn_b_kernel,
        grid=(B_HEADS // 2, nb),
        in_specs=[pl.BlockSpec((1, 2 * LANES, t), lambda p, i: (i, p, 0)),
                  pl.BlockSpec((s, 2 * LANES), lambda p, i: (0, p)),
                  pl.BlockSpec((nb, LANES, t), lambda p, i: (0, p, 0)),
                  _full((t, t))],
        out_specs=pl.BlockSpec((t, LANES), lambda p, i: (i, p)),
        out_shape=jax.ShapeDtypeStruct((s, B_HEADS * B_V_DIM), BF16),
        scratch_shapes=_flash_scratch(2 * t),
        compiler_params=_params("arbitrary", "arbitrary"),
        name="attn_b",
    )(qbt, kb, vbt, _chunk_causal_bias_t(t))


def _attn_c_kernel(lam_ref, qt_ref, k_ref, vt_ref, d0_ref, d1_ref, g_ref, o_ref, q2_ref, *scratch, out_scale):
    i = pl.program_id(1)
    t = ATT_TILE
    st = _FlashState(*scratch)
    d = C_HEAD_DIM
    zero = jnp.zeros((d, t), BF16)
    q2_ref[:d, :t] = qt_ref[0, :d, :]
    q2_ref[d:, :t] = zero
    q2_ref[:d, t:] = zero
    q2_ref[d:, t:] = qt_ref[0, d:, :]

    def score(kt, c):
        return jnp.dot(k_ref[pl.ds(pl.multiple_of(kt * t, t), t), :],
                       q2_ref[:, c * COL_CHUNK:(c + 1) * COL_CHUNK], preferred_element_type=F32)

    vt_fn = lambda kt: vt_ref[kt]

    def tail():
        tile_of = lambda ref: lambda c0, r: ref[0, r * ROW_CHUNK:(r + 1) * ROW_CHUNK, c0:c0 + COL_CHUNK]
        diag = (i, tile_of(d0_ref))

        @pl.when(i > 0)
        def _():
            below = (i - 1, tile_of(d1_ref))
            _flash_region(st, score, vt_fn, [below, diag], False)

        @pl.when(i == 0)
        def _():
            _flash_region(st, score, vt_fn, [diag], False)

    _flash_sweep(st, score, vt_fn, jnp.maximum(i - 1, 0), tail, i)

    o2 = st.acc[...] / st.l[...]
    ot = o2[:, :t] - lam_ref[0] * o2[:, t:]
    inv = lax.rsqrt(jnp.mean(ot * ot, axis=0, keepdims=True) + RMS_EPS)
    o_ref[...] = (ot * inv * g_ref[...] * out_scale).T.astype(BF16)


def _t5_bucket(rel):
    half = T5_BUCKETS // 2
    max_exact = half // 2
    ret = jnp.where(rel > 0, half, 0)
    n = jnp.abs(rel)
    large = max_exact + (jnp.log(jnp.maximum(n, max_exact).astype(F32) / max_exact)
                         / math.log(T5_MAX_DIST / max_exact) * (half - max_exact)).astype(I32)
    large = jnp.minimum(large, half - 1)
    return ret + jnp.where(n < max_exact, n, large)


def _t5_tiles_t(t5_table, t):
    assert t >= T5_MAX_DIST
    far_bucket = T5_BUCKETS // 2 - 1
    tab = (t5_table.astype(F32) - t5_table[far_bucket].astype(F32)[None, :]).T
    by_rel = lambda rel: tab[:, _t5_bucket(rel)]
    d_pos = jnp.arange(t, dtype=I32)
    d_neg = jnp.arange(-t, 0, dtype=I32)
    d0 = _toeplitz(by_rel(-d_pos), by_rel(-d_neg), t, t)
    d1 = _toeplitz(by_rel(-d_pos - t), by_rel(-d_neg - t), t, t)
    r = np.arange(t)
    ok = (r[:, None] // CHUNK) <= (r[None, :] // CHUNK)
    d0 = jnp.where(jnp.asarray(ok)[None], d0 * LOG2E, NEG_INF)
    return d0, d1 * LOG2E


def _attn_c(qt, k, vt, lam, subln_g, t5_table, lam_init):
    nb, _, t = qt.shape
    s = nb * t
    d0, d1 = _t5_tiles_t(t5_table, t)
    return pl.pallas_call(
        functools.partial(_attn_c_kernel, out_scale=1.0 - lam_init),
        grid=(C_HEADS, nb),
        in_specs=[pl.BlockSpec(memory_space=pltpu.SMEM),
                  pl.BlockSpec((1, LANES, t), lambda h, i: (i, h, 0)),
                  pl.BlockSpec((s, LANES), lambda h, i: (0, h)),
                  pl.BlockSpec((nb, LANES, t), lambda h, i: (0, h, 0)),
                  pl.BlockSpec((1, t, t), lambda h, i: (h, 0, 0)),
                  pl.BlockSpec((1, t, t), lambda h, i: (h, 0, 0)),
                  _full((LANES, 1))],
        out_specs=pl.BlockSpec((t, LANES), lambda h, i: (i, h)),
        out_shape=jax.ShapeDtypeStruct((s, C_HEADS * 2 * C_HEAD_DIM), BF16),
        scratch_shapes=[pltpu.VMEM((LANES, 2 * t), BF16)] + _flash_scratch(2 * t),
        compiler_params=_params("arbitrary", "arbitrary"),
        name="attn_c",
    )(lam.reshape(1), qt, k, vt, d0, d1, subln_g.reshape(-1, 1))


def _prep_c_kernel(x_ref, wq_ref, wk_ref, wv_ref, qt_ref, k_ref, vt_ref):
    xb = x_ref[...].astype(BF16)
    qt = lax.dot_general(wq_ref[...], xb, NT_DIMS, preferred_element_type=F32)
    qt_ref[0] = (qt * (C_HEAD_DIM ** -0.5 * LOG2E)).astype(BF16)
    k_ref[...] = jnp.dot(xb, wk_ref[...], preferred_element_type=F32).astype(BF16)
    vt_ref[0] = lax.dot_general(wv_ref[...], xb, NT_DIMS, preferred_element_type=F32).astype(BF16)


def _prep_c(x, w_in):
    s = x.shape[0]
    t = ATT_TILE
    wq = w_in[:, :C_QK].T.astype(BF16)
    wk = w_in[:, C_QK:2 * C_QK].astype(BF16)
    wv = w_in[:, 2 * C_QK:].T.astype(BF16)
    row = pl.BlockSpec((t, D_MODEL), lambda i: (i, 0))
    tr = pl.BlockSpec((1, C_QK, t), lambda i: (i, 0, 0))
    sds = jax.ShapeDtypeStruct
    return pl.pallas_call(
        _prep_c_kernel,
        grid=(s // t,),
        in_specs=[row, _full(wq.shape), _full(wk.shape), _full(wv.shape)],
        out_specs=[tr, row, tr],
        out_shape=[sds((s // t, C_QK, t), BF16), sds((s, C_QK), BF16), sds((s // t, C_QK, t), BF16)],
        compiler_params=_params("arbitrary"),
        name="prep_c",
    )(x, wq, wk, wv)


def _post_attn_kernel(*refs, n_in):
    x_ref = refs[0]
    o_refs = refs[1:1 + n_in]
    w_refs = refs[1 + n_in:1 + 2 * n_in]
    g_ref, b_ref, out_ref = refs[1 + 2 * n_in:]
    y = DEEPNORM_ALPHA * x_ref[...]
    for o_ref, w_ref in zip(o_refs, w_refs):
        y = y + jnp.dot(o_ref[...], w_ref[...], preferred_element_type=F32)
    out_ref[...] = _layer_norm(y, g_ref[...], b_ref[...])


def _post_attn(x, outs, weights, g, b):
    s = x.shape[0]
    t = ATT_TILE
    row = lambda w: pl.BlockSpec((t, w), lambda i: (i, 0))
    ws = [w.astype(BF16) for w in weights]
    return pl.pallas_call(
        functools.partial(_post_attn_kernel, n_in=len(outs)),
        grid=(s // t,),
        in_specs=[row(D_MODEL)] + [row(o.shape[1]) for o in outs] + [_full(w.shape) for w in ws]
        + [_full((1, D_MODEL))] * 2,
        out_specs=row(D_MODEL),
        out_shape=jax.ShapeDtypeStruct((s, D_MODEL), F32),
        compiler_params=_params("arbitrary"),
        name="post_attn",
    )(x, *outs, *ws, g.reshape(1, -1), b.reshape(1, -1))


def _router_kernel(x_ref, wr_ref, br_ref, tri_ref, idx_ref, rank_ref, w_ref, cnt_ref, carry):
    step = pl.program_id(0)

    @pl.when(step == 0)
    def _():
        carry[...] = jnp.zeros(carry.shape, F32)

    t = ROUTER_TILE
    logits = lax.dot_general(wr_ref[...], x_ref[...], NT_DIMS, preferred_element_type=F32,
                             precision=lax.Precision.HIGHEST) + br_ref[...]
    eid = lax.broadcasted_iota(I32, (N_EXPERTS, t), 0).astype(F32)
    work = logits
    vals, sels = [], []
    for _ in range(TOP_K):
        m = jnp.max(work, axis=0, keepdims=True)
        sel = jnp.min(jnp.where(work == m, eid, float(N_EXPERTS)), axis=0, keepdims=True)
        vals.append(m)
        sels.append(sel)
        work = jnp.where(eid == sel, -jnp.inf, work)
    exps = [jnp.exp(v - vals[0]) for v in vals]
    denom = exps[0] + exps[1] + exps[2] + exps[3]

    chosen = jnp.zeros((N_EXPERTS, t), F32)
    for sel in sels:
        chosen = chosen + (eid == sel).astype(F32)
    before = jnp.dot(chosen.astype(BF16), tri_ref[...], preferred_element_type=F32) + carry[...]
    for kk, sel in enumerate(sels):
        rank = jnp.sum(jnp.where(eid == sel, before, 0.0), axis=0, keepdims=True)
        rank_ref[kk:kk + 1, :] = rank.astype(I32)
        idx_ref[kk:kk + 1, :] = sel.astype(I32)
    carry[...] = carry[...] + jnp.sum(chosen, axis=1, keepdims=True)
    cnt_ref[...] = jnp.broadcast_to(carry[...], cnt_ref.shape)

    sub = lax.broadcasted_iota(I32, (LANES, t), 0)
    wrows = jnp.zeros((LANES, t), F32)
    for kk in range(TOP_K):
        wrows = jnp.where(sub == kk, exps[kk] / denom, wrows)
    w_ref[...] = wrows.T


def _router(x1, router_w, router_b):
    s = x1.shape[0]
    t = ROUTER_TILE
    tri = jnp.asarray(np.triu(np.ones((t, t), np.float32), k=1), BF16)
    tok = pl.BlockSpec((TOP_K, t), lambda i: (0, i))
    return pl.pallas_call(
        _router_kernel,
        grid=(s // t,),
        in_specs=[pl.BlockSpec((t, D_MODEL), lambda i: (i, 0)),
                  _full((N_EXPERTS, D_MODEL)), _full((N_EXPERTS, 1)), _full((t, t))],
        out_specs=[tok, tok, pl.BlockSpec((t, LANES), lambda i: (i, 0)), _full((N_EXPERTS, LANES))],
        out_shape=[jax.ShapeDtypeStruct((TOP_K, s), I32), jax.ShapeDtypeStruct((TOP_K, s), I32),
                   jax.ShapeDtypeStruct((s, LANES), F32), jax.ShapeDtypeStruct((N_EXPERTS, LANES), F32)],
        scratch_shapes=[pltpu.VMEM((N_EXPERTS, 1), F32)],
        compiler_params=_params("arbitrary"),
        name="router",
    )(x1, router_w.T, router_b.reshape(-1, 1), tri)


def _gmm_kernel(te_ref, nt_ref, src_hbm, x_hbm, wi_ref, bi_ref, wo_ref, bo_ref, y_ref,
                wi_bf, wo_bf, xbuf, src_smem, src_sem, row_sem):
    i = pl.program_id(0)
    n_tiles = nt_ref[0]
    tm = EXPERT_TILE
    group = SRC_GROUP * tm

    def half_base(tile):
        return (lax.shift_right_logical(tile, SRC_GROUP.bit_length() - 1) & 1) * group

    def fetch_indices(tile):
        dst = pl.multiple_of(half_base(tile), group)
        cp = pltpu.make_async_copy(src_hbm.at[pl.ds(pl.multiple_of(tile * tm, group), group)],
                                   src_smem.at[pl.ds(dst, group)], src_sem)
        cp.start()
        cp.wait()

    def row_copy(tile, r):
        slot = tile & 1
        tok = src_smem[half_base(tile) + (tile & (SRC_GROUP - 1)) * tm + r]
        return pltpu.make_async_copy(x_hbm.at[pl.ds(tok, 1)], xbuf.at[slot, pl.ds(r, 1)], row_sem.at[slot])

    def issue_tile(tile):
        def body(r, carry):
            row_copy(tile, r).start()
            return carry

        lax.fori_loop(0, tm, body, 0, unroll=8)

    def wait_tile(tile):
        def body(r, carry):
            row_copy(tile, r).wait()
            return carry

        lax.fori_loop(0, tm, body, 0, unroll=8)

    @pl.when(i == 0)
    def _():
        fetch_indices(i)
        issue_tile(i)

    nxt = i + 1

    @pl.when(nxt < n_tiles)
    def _():
        @pl.when((nxt & (SRC_GROUP - 1)) == 0)
        def _():
            fetch_indices(nxt)

        issue_tile(nxt)

    prev = te_ref[jnp.maximum(i - 1, 0)]
    fresh = jnp.logical_or(i == 0, te_ref[i] != prev)

    @pl.when(jnp.logical_and(i < n_tiles, fresh))
    def _():
        def cast(c, carry):
            rows = pl.ds(pl.multiple_of(c * CAST_ROWS, CAST_ROWS), CAST_ROWS)
            wi_bf[rows, :] = wi_ref[0, rows, :].astype(BF16)
            wo_bf[rows, :] = wo_ref[0, rows, :].astype(BF16)
            return carry

        lax.fori_loop(0, D_MODEL // CAST_ROWS, cast, 0)

    @pl.when(i < n_tiles)
    def _():
        wait_tile(i)
        xb = xbuf[i & 1].astype(BF16)
        h = jnp.dot(xb, wi_bf[...], preferred_element_type=F32) + bi_ref[0]
        glu = jnp.minimum(h[:, :D_FF], SWIGLU_LIMIT)
        lin = jnp.clip(h[:, D_FF:], -SWIGLU_LIMIT, SWIGLU_LIMIT)
        act = glu * jax.nn.sigmoid(SWIGLU_ALPHA * glu) * (lin + 1.0)
        y_ref[...] = jnp.dot(act.astype(BF16), wo_bf[...], preferred_element_type=F32) + bo_ref[0]

    @pl.when(i >= n_tiles)
    def _():
        y_ref[...] = jnp.zeros(y_ref.shape, F32)


def _gmm(layer, tile_expert, n_tiles, src, x1, w_in, b_in, w_out, b_out):
    n_rows = src.shape[0]
    tm = EXPERT_TILE
    assert n_rows % (SRC_GROUP * tm) == 0
    last = lambda i, nt: jnp.maximum(jnp.minimum(i, nt[0] - 1), 0)
    expert = lambda i, te, nt: layer * N_EXPERTS + te[last(i, nt)]
    any_spec = pl.BlockSpec(memory_space=pl.ANY)
    grid_spec = pltpu.PrefetchScalarGridSpec(
        num_scalar_prefetch=2,
        grid=(n_rows // tm,),
        in_specs=[any_spec, any_spec,
                  pl.BlockSpec((1, D_MODEL, 2 * D_FF), lambda i, te, nt: (expert(i, te, nt), 0, 0)),
                  pl.BlockSpec((1, 1, 2 * D_FF), lambda i, te, nt: (expert(i, te, nt), 0, 0)),
                  pl.BlockSpec((1, D_FF, D_MODEL), lambda i, te, nt: (expert(i, te, nt), 0, 0)),
                  pl.BlockSpec((1, 1, D_MODEL), lambda i, te, nt: (expert(i, te, nt), 0, 0))],
        out_specs=pl.BlockSpec((tm, D_MODEL), lambda i, te, nt: (i, 0)),
        scratch_shapes=[pltpu.VMEM((D_MODEL, 2 * D_FF), BF16), pltpu.VMEM((D_FF, D_MODEL), BF16),
                        pltpu.VMEM((2, tm, D_MODEL), F32), pltpu.SMEM((2 * SRC_GROUP * tm,), I32),
                        pltpu.SemaphoreType.DMA, pltpu.SemaphoreType.DMA((2,))],
    )
    return pl.pallas_call(
        _gmm_kernel,
        grid_spec=grid_spec,
        out_shape=jax.ShapeDtypeStruct((n_rows, D_MODEL), F32),
        compiler_params=_params("arbitrary"),
        name="expert_mlp",
    )(tile_expert, n_tiles, src, x1, w_in, b_in, w_out, b_out)


def _combine_kernel(pos_hbm, y_hbm, x_ref, w_ref, p_ref, wg_ref, wp_ref, g_ref, b_ref, out_ref,
                    pos_smem, ybuf, pos_sem, row_sem):
    i = pl.program_id(0)
    t = COMBINE_TILE
    n = t * TOP_K
    cp = pltpu.make_async_copy(pos_hbm.at[pl.ds(pl.multiple_of(i * n, n), n)], pos_smem, pos_sem)
    cp.start()
    cp.wait()

    def row_copy(tok, kk):
        return pltpu.make_async_copy(y_hbm.at[pl.ds(pos_smem[tok * TOP_K + kk], 1)],
                                     ybuf.at[kk, pl.ds(tok, 1)], row_sem)

    def issue(tok, carry):
        for kk in range(TOP_K):
            row_copy(tok, kk).start()
        return carry

    def drain(tok, carry):
        for kk in range(TOP_K):
            row_copy(tok, kk).wait()
        return carry

    lax.fori_loop(0, t, issue, 0, unroll=2)

    x = x_ref[...]
    gate = jax.nn.sigmoid(jnp.dot(x.astype(BF16), wg_ref[...], preferred_element_type=F32))
    proj = jnp.dot(p_ref[...].astype(BF16), wp_ref[...], preferred_element_type=F32)
    y = DEEPNORM_ALPHA * x + gate * proj

    lax.fori_loop(0, t, drain, 0, unroll=2)
    w = w_ref[...]
    for kk in range(TOP_K):
        y = y + w[:, kk:kk + 1] * ybuf[kk]
    out_ref[...] = _layer_norm(y, g_ref[...], b_ref[...])


def _combine(pos_flat, y, x1, wcol, p, gate_w, proj_w, g, b):
    s = x1.shape[0]
    t = COMBINE_TILE
    any_spec = pl.BlockSpec(memory_space=pl.ANY)
    row = lambda w: pl.BlockSpec((t, w), lambda i: (i, 0))
    return pl.pallas_call(
        _combine_kernel,
        grid=(s // t,),
        in_specs=[any_spec, any_spec, row(D_MODEL), row(LANES), row(PLE_DIM),
                  _full((D_MODEL, D_MODEL)), _full((PLE_DIM, D_MODEL)),
                  _full((1, D_MODEL)), _full((1, D_MODEL))],
        out_specs=row(D_MODEL),
        out_shape=jax.ShapeDtypeStruct((s, D_MODEL), F32),
        scratch_shapes=[pltpu.SMEM((t * TOP_K,), I32), pltpu.VMEM((TOP_K, t, D_MODEL), F32),
                        pltpu.SemaphoreType.DMA, pltpu.SemaphoreType.DMA],
        compiler_params=_params("arbitrary"),
        name="combine",
    )(pos_flat, y, x1, wcol, p, gate_w.astype(BF16), proj_w.astype(BF16), g.reshape(1, -1), b.reshape(1, -1))


def _moe_block(layer, x1, p, router_w, router_b, w_in, b_in, w_out, b_out, gate_w, proj_w, g, b):
    s = x1.shape[0]
    tm = EXPERT_TILE
    n_tiles_max = s * TOP_K // tm + N_EXPERTS
    idx, rank, wcol, cnt = _router(x1, router_w, router_b)
    counts = cnt[:, 0].astype(I32)
    tiles_per = (counts + tm - 1) // tm
    tile_end = jnp.cumsum(tiles_per)
    offsets = (tile_end - tiles_per) * tm
    experts = jnp.arange(N_EXPERTS, dtype=I32)
    pos = rank + jnp.sum(jnp.where(idx[..., None] == experts, offsets, 0), axis=-1)
    pos_flat = pos.T.reshape(-1)
    tile_ids = jnp.arange(n_tiles_max, dtype=I32)
    tile_expert = jnp.minimum(jnp.sum((tile_end[None, :] <= tile_ids[:, None]).astype(I32), axis=-1),
                              N_EXPERTS - 1)
    n_tiles = tile_end[-1:].astype(I32)
    src = jnp.zeros((n_tiles_max * tm,), I32).at[pos_flat].set(jnp.arange(s * TOP_K, dtype=I32) // TOP_K)
    y = _gmm(layer, tile_expert, n_tiles, src, x1, w_in, b_in, w_out, b_out)
    return _combine(pos_flat, y, x1, wcol, p, gate_w, proj_w, g, b)


def _lambda_init(layer_idx):
    return 0.8 - 0.6 * math.exp(-0.3 * layer_idx)


def kernel(x, p, ab_w_in, ab_rel_bias, ab_q_norm, ab_kv_norm, ab_w_uq, ab_w_ukv, ab_w_out, c_w_in, c_lambda, c_subln, c_w_out, t5_table, ln_mix_g, ln_mix_b, ln_ffn_g, ln_ffn_b, router_w, router_b, exp_w_in, exp_b_in, exp_w_out, exp_b_out, ple_gate_w, ple_proj_w):
    batch, s, _ = x.shape
    assert batch == 1 and s % (SRC_GROUP * EXPERT_TILE) == 0
    xr = x[0]
    w_in_all = exp_w_in.reshape(DEPTH * N_EXPERTS, D_MODEL, 2 * D_FF)
    b_in_all = exp_b_in.reshape(DEPTH * N_EXPERTS, 1, 2 * D_FF)
    w_out_all = exp_w_out.reshape(DEPTH * N_EXPERTS, D_FF, D_MODEL)
    b_out_all = exp_b_out.reshape(DEPTH * N_EXPERTS, 1, D_MODEL)
    for i in range(DEPTH):
        j = i // 2
        if i % 2 == 0:
            qa, ka, va, qbt, kb, vbt = _prep_ab(xr, ab_w_in[j], ab_q_norm[j], ab_kv_norm[j], ab_w_uq[j],
                                                ab_w_ukv[j])
            o_a = _attn_a(qa, ka, va, ab_rel_bias[j])
            o_b = _attn_b(qbt, kb, vbt)
            x1 = _post_attn(xr, [o_a, o_b], [ab_w_out[j][:A_WIDTH], ab_w_out[j][A_WIDTH:]],
                            ln_mix_g[i], ln_mix_b[i])
        else:
            qt, k, vt = _prep_c(xr, c_w_in[j])
            lp = c_lambda[j].astype(F32)
            lam_init = _lambda_init(i)
            lam = jnp.exp(jnp.sum(lp[0] * lp[1])) - jnp.exp(jnp.sum(lp[2] * lp[3])) + lam_init
            o_c = _attn_c(qt, k, vt, lam, c_subln[j], t5_table, lam_init)
            x1 = _post_attn(xr, [o_c], [c_w_out[j]], ln_mix_g[i], ln_mix_b[i])
        xr = _moe_block(i, x1, p[i, 0], router_w[i], router_b[i], w_in_all, b_in_all, w_out_all, b_out_all,
                        ple_gate_w[i], ple_proj_w[i], ln_ffn_g[i], ln_ffn_b[i])
    return xr[None]
```

```python
import functools
import math

import jax
import jax.numpy as jnp
import numpy as np
from jax import lax
from jax.experimental import pallas as pl
from jax.experimental.pallas import tpu as pltpu

F32 = jnp.float32
BF16 = jnp.bfloat16
I32 = jnp.int32

D_MODEL = 1024
DEPTH = 4
CHUNK = 64
PLE_DIM = 256

A_HEADS = 8
A_HEAD_DIM = 64
A_LEFT_CHUNKS = 8
A_REL_CLIP = 128
A_WIDTH = A_HEADS * A_HEAD_DIM

B_HEADS = 8
B_NOPE_DIM = 64
B_ROPE_DIM = 32
B_V_DIM = 64
B_Q_RANK = 512
B_KV_RANK = 256
ROPE_BASE = 10000.0

C_HEADS = 8
C_HEAD_DIM = 64
C_QK = 2 * C_HEADS * C_HEAD_DIM

T5_BUCKETS = 32
T5_MAX_DIST = 128

N_EXPERTS = 32
TOP_K = 4
D_FF = 1024
SWIGLU_ALPHA = 1.702
SWIGLU_LIMIT = 7.0

DEEPNORM_ALPHA = (2.0 * DEPTH) ** 0.25
LN_EPS = 1e-5
RMS_EPS = 1e-6
NEG_INF = -1e30

LANES = 128
ROW_SUBLANES = D_MODEL // LANES
VMEM_LIMIT = 56 * 1024 * 1024

ATT_TILE = 512
COL_CHUNK = 256
ROW_CHUNK = 64
SUM_ROWS = 16
LOG2E = math.log2(math.e)
BAND_GROUP = 256
BAND_WINDOW = BAND_GROUP + A_LEFT_CHUNKS * CHUNK
ROUTER_TILE = 512
EXPERT_TILE = 256
SRC_GROUP = 4
COMBINE_TILE = 256
CAST_ROWS = 128

NT_DIMS = (((1,), (1,)), ((), ()))


def _params(*sem):
    return pltpu.CompilerParams(dimension_semantics=sem, vmem_limit_bytes=VMEM_LIMIT)


def _full(shape):
    n = len(shape)
    return pl.BlockSpec(shape, lambda *_: (0,) * n)


def _layer_norm(y, g, b):
    mu = jnp.mean(y, axis=-1, keepdims=True)
    d = y - mu
    var = jnp.mean(d * d, axis=-1, keepdims=True)
    return d * lax.rsqrt(var + LN_EPS) * g + b


def _rms_norm(c, g):
    inv = lax.rsqrt(jnp.mean(c * c, axis=-1, keepdims=True) + RMS_EPS)
    return c * inv * g


def _load_row_tiles(ref, *lead):
    tiles = pltpu.einshape("ral->arl", ref[lead] if lead else ref[...])
    return jnp.concatenate([tiles[a] for a in range(ROW_SUBLANES)], axis=1)


def _store_row_tiles(ref, value):
    tiles = jnp.stack([value[:, a * LANES:(a + 1) * LANES] for a in range(ROW_SUBLANES)], axis=0)
    ref[...] = pltpu.einshape("arl->ral", tiles)


def _toeplitz(vals_pos, vals_neg, nr, nc):
    w = jnp.concatenate([vals_pos, vals_neg], axis=-1)
    lw = nc + nr
    assert w.shape[-1] == lw
    flat = jnp.tile(w, (1,) * (w.ndim - 1) + (nr,))[..., :nr * (lw - 1)]
    return flat.reshape(w.shape[:-1] + (nr, lw - 1))[..., :nc]


def _prep_ab_kernel(x_ref, wa_ref, wcq_ref, wckv_ref, wkr_ref, wkrr_ref, qn_ref, kvn_ref,
                    wuqa_ref, wuqb_ref, wukn_ref, wuv_ref, place_ref, ctab_ref, stab_ref,
                    cos_ref, sin_ref,
                    qa_ref, ka_ref, va_ref, qbt_ref, kb_ref, vbt_ref):
    xb = x_ref[...].astype(BF16)
    ha = jnp.dot(xb, wa_ref[...], preferred_element_type=F32)
    qa_ref[...] = (ha[:, :A_WIDTH] * (A_HEAD_DIM ** -0.5)).astype(BF16)
    ka_ref[...] = ha[:, A_WIDTH:2 * A_WIDTH].astype(BF16)
    va_ref[...] = ha[:, 2 * A_WIDTH:].astype(BF16)

    cq = jnp.dot(xb, wcq_ref[...], preferred_element_type=F32)
    cqn = _rms_norm(cq, qn_ref[...]).astype(BF16)
    qa_part = lax.dot_general(wuqa_ref[...], cqn, NT_DIMS, preferred_element_type=F32)
    qb_part = lax.dot_general(wuqb_ref[...], cqn, NT_DIMS, preferred_element_type=F32)
    ct = ctab_ref[...]
    st = stab_ref[...]
    scale = (B_NOPE_DIM + B_ROPE_DIM) ** -0.5 * LOG2E
    for h in range(B_HEADS):
        sl = slice(h * LANES, (h + 1) * LANES)
        qbt_ref[0, sl, :] = ((qa_part[sl, :] * ct + qb_part[sl, :] * st) * scale).astype(BF16)

    ckv = jnp.dot(xb, wckv_ref[...], preferred_element_type=F32)
    ckvn = _rms_norm(ckv, kvn_ref[...]).astype(BF16)
    kr = jnp.dot(xb, wkr_ref[...], preferred_element_type=F32)
    krr = jnp.dot(xb, wkrr_ref[...], preferred_element_type=F32)
    k_rope = (kr * cos_ref[...] + krr * sin_ref[...]).astype(BF16)
    kb = jnp.dot(ckvn, wukn_ref[...], preferred_element_type=F32)
    kb = kb + jnp.dot(k_rope, place_ref[...], preferred_element_type=F32)
    kb_ref[...] = kb.astype(BF16)
    vbt_ref[0] = lax.dot_general(wuv_ref[...], ckvn, NT_DIMS, preferred_element_type=F32).astype(BF16)


def _rot_half_cols(w):
    half = w.shape[-1] // 2
    return jnp.concatenate([-w[..., half:], w[..., :half]], axis=-1)


def _prep_ab(x, w_in, q_norm, kv_norm, w_uq, w_ukv):
    s = x.shape[0]
    t = ATT_TILE
    splits = [3 * A_WIDTH, 3 * A_WIDTH + B_Q_RANK, 3 * A_WIDTH + B_Q_RANK + B_KV_RANK]
    wa = w_in[:, :splits[0]].astype(BF16)
    wcq = w_in[:, splits[0]:splits[1]].astype(BF16)
    wckv = w_in[:, splits[1]:splits[2]].astype(BF16)
    wkr_f = w_in[:, splits[2]:]
    wkr = wkr_f.astype(BF16)
    wkrr = _rot_half_cols(wkr_f).astype(BF16)

    qd = B_NOPE_DIM + B_ROPE_DIM
    w_uq3 = w_uq.reshape(B_Q_RANK, B_HEADS, qd)
    zpad = jnp.zeros((B_Q_RANK, B_HEADS, LANES - qd), F32)
    wuqa = jnp.concatenate([w_uq3, zpad], axis=-1).reshape(B_Q_RANK, B_HEADS * LANES).T.astype(BF16)
    wuqb = jnp.concatenate([jnp.zeros((B_Q_RANK, B_HEADS, B_NOPE_DIM), F32),
                            _rot_half_cols(w_uq3[..., B_NOPE_DIM:]), zpad],
                           axis=-1).reshape(B_Q_RANK, B_HEADS * LANES).T.astype(BF16)
    w_ukv3 = w_ukv.reshape(B_KV_RANK, B_HEADS, B_NOPE_DIM + B_V_DIM)
    wukn = jnp.concatenate([w_ukv3[..., :B_NOPE_DIM],
                            jnp.zeros((B_KV_RANK, B_HEADS, LANES - B_NOPE_DIM), F32)],
                           axis=-1).reshape(B_KV_RANK, B_HEADS * LANES).astype(BF16)
    wuv = w_ukv3[..., B_NOPE_DIM:].reshape(B_KV_RANK, B_HEADS * B_V_DIM).T.astype(BF16)
    place = np.zeros((B_ROPE_DIM, B_HEADS * LANES), np.float32)
    for h in range(B_HEADS):
        for r in range(B_ROPE_DIM):
            place[r, h * LANES + B_NOPE_DIM + r] = 1.0
    place = jnp.asarray(place, BF16)

    pos = jnp.arange(s, dtype=F32)
    inv_freq = ROPE_BASE ** (-jnp.arange(0, B_ROPE_DIM, 2, dtype=F32) / B_ROPE_DIM)
    ang = pos[:, None] * inv_freq[None, :]
    cos = jnp.concatenate([jnp.cos(ang)] * 2, axis=-1)
    sin = jnp.concatenate([jnp.sin(ang)] * 2, axis=-1)
    ctab = jnp.concatenate([jnp.ones((s, B_NOPE_DIM), F32), cos, jnp.zeros((s, LANES - qd), F32)], axis=-1).T
    stab = jnp.concatenate([jnp.zeros((s, B_NOPE_DIM), F32), sin, jnp.zeros((s, LANES - qd), F32)], axis=-1).T

    row = lambda w: pl.BlockSpec((t, w), lambda i: (i, 0))
    col = pl.BlockSpec((LANES, t), lambda i: (0, i))
    tr = lambda w: pl.BlockSpec((1, w, t), lambda i: (i, 0, 0))
    consts = [wa, wcq, wckv, wkr, wkrr, q_norm.reshape(1, -1), kv_norm.reshape(1, -1),
              wuqa, wuqb, wukn, wuv, place]
    sds = jax.ShapeDtypeStruct
    return pl.pallas_call(
        _prep_ab_kernel,
        grid=(s // t,),
        in_specs=[row(D_MODEL)] + [_full(c.shape) for c in consts]
        + [col, col, row(B_ROPE_DIM), row(B_ROPE_DIM)],
        out_specs=[row(A_WIDTH), row(A_WIDTH), row(A_WIDTH),
                   tr(B_HEADS * LANES), row(B_HEADS * LANES), tr(B_HEADS * B_V_DIM)],
        out_shape=[sds((s, A_WIDTH), BF16)] * 3
        + [sds((s // t, B_HEADS * LANES, t), BF16), sds((s, B_HEADS * LANES), BF16),
           sds((s // t, B_HEADS * B_V_DIM, t), BF16)],
        compiler_params=_params("arbitrary"),
        name="prep_ab",
    )(x, *consts, ctab, stab, cos, sin)


def _attn_a_kernel(q_ref, kp_ref, kc_ref, vp_ref, vc_ref, bias_ref, o_ref, kbuf, vbuf):
    j = pl.program_id(1)
    t = ATT_TILE
    kbuf[0:t, :] = kp_ref[...]
    kbuf[t:2 * t, :] = kc_ref[...]
    vbuf[0:t, :] = vp_ref[...]
    vbuf[t:2 * t, :] = vc_ref[...]
    lo = lax.broadcasted_iota(I32, (1, LANES), 1) < A_HEAD_DIM
    col = lax.broadcasted_iota(I32, (1, BAND_WINDOW), 1)
    for g in range(t // BAND_GROUP):
        r0 = g * BAND_GROUP
        q = q_ref[r0:r0 + BAND_GROUP, :]
        k = kbuf[r0:r0 + BAND_WINDOW, :]
        v = vbuf[r0:r0 + BAND_WINDOW, :]
        valid = col + r0 >= jnp.where(j > 0, 0, t)
        outs = []
        for hh in range(2):
            qm = jnp.where(lo if hh == 0 else jnp.logical_not(lo), q, jnp.zeros_like(q))
            sc = lax.dot_general(qm, k, NT_DIMS, preferred_element_type=F32)
            sc = jnp.where(valid, sc + bias_ref[hh], NEG_INF)
            m = jnp.max(sc, axis=-1, keepdims=True)
            p = jnp.exp(sc - m)
            l = jnp.sum(p, axis=-1, keepdims=True)
            o = jnp.dot(p.astype(BF16), v, preferred_element_type=F32)
            outs.append(o / l)
        o_ref[r0:r0 + BAND_GROUP, :] = jnp.where(lo, outs[0], outs[1]).astype(BF16)


def _band_bias(rel_bias):
    qi = np.arange(BAND_GROUP)[:, None]
    kr = np.arange(BAND_WINDOW)[None, :]
    dchunk = (qi // CHUNK + A_LEFT_CHUNKS) - kr // CHUNK
    allowed = (dchunk >= 0) & (dchunk <= A_LEFT_CHUNKS)
    d_pos = np.arange(BAND_WINDOW)
    d_neg = np.arange(-BAND_GROUP, 0)
    to_idx = lambda d: np.clip(A_LEFT_CHUNKS * CHUNK - d, -A_REL_CLIP, A_REL_CLIP) + A_REL_CLIP
    tab = rel_bias.astype(F32).T
    table = _toeplitz(tab[:, to_idx(d_pos)], tab[:, to_idx(d_neg)], BAND_GROUP, BAND_WINDOW)
    return jnp.where(jnp.asarray(allowed)[None], table, NEG_INF)


def _attn_a(qa, ka, va, rel_bias):
    s = qa.shape[0]
    t = ATT_TILE
    bias = _band_bias(rel_bias)
    cur = pl.BlockSpec((t, LANES), lambda p, j: (j, p))
    prev = pl.BlockSpec((t, LANES), lambda p, j: (jnp.maximum(j - 1, 0), p))
    return pl.pallas_call(
        _attn_a_kernel,
        grid=(A_HEADS // 2, s // t),
        in_specs=[cur, prev, cur, prev, cur,
                  pl.BlockSpec((2, BAND_GROUP, BAND_WINDOW), lambda p, j: (p, 0, 0))],
        out_specs=cur,
        out_shape=jax.ShapeDtypeStruct((s, A_WIDTH), BF16),
        scratch_shapes=[pltpu.VMEM((2 * t, LANES), BF16), pltpu.VMEM((2 * t, LANES), BF16)],
        compiler_params=_params("arbitrary", "arbitrary"),
        name="attn_a",
    )(qa, ka, ka, va, va, bias)


class _FlashState:
    def __init__(self, m, l, acc, s0, s1, p0, p1, a0, a1):
        self.m, self.l, self.acc = m, l, acc
        self.s, self.p, self.a = (s0, s1), (p0, p1), (a0, a1)
        self.n_chunks = m.shape[1] // COL_CHUNK


def _flash_scratch(n_cols):
    t, w = ATT_TILE, COL_CHUNK
    return [pltpu.VMEM((1, n_cols), F32), pltpu.VMEM((1, n_cols), F32), pltpu.VMEM((LANES, n_cols), F32),
            pltpu.VMEM((t, w), F32), pltpu.VMEM((t, w), F32), pltpu.VMEM((t, w), BF16), pltpu.VMEM((t, w), BF16),
            pltpu.VMEM((1, w), F32), pltpu.VMEM((1, w), F32)]


def _flash_start(st, score_fn):
    st.m[...] = jnp.full(st.m.shape, NEG_INF, F32)
    st.l[...] = jnp.zeros(st.l.shape, F32)
    st.acc[...] = jnp.zeros(st.acc.shape, F32)
    st.p[1][...] = jnp.zeros(st.p[1].shape, BF16)
    st.a[1][...] = jnp.ones(st.a[1].shape, F32)
    st.s[0][...] = score_fn(0, 0)


def _flash_values(st, vt_fn, kt, c, b):
    cs = slice(c * COL_CHUNK, (c + 1) * COL_CHUNK)
    lhs = jnp.concatenate([vt_fn(kt), jnp.ones((SUM_ROWS, ATT_TILE), BF16)], axis=0)
    pv = jnp.dot(lhs, st.p[b][...], preferred_element_type=F32)
    alpha = st.a[b][...]
    st.acc[:, cs] = alpha * st.acc[:, cs] + pv[:LANES]
    st.l[:, cs] = alpha * st.l[:, cs] + pv[LANES:LANES + 1]


def _flash_region(st, score_fn, vt_fn, tiles, has_next):
    units = [(ti, c) for ti in range(len(tiles)) for c in range(st.n_chunks)]
    assert len(units) % 2 == 0
    for u, (ti, c) in enumerate(units):
        b = u % 2
        kt, bias_fn = tiles[ti]
        if u + 1 < len(units):
            nti, nc = units[u + 1]
            st.s[1 - b][...] = score_fn(tiles[nti][0], nc)
        elif has_next:
            st.s[1 - b][...] = score_fn(tiles[-1][0] + 1, 0)
        if u > 0:
            pti, pc = units[u - 1]
            _flash_values(st, vt_fn, tiles[pti][0], pc, 1 - b)
        else:
            _flash_values(st, vt_fn, jnp.maximum(kt - 1, 0), st.n_chunks - 1, 1 - b)
        cs = slice(c * COL_CHUNK, (c + 1) * COL_CHUNK)
        c0 = (c * COL_CHUNK) % ATT_TILE

        def rows(r, s_ref=st.s[b], bias_fn=bias_fn, c0=c0):
            blk = s_ref[r * ROW_CHUNK:(r + 1) * ROW_CHUNK, :]
            if bias_fn is not None:
                blk = blk + bias_fn(c0, r)
            return blk

        n_rows = ATT_TILE // ROW_CHUNK
        part = rows(0)
        for r in range(1, n_rows):
            part = jnp.maximum(part, rows(r))
        m_prev = st.m[:, cs]
        m_new = jnp.maximum(m_prev, jnp.max(part, axis=0, keepdims=True))
        st.a[b][...] = jnp.exp2(m_prev - m_new)
        st.m[:, cs] = m_new
        for r in range(n_rows):
            st.p[b][r * ROW_CHUNK:(r + 1) * ROW_CHUNK, :] = jnp.exp2(rows(r) - m_new).astype(BF16)


def _flash_sweep(st, score_fn, vt_fn, n_plain, tail_regions, last_tile):
    _flash_start(st, score_fn)

    def pair(j, carry):
        _flash_region(st, score_fn, vt_fn, [(2 * j, None), (2 * j + 1, None)], True)
        return carry

    lax.fori_loop(0, lax.shift_right_logical(n_plain, 1), pair, 0)

    @pl.when((n_plain & 1) == 1)
    def _():
        _flash_region(st, score_fn, vt_fn, [(n_plain - 1, None)], True)

    tail_regions()
    _flash_values(st, vt_fn, last_tile, st.n_chunks - 1, 1)


def _chunk_causal_bias_t(t):
    r = np.arange(t)
    ok = (r[:, None] // CHUNK) <= (r[None, :] // CHUNK)
    return jnp.asarray(np.where(ok, 0.0, NEG_INF), F32)


def _attn_b_kernel(qt_ref, k_ref, vt_ref, diag_ref, o_ref, *scratch):
    i = pl.program_id(1)
    t = ATT_TILE
    st = _FlashState(*scratch)
    per_head = t // COL_CHUNK

    def score(kt, c):
        hh = c // per_head
        c0 = (c % per_head) * COL_CHUNK
        k = k_ref[pl.ds(pl.multiple_of(kt * t, t), t), hh * LANES:(hh + 1) * LANES]
        return jnp.dot(k, qt_ref[0, hh * LANES:(hh + 1) * LANES, c0:c0 + COL_CHUNK],
                       preferred_element_type=F32)

    vt_fn = lambda kt: vt_ref[kt]

    def tail():
        diag = lambda c0, r: diag_ref[r * ROW_CHUNK:(r + 1) * ROW_CHUNK, c0:c0 + COL_CHUNK]
        _flash_region(st, score, vt_fn, [(i, diag)], False)

    _flash_sweep(st, score, vt_fn, i, tail, i)
    o2 = st.acc[...] / st.l[...]
    ot = jnp.concatenate([o2[:B_V_DIM, :t], o2[B_V_DIM:, t:]], axis=0)
    o_ref[...] = ot.T.astype(BF16)


def _attn_b(qbt, kb, vbt):
    nb, _, t = qbt.shape
    s = nb * t
    return pl.pallas_call(
        _attn_b_kernel,
        grid=(B_HEADS // 2, nb),
        in_specs=[pl.BlockSpec((1, 2 * LANES, t), lambda p, i: (i, p, 0)),
                  pl.BlockSpec((s, 2 * LANES), lambda p, i: (0, p)),
                  pl.BlockSpec((nb, LANES, t), lambda p, i: (0, p, 0)),
                  _full((t, t))],
        out_specs=pl.BlockSpec((t, LANES), lambda p, i: (i, p)),
        out_shape=jax.ShapeDtypeStruct((s, B_HEADS * B_V_DIM), BF16),
        scratch_shapes=_flash_scratch(2 * t),
        compiler_params=_params("arbitrary", "arbitrary"),
        name="attn_b",
    )(qbt, kb, vbt, _chunk_causal_bias_t(t))


def _attn_c_kernel(lam_ref, qt_ref, k_ref, vt_ref, d0_ref, d1_ref, g_ref, o_ref, q2_ref, *scratch, out_scale):
    i = pl.program_id(1)
    t = ATT_TILE
    st = _FlashState(*scratch)
    d = C_HEAD_DIM
    zero = jnp.zeros((d, t), BF16)
    q2_ref[:d, :t] = qt_ref[0, :d, :]
    q2_ref[d:, :t] = zero
    q2_ref[:d, t:] = zero
    q2_ref[d:, t:] = qt_ref[0, d:, :]

    def score(kt, c):
        return jnp.dot(k_ref[pl.ds(pl.multiple_of(kt * t, t), t), :],
                       q2_ref[:, c * COL_CHUNK:(c + 1) * COL_CHUNK], preferred_element_type=F32)

    vt_fn = lambda kt: vt_ref[kt]

    def tail():
        tile_of = lambda ref: lambda c0, r: ref[0, r * ROW_CHUNK:(r + 1) * ROW_CHUNK, c0:c0 + COL_CHUNK]
        diag = (i, tile_of(d0_ref))

        @pl.when(i > 0)
        def _():
            below = (i - 1, tile_of(d1_ref))
            _flash_region(st, score, vt_fn, [below, diag], False)

        @pl.when(i == 0)
        def _():
            _flash_region(st, score, vt_fn, [diag], False)

    _flash_sweep(st, score, vt_fn, jnp.maximum(i - 1, 0), tail, i)

    o2 = st.acc[...] / st.l[...]
    ot = o2[:, :t] - lam_ref[0] * o2[:, t:]
    inv = lax.rsqrt(jnp.mean(ot * ot, axis=0, keepdims=True) + RMS_EPS)
    o_ref[...] = (ot * inv * g_ref[...] * out_scale).T.astype(BF16)


def _t5_bucket(rel):
    half = T5_BUCKETS // 2
    max_exact = half // 2
    ret = jnp.where(rel > 0, half, 0)
    n = jnp.abs(rel)
    large = max_exact + (jnp.log(jnp.maximum(n, max_exact).astype(F32) / max_exact)
                         / math.log(T5_MAX_DIST / max_exact) * (half - max_exact)).astype(I32)
    large = jnp.minimum(large, half - 1)
    return ret + jnp.where(n < max_exact, n, large)


def _t5_tiles_t(t5_table, t):
    assert t >= T5_MAX_DIST
    far_bucket = T5_BUCKETS // 2 - 1
    tab = (t5_table.astype(F32) - t5_table[far_bucket].astype(F32)[None, :]).T
    by_rel = lambda rel: tab[:, _t5_bucket(rel)]
    d_pos = jnp.arange(t, dtype=I32)
    d_neg = jnp.arange(-t, 0, dtype=I32)
    d0 = _toeplitz(by_rel(-d_pos), by_rel(-d_neg), t, t)
    d1 = _toeplitz(by_rel(-d_pos - t), by_rel(-d_neg - t), t, t)
    r = np.arange(t)
    ok = (r[:, None] // CHUNK) <= (r[None, :] // CHUNK)
    d0 = jnp.where(jnp.asarray(ok)[None], d0 * LOG2E, NEG_INF)
    return d0, d1 * LOG2E


def _attn_c(qt, k, vt, lam, subln_g, t5_table, lam_init):
    nb, _, t = qt.shape
    s = nb * t
    d0, d1 = _t5_tiles_t(t5_table, t)
    return pl.pallas_call(
        functools.partial(_attn_c_kernel, out_scale=1.0 - lam_init),
        grid=(C_HEADS, nb),
        in_specs=[pl.BlockSpec(memory_space=pltpu.SMEM),
                  pl.BlockSpec((1, LANES, t), lambda h, i: (i, h, 0)),
                  pl.BlockSpec((s, LANES), lambda h, i: (0, h)),
                  pl.BlockSpec((nb, LANES, t), lambda h, i: (0, h, 0)),
                  pl.BlockSpec((1, t, t), lambda h, i: (h, 0, 0)),
                  pl.BlockSpec((1, t, t), lambda h, i: (h, 0, 0)),
                  _full((LANES, 1))],
        out_specs=pl.BlockSpec((t, LANES), lambda h, i: (i, h)),
        out_shape=jax.ShapeDtypeStruct((s, C_HEADS * 2 * C_HEAD_DIM), BF16),
        scratch_shapes=[pltpu.VMEM((LANES, 2 * t), BF16)] + _flash_scratch(2 * t),
        compiler_params=_params("arbitrary", "arbitrary"),
        name="attn_c",
    )(lam.reshape(1), qt, k, vt, d0, d1, subln_g.reshape(-1, 1))


def _prep_c_kernel(x_ref, wq_ref, wk_ref, wv_ref, qt_ref, k_ref, vt_ref):
    xb = x_ref[...].astype(BF16)
    qt = lax.dot_general(wq_ref[...], xb, NT_DIMS, preferred_element_type=F32)
    qt_ref[0] = (qt * (C_HEAD_DIM ** -0.5 * LOG2E)).astype(BF16)
    k_ref[...] = jnp.dot(xb, wk_ref[...], preferred_element_type=F32).astype(BF16)
    vt_ref[0] = lax.dot_general(wv_ref[...], xb, NT_DIMS, preferred_element_type=F32).astype(BF16)


def _prep_c(x, w_in):
    s = x.shape[0]
    t = ATT_TILE
    wq = w_in[:, :C_QK].T.astype(BF16)
    wk = w_in[:, C_QK:2 * C_QK].astype(BF16)
    wv = w_in[:, 2 * C_QK:].T.astype(BF16)
    row = pl.BlockSpec((t, D_MODEL), lambda i: (i, 0))
    tr = pl.BlockSpec((1, C_QK, t), lambda i: (i, 0, 0))
    sds = jax.ShapeDtypeStruct
    return pl.pallas_call(
        _prep_c_kernel,
        grid=(s // t,),
        in_specs=[row, _full(wq.shape), _full(wk.shape), _full(wv.shape)],
        out_specs=[tr, row, tr],
        out_shape=[sds((s // t, C_QK, t), BF16), sds((s, C_QK), BF16), sds((s // t, C_QK, t), BF16)],
        compiler_params=_params("arbitrary"),
        name="prep_c",
    )(x, wq, wk, wv)


def _post_attn_kernel(*refs, n_in):
    x_ref = refs[0]
    o_refs = refs[1:1 + n_in]
    w_refs = refs[1 + n_in:1 + 2 * n_in]
    g_ref, b_ref, out_ref = refs[1 + 2 * n_in:]
    y = DEEPNORM_ALPHA * x_ref[...]
    for o_ref, w_ref in zip(o_refs, w_refs):
        y = y + jnp.dot(o_ref[...], w_ref[...], preferred_element_type=F32)
    _store_row_tiles(out_ref, _layer_norm(y, g_ref[...], b_ref[...]))


def _post_attn(x, outs, weights, g, b):
    s = x.shape[0]
    t = ATT_TILE
    row = lambda w: pl.BlockSpec((t, w), lambda i: (i, 0))
    ws = [w.astype(BF16) for w in weights]
    return pl.pallas_call(
        functools.partial(_post_attn_kernel, n_in=len(outs)),
        grid=(s // t,),
        in_specs=[row(D_MODEL)] + [row(o.shape[1]) for o in outs] + [_full(w.shape) for w in ws]
        + [_full((1, D_MODEL))] * 2,
        out_specs=pl.BlockSpec((t, ROW_SUBLANES, LANES), lambda i: (i, 0, 0)),
        out_shape=jax.ShapeDtypeStruct((s, ROW_SUBLANES, LANES), F32),
        compiler_params=_params("arbitrary"),
        name="post_attn",
    )(x, *outs, *ws, g.reshape(1, -1), b.reshape(1, -1))


def _router_kernel(x_ref, wr_ref, br_ref, tri_ref, idx_ref, rank_ref, w_ref, cnt_ref, carry):
    step = pl.program_id(0)

    @pl.when(step == 0)
    def _():
        carry[...] = jnp.zeros(carry.shape, F32)

    t = ROUTER_TILE
    logits = lax.dot_general(wr_ref[...], _load_row_tiles(x_ref), NT_DIMS, preferred_element_type=F32,
                             precision=lax.Precision.HIGHEST) + br_ref[...]
    eid = lax.broadcasted_iota(I32, (N_EXPERTS, t), 0).astype(F32)
    work = logits
    vals, sels = [], []
    for _ in range(TOP_K):
        m = jnp.max(work, axis=0, keepdims=True)
        sel = jnp.min(jnp.where(work == m, eid, float(N_EXPERTS)), axis=0, keepdims=True)
        vals.append(m)
        sels.append(sel)
        work = jnp.where(eid == sel, -jnp.inf, work)
    exps = [jnp.exp(v - vals[0]) for v in vals]
    denom = exps[0] + exps[1] + exps[2] + exps[3]

    chosen = jnp.zeros((N_EXPERTS, t), F32)
    for sel in sels:
        chosen = chosen + (eid == sel).astype(F32)
    before = jnp.dot(chosen.astype(BF16), tri_ref[...], preferred_element_type=F32) + carry[...]
    for kk, sel in enumerate(sels):
        rank = jnp.sum(jnp.where(eid == sel, before, 0.0), axis=0, keepdims=True)
        rank_ref[kk:kk + 1, :] = rank.astype(I32)
        idx_ref[kk:kk + 1, :] = sel.astype(I32)
    carry[...] = carry[...] + jnp.sum(chosen, axis=1, keepdims=True)
    cnt_ref[...] = jnp.broadcast_to(carry[...], cnt_ref.shape)

    sub = lax.broadcasted_iota(I32, (LANES, t), 0)
    wrows = jnp.zeros((LANES, t), F32)
    for kk in range(TOP_K):
        wrows = jnp.where(sub == kk, exps[kk] / denom, wrows)
    w_ref[...] = wrows.T


def _router(x1, router_w, router_b):
    s = x1.shape[0]
    t = ROUTER_TILE
    tri = jnp.asarray(np.triu(np.ones((t, t), np.float32), k=1), BF16)
    tok = pl.BlockSpec((TOP_K, t), lambda i: (0, i))
    return pl.pallas_call(
        _router_kernel,
        grid=(s // t,),
        in_specs=[pl.BlockSpec((t, ROW_SUBLANES, LANES), lambda i: (i, 0, 0)),
                  _full((N_EXPERTS, D_MODEL)), _full((N_EXPERTS, 1)), _full((t, t))],
        out_specs=[tok, tok, pl.BlockSpec((t, LANES), lambda i: (i, 0)), _full((N_EXPERTS, LANES))],
        out_shape=[jax.ShapeDtypeStruct((TOP_K, s), I32), jax.ShapeDtypeStruct((TOP_K, s), I32),
                   jax.ShapeDtypeStruct((s, LANES), F32), jax.ShapeDtypeStruct((N_EXPERTS, LANES), F32)],
        scratch_shapes=[pltpu.VMEM((N_EXPERTS, 1), F32)],
        compiler_params=_params("arbitrary"),
        name="router",
    )(x1, router_w.T, router_b.reshape(-1, 1), tri)


def _gmm_kernel(te_ref, nt_ref, src_hbm, x_hbm, wi_ref, bi_ref, wo_ref, bo_ref, y_ref,
                wi_bf, wo_bf, xbuf, src_smem, src_sem, row_sem):
    i = pl.program_id(0)
    n_tiles = nt_ref[0]
    tm = EXPERT_TILE
    group = SRC_GROUP * tm

    def half_base(tile):
        return (lax.shift_right_logical(tile, SRC_GROUP.bit_length() - 1) & 1) * group

    def fetch_indices(tile):
        dst = pl.multiple_of(half_base(tile), group)
        cp = pltpu.make_async_copy(src_hbm.at[pl.ds(pl.multiple_of(tile * tm, group), group)],
                                   src_smem.at[pl.ds(dst, group)], src_sem)
        cp.start()
        cp.wait()

    def issue_tile(tile):
        slot = tile & 1
        base = half_base(tile) + (tile & (SRC_GROUP - 1)) * tm

        def body(r, carry):
            tok = src_smem[base + r]
            pltpu.make_async_copy(x_hbm.at[tok], xbuf.at[slot, r], row_sem.at[slot]).start()
            return carry

        lax.fori_loop(0, tm, body, 0, unroll=8)

    def wait_tile(tile):
        slot = tile & 1
        pltpu.make_async_copy(x_hbm.at[pl.ds(0, tm)], xbuf.at[slot], row_sem.at[slot]).wait()

    @pl.when(i == 0)
    def _():
        fetch_indices(i)
        issue_tile(i)

    nxt = i + 1

    @pl.when(nxt < n_tiles)
    def _():
        @pl.when((nxt & (SRC_GROUP - 1)) == 0)
        def _():
            fetch_indices(nxt)

        issue_tile(nxt)

    prev = te_ref[jnp.maximum(i - 1, 0)]
    fresh = jnp.logical_or(i == 0, te_ref[i] != prev)

    @pl.when(jnp.logical_and(i < n_tiles, fresh))
    def _():
        def cast(c, carry):
            rows = pl.ds(pl.multiple_of(c * CAST_ROWS, CAST_ROWS), CAST_ROWS)
            wi_bf[rows, :] = wi_ref[0, rows, :].astype(BF16)
            wo_bf[rows, :] = wo_ref[0, rows, :].astype(BF16)
            return carry

        lax.fori_loop(0, D_MODEL // CAST_ROWS, cast, 0)

    @pl.when(i < n_tiles)
    def _():
        wait_tile(i)
        xb = _load_row_tiles(xbuf, i & 1).astype(BF16)
        h = jnp.dot(xb, wi_bf[...], preferred_element_type=F32) + bi_ref[0]
        glu = jnp.minimum(h[:, :D_FF], SWIGLU_LIMIT)
        lin = jnp.clip(h[:, D_FF:], -SWIGLU_LIMIT, SWIGLU_LIMIT)
        act = glu * jax.nn.sigmoid(SWIGLU_ALPHA * glu) * (lin + 1.0)
        _store_row_tiles(y_ref, jnp.dot(act.astype(BF16), wo_bf[...], preferred_element_type=F32) + bo_ref[0])

    @pl.when(i >= n_tiles)
    def _():
        y_ref[...] = jnp.zeros(y_ref.shape, F32)


def _gmm(layer, tile_expert, n_tiles, src, x1, w_in, b_in, w_out, b_out):
    n_rows = src.shape[0]
    tm = EXPERT_TILE
    assert n_rows % (SRC_GROUP * tm) == 0
    last = lambda i, nt: jnp.maximum(jnp.minimum(i, nt[0] - 1), 0)
    expert = lambda i, te, nt: layer * N_EXPERTS + te[last(i, nt)]
    any_spec = pl.BlockSpec(memory_space=pl.ANY)
    grid_spec = pltpu.PrefetchScalarGridSpec(
        num_scalar_prefetch=2,
        grid=(n_rows // tm,),
        in_specs=[any_spec, any_spec,
                  pl.BlockSpec((1, D_MODEL, 2 * D_FF), lambda i, te, nt: (expert(i, te, nt), 0, 0)),
                  pl.BlockSpec((1, 1, 2 * D_FF), lambda i, te, nt: (expert(i, te, nt), 0, 0)),
                  pl.BlockSpec((1, D_FF, D_MODEL), lambda i, te, nt: (expert(i, te, nt), 0, 0)),
                  pl.BlockSpec((1, 1, D_MODEL), lambda i, te, nt: (expert(i, te, nt), 0, 0))],
        out_specs=pl.BlockSpec((tm, ROW_SUBLANES, LANES), lambda i, te, nt: (i, 0, 0)),
        scratch_shapes=[pltpu.VMEM((D_MODEL, 2 * D_FF), BF16), pltpu.VMEM((D_FF, D_MODEL), BF16),
                        pltpu.VMEM((2, tm, ROW_SUBLANES, LANES), F32), pltpu.SMEM((2 * SRC_GROUP * tm,), I32),
                        pltpu.SemaphoreType.DMA, pltpu.SemaphoreType.DMA((2,))],
    )
    return pl.pallas_call(
        _gmm_kernel,
        grid_spec=grid_spec,
        out_shape=jax.ShapeDtypeStruct((n_rows, ROW_SUBLANES, LANES), F32),
        compiler_params=_params("arbitrary"),
        name="expert_mlp",
    )(tile_expert, n_tiles, src, x1, w_in, b_in, w_out, b_out)


def _combine_kernel(pos_hbm, y_hbm, x_ref, w_ref, p_ref, wg_ref, wp_ref, g_ref, b_ref, out_ref,
                    pos_smem, ybuf, pos_sem, row_sem):
    i = pl.program_id(0)
    t = COMBINE_TILE
    n = t * TOP_K
    cp = pltpu.make_async_copy(pos_hbm.at[pl.ds(pl.multiple_of(i * n, n), n)], pos_smem, pos_sem)
    cp.start()
    cp.wait()

    def issue(tok, carry):
        for kk in range(TOP_K):
            pltpu.make_async_copy(y_hbm.at[pos_smem[tok * TOP_K + kk]], ybuf.at[kk, tok], row_sem).start()
        return carry

    lax.fori_loop(0, t, issue, 0, unroll=2)

    x = _load_row_tiles(x_ref)
    gate = jax.nn.sigmoid(jnp.dot(x.astype(BF16), wg_ref[...], preferred_element_type=F32))
    proj = jnp.dot(p_ref[...].astype(BF16), wp_ref[...], preferred_element_type=F32)
    y = DEEPNORM_ALPHA * x + gate * proj

    for kk in range(TOP_K):
        pltpu.make_async_copy(y_hbm.at[pl.ds(0, t)], ybuf.at[kk], row_sem).wait()
    w = w_ref[...]
    for kk in range(TOP_K):
        y = y + w[:, kk:kk + 1] * _load_row_tiles(ybuf, kk)
    out_ref[...] = _layer_norm(y, g_ref[...], b_ref[...])


def _combine(pos_flat, y, x1, wcol, p, gate_w, proj_w, g, b):
    s = x1.shape[0]
    t = COMBINE_TILE
    any_spec = pl.BlockSpec(memory_space=pl.ANY)
    row = lambda w: pl.BlockSpec((t, w), lambda i: (i, 0))
    return pl.pallas_call(
        _combine_kernel,
        grid=(s // t,),
        in_specs=[any_spec, any_spec, pl.BlockSpec((t, ROW_SUBLANES, LANES), lambda i: (i, 0, 0)),
                  row(LANES), row(PLE_DIM),
                  _full((D_MODEL, D_MODEL)), _full((PLE_DIM, D_MODEL)),
                  _full((1, D_MODEL)), _full((1, D_MODEL))],
        out_specs=row(D_MODEL),
        out_shape=jax.ShapeDtypeStruct((s, D_MODEL), F32),
        scratch_shapes=[pltpu.SMEM((t * TOP_K,), I32), pltpu.VMEM((TOP_K, t, ROW_SUBLANES, LANES), F32),
                        pltpu.SemaphoreType.DMA, pltpu.SemaphoreType.DMA],
        compiler_params=_params("arbitrary"),
        name="combine",
    )(pos_flat, y, x1, wcol, p, gate_w.astype(BF16), proj_w.astype(BF16), g.reshape(1, -1), b.reshape(1, -1))


def _moe_block(layer, x1, p, router_w, router_b, w_in, b_in, w_out, b_out, gate_w, proj_w, g, b):
    s = x1.shape[0]
    tm = EXPERT_TILE
    n_tiles_max = s * TOP_K // tm + N_EXPERTS
    idx, rank, wcol, cnt = _router(x1, router_w, router_b)
    counts = cnt[:, 0].astype(I32)
    tiles_per = (counts + tm - 1) // tm
    tile_end = jnp.cumsum(tiles_per)
    offsets = (tile_end - tiles_per) * tm
    experts = jnp.arange(N_EXPERTS, dtype=I32)
    pos = rank + jnp.sum(jnp.where(idx[..., None] == experts, offsets, 0), axis=-1)
    pos_flat = pos.T.reshape(-1)
    tile_ids = jnp.arange(n_tiles_max, dtype=I32)
    tile_expert = jnp.minimum(jnp.sum((tile_end[None, :] <= tile_ids[:, None]).astype(I32), axis=-1),
                              N_EXPERTS - 1)
    n_tiles = tile_end[-1:].astype(I32)
    src = jnp.zeros((n_tiles_max * tm,), I32).at[pos_flat].set(jnp.arange(s * TOP_K, dtype=I32) // TOP_K)
    y = _gmm(layer, tile_expert, n_tiles, src, x1, w_in, b_in, w_out, b_out)
    return _combine(pos_flat, y, x1, wcol, p, gate_w, proj_w, g, b)


def _lambda_init(layer_idx):
    return 0.8 - 0.6 * math.exp(-0.3 * layer_idx)


def kernel(x, p, ab_w_in, ab_rel_bias, ab_q_norm, ab_kv_norm, ab_w_uq, ab_w_ukv, ab_w_out, c_w_in, c_lambda, c_subln, c_w_out, t5_table, ln_mix_g, ln_mix_b, ln_ffn_g, ln_ffn_b, router_w, router_b, exp_w_in, exp_b_in, exp_w_out, exp_b_out, ple_gate_w, ple_proj_w):
    batch, s, _ = x.shape
    assert batch == 1 and s % (SRC_GROUP * EXPERT_TILE) == 0
    xr = x[0]
    w_in_all = exp_w_in.reshape(DEPTH * N_EXPERTS, D_MODEL, 2 * D_FF)
    b_in_all = exp_b_in.reshape(DEPTH * N_EXPERTS, 1, 2 * D_FF)
    w_out_all = exp_w_out.reshape(DEPTH * N_EXPERTS, D_FF, D_MODEL)
    b_out_all = exp_b_out.reshape(DEPTH * N_EXPERTS, 1, D_MODEL)
    for i in range(DEPTH):
        j = i // 2
        if i % 2 == 0:
            qa, ka, va, qbt, kb, vbt = _prep_ab(xr, ab_w_in[j], ab_q_norm[j], ab_kv_norm[j], ab_w_uq[j],
                                                ab_w_ukv[j])
            o_a = _attn_a(qa, ka, va, ab_rel_bias[j])
            o_b = _attn_b(qbt, kb, vbt)
            x1 = _post_attn(xr, [o_a, o_b], [ab_w_out[j][:A_WIDTH], ab_w_out[j][A_WIDTH:]],
                            ln_mix_g[i], ln_mix_b[i])
        else:
            qt, k, vt = _prep_c(xr, c_w_in[j])
            lp = c_lambda[j].astype(F32)
            lam_init = _lambda_init(i)
            lam = jnp.exp(jnp.sum(lp[0] * lp[1])) - jnp.exp(jnp.sum(lp[2] * lp[3])) + lam_init
            o_c = _attn_c(qt, k, vt, lam, c_subln[j], t5_table, lam_init)
            x1 = _post_attn(xr, [o_c], [c_w_out[j]], ln_mix_g[i], ln_mix_b[i])
        xr = _moe_block(i, x1, p[i, 0], router_w[i], router_b[i], w_in_all, b_in_all, w_out_all, b_out_all,
                        ple_gate_w[i], ple_proj_w[i], ln_ffn_g[i], ln_ffn_b[i])
    return xr[None]
```

```python
import functools
import math

import jax
import jax.numpy as jnp
import numpy as np
from jax import lax
from jax.experimental import pallas as pl
from jax.experimental.pallas import tpu as pltpu

F32 = jnp.float32
BF16 = jnp.bfloat16
I32 = jnp.int32

D_MODEL = 1024
DEPTH = 4
CHUNK = 64
PLE_DIM = 256

A_HEADS = 8
A_HEAD_DIM = 64
A_LEFT_CHUNKS = 8
A_REL_CLIP = 128
A_WIDTH = A_HEADS * A_HEAD_DIM

B_HEADS = 8
B_NOPE_DIM = 64
B_ROPE_DIM = 32
B_V_DIM = 64
B_Q_RANK = 512
B_KV_RANK = 256
ROPE_BASE = 10000.0

C_HEADS = 8
C_HEAD_DIM = 64
C_QK = 2 * C_HEADS * C_HEAD_DIM

T5_BUCKETS = 32
T5_MAX_DIST = 128

N_EXPERTS = 32
TOP_K = 4
D_FF = 1024
SWIGLU_ALPHA = 1.702
SWIGLU_LIMIT = 7.0

DEEPNORM_ALPHA = (2.0 * DEPTH) ** 0.25
LN_EPS = 1e-5
RMS_EPS = 1e-6
NEG_INF = -1e30

LANES = 128
ROW_SUBLANES = D_MODEL // LANES
VMEM_LIMIT = 56 * 1024 * 1024

ATT_TILE = 512
COL_CHUNK = 256
ROW_CHUNK = 64
SUM_ROWS = 16
LOG2E = math.log2(math.e)
BAND_GROUP = 256
BAND_WINDOW = BAND_GROUP + A_LEFT_CHUNKS * CHUNK
ROUTER_TILE = 512
EXPERT_TILE = 256
SRC_GROUP = 4
COMBINE_TILE = 256
CAST_ROWS = 128
ISSUE_UNROLL = 8

NT_DIMS = (((1,), (1,)), ((), ()))


def _params(*sem):
    return pltpu.CompilerParams(dimension_semantics=sem, vmem_limit_bytes=VMEM_LIMIT)


def _full(shape):
    n = len(shape)
    return pl.BlockSpec(shape, lambda *_: (0,) * n)


def _layer_norm(y, g, b):
    mu = jnp.mean(y, axis=-1, keepdims=True)
    d = y - mu
    var = jnp.mean(d * d, axis=-1, keepdims=True)
    return d * lax.rsqrt(var + LN_EPS) * g + b


def _rms_norm(c, g):
    inv = lax.rsqrt(jnp.mean(c * c, axis=-1, keepdims=True) + RMS_EPS)
    return c * inv * g


def _load_row_tiles(ref, *lead):
    tiles = pltpu.einshape("ral->arl", ref[lead] if lead else ref[...])
    return jnp.concatenate([tiles[a] for a in range(ROW_SUBLANES)], axis=1)


def _store_row_tiles(ref, value):
    tiles = jnp.stack([value[:, a * LANES:(a + 1) * LANES] for a in range(ROW_SUBLANES)], axis=0)
    ref[...] = pltpu.einshape("arl->ral", tiles)


def _toeplitz(vals_pos, vals_neg, nr, nc):
    w = jnp.concatenate([vals_pos, vals_neg], axis=-1)
    lw = nc + nr
    assert w.shape[-1] == lw
    flat = jnp.tile(w, (1,) * (w.ndim - 1) + (nr,))[..., :nr * (lw - 1)]
    return flat.reshape(w.shape[:-1] + (nr, lw - 1))[..., :nc]


def _prep_ab_kernel(x_ref, wa_ref, wcq_ref, wckv_ref, wkr_ref, wkrr_ref, qn_ref, kvn_ref,
                    wuqa_ref, wuqb_ref, wukn_ref, wuv_ref, place_ref, ctab_ref, stab_ref,
                    cos_ref, sin_ref,
                    qa_ref, ka_ref, va_ref, qbt_ref, kb_ref, vbt_ref):
    xb = x_ref[...].astype(BF16)
    ha = jnp.dot(xb, wa_ref[...], preferred_element_type=F32)
    qa_ref[...] = (ha[:, :A_WIDTH] * (A_HEAD_DIM ** -0.5)).astype(BF16)
    ka_ref[...] = ha[:, A_WIDTH:2 * A_WIDTH].astype(BF16)
    va_ref[...] = ha[:, 2 * A_WIDTH:].astype(BF16)

    cq = jnp.dot(xb, wcq_ref[...], preferred_element_type=F32)
    cqn = _rms_norm(cq, qn_ref[...]).astype(BF16)
    qa_part = lax.dot_general(wuqa_ref[...], cqn, NT_DIMS, preferred_element_type=F32)
    qb_part = lax.dot_general(wuqb_ref[...], cqn, NT_DIMS, preferred_element_type=F32)
    ct = ctab_ref[...]
    st = stab_ref[...]
    scale = (B_NOPE_DIM + B_ROPE_DIM) ** -0.5 * LOG2E
    for h in range(B_HEADS):
        sl = slice(h * LANES, (h + 1) * LANES)
        qbt_ref[0, sl, :] = ((qa_part[sl, :] * ct + qb_part[sl, :] * st) * scale).astype(BF16)

    ckv = jnp.dot(xb, wckv_ref[...], preferred_element_type=F32)
    ckvn = _rms_norm(ckv, kvn_ref[...]).astype(BF16)
    kr = jnp.dot(xb, wkr_ref[...], preferred_element_type=F32)
    krr = jnp.dot(xb, wkrr_ref[...], preferred_element_type=F32)
    k_rope = (kr * cos_ref[...] + krr * sin_ref[...]).astype(BF16)
    kb = jnp.dot(ckvn, wukn_ref[...], preferred_element_type=F32)
    kb = kb + jnp.dot(k_rope, place_ref[...], preferred_element_type=F32)
    kb_ref[...] = kb.astype(BF16)
    vbt_ref[0] = lax.dot_general(wuv_ref[...], ckvn, NT_DIMS, preferred_element_type=F32).astype(BF16)


def _rot_half_cols(w):
    half = w.shape[-1] // 2
    return jnp.concatenate([-w[..., half:], w[..., :half]], axis=-1)


def _prep_ab(x, w_in, q_norm, kv_norm, w_uq, w_ukv):
    s = x.shape[0]
    t = ATT_TILE
    splits = [3 * A_WIDTH, 3 * A_WIDTH + B_Q_RANK, 3 * A_WIDTH + B_Q_RANK + B_KV_RANK]
    wa = w_in[:, :splits[0]].astype(BF16)
    wcq = w_in[:, splits[0]:splits[1]].astype(BF16)
    wckv = w_in[:, splits[1]:splits[2]].astype(BF16)
    wkr_f = w_in[:, splits[2]:]
    wkr = wkr_f.astype(BF16)
    wkrr = _rot_half_cols(wkr_f).astype(BF16)

    qd = B_NOPE_DIM + B_ROPE_DIM
    w_uq3 = w_uq.reshape(B_Q_RANK, B_HEADS, qd)
    zpad = jnp.zeros((B_Q_RANK, B_HEADS, LANES - qd), F32)
    wuqa = jnp.concatenate([w_uq3, zpad], axis=-1).reshape(B_Q_RANK, B_HEADS * LANES).T.astype(BF16)
    wuqb = jnp.concatenate([jnp.zeros((B_Q_RANK, B_HEADS, B_NOPE_DIM), F32),
                            _rot_half_cols(w_uq3[..., B_NOPE_DIM:]), zpad],
                           axis=-1).reshape(B_Q_RANK, B_HEADS * LANES).T.astype(BF16)
    w_ukv3 = w_ukv.reshape(B_KV_RANK, B_HEADS, B_NOPE_DIM + B_V_DIM)
    wukn = jnp.concatenate([w_ukv3[..., :B_NOPE_DIM],
                            jnp.zeros((B_KV_RANK, B_HEADS, LANES - B_NOPE_DIM), F32)],
                           axis=-1).reshape(B_KV_RANK, B_HEADS * LANES).astype(BF16)
    wuv = w_ukv3[..., B_NOPE_DIM:].reshape(B_KV_RANK, B_HEADS * B_V_DIM).T.astype(BF16)
    place = np.zeros((B_ROPE_DIM, B_HEADS * LANES), np.float32)
    for h in range(B_HEADS):
        for r in range(B_ROPE_DIM):
            place[r, h * LANES + B_NOPE_DIM + r] = 1.0
    place = jnp.asarray(place, BF16)

    pos = jnp.arange(s, dtype=F32)
    inv_freq = ROPE_BASE ** (-jnp.arange(0, B_ROPE_DIM, 2, dtype=F32) / B_ROPE_DIM)
    ang = pos[:, None] * inv_freq[None, :]
    cos = jnp.concatenate([jnp.cos(ang)] * 2, axis=-1)
    sin = jnp.concatenate([jnp.sin(ang)] * 2, axis=-1)
    ctab = jnp.concatenate([jnp.ones((s, B_NOPE_DIM), F32), cos, jnp.zeros((s, LANES - qd), F32)], axis=-1).T
    stab = jnp.concatenate([jnp.zeros((s, B_NOPE_DIM), F32), sin, jnp.zeros((s, LANES - qd), F32)], axis=-1).T

    row = lambda w: pl.BlockSpec((t, w), lambda i: (i, 0))
    col = pl.BlockSpec((LANES, t), lambda i: (0, i))
    tr = lambda w: pl.BlockSpec((1, w, t), lambda i: (i, 0, 0))
    consts = [wa, wcq, wckv, wkr, wkrr, q_norm.reshape(1, -1), kv_norm.reshape(1, -1),
              wuqa, wuqb, wukn, wuv, place]
    sds = jax.ShapeDtypeStruct
    return pl.pallas_call(
        _prep_ab_kernel,
        grid=(s // t,),
        in_specs=[row(D_MODEL)] + [_full(c.shape) for c in consts]
        + [col, col, row(B_ROPE_DIM), row(B_ROPE_DIM)],
        out_specs=[row(A_WIDTH), row(A_WIDTH), row(A_WIDTH),
                   tr(B_HEADS * LANES), row(B_HEADS * LANES), tr(B_HEADS * B_V_DIM)],
        out_shape=[sds((s, A_WIDTH), BF16)] * 3
        + [sds((s // t, B_HEADS * LANES, t), BF16), sds((s, B_HEADS * LANES), BF16),
           sds((s // t, B_HEADS * B_V_DIM, t), BF16)],
        compiler_params=_params("arbitrary"),
        name="prep_ab",
    )(x, *consts, ctab, stab, cos, sin)


def _attn_a_kernel(q_ref, kp_ref, kc_ref, vp_ref, vc_ref, bias_ref, o_ref, kbuf, vbuf):
    j = pl.program_id(1)
    t = ATT_TILE
    kbuf[0:t, :] = kp_ref[...]
    kbuf[t:2 * t, :] = kc_ref[...]
    vbuf[0:t, :] = vp_ref[...]
    vbuf[t:2 * t, :] = vc_ref[...]
    lo = lax.broadcasted_iota(I32, (1, LANES), 1) < A_HEAD_DIM
    col = lax.broadcasted_iota(I32, (1, BAND_WINDOW), 1)
    for g in range(t // BAND_GROUP):
        r0 = g * BAND_GROUP
        q = q_ref[r0:r0 + BAND_GROUP, :]
        k = kbuf[r0:r0 + BAND_WINDOW, :]
        v = vbuf[r0:r0 + BAND_WINDOW, :]
        valid = col + r0 >= jnp.where(j > 0, 0, t)
        outs = []
        for hh in range(2):
            qm = jnp.where(lo if hh == 0 else jnp.logical_not(lo), q, jnp.zeros_like(q))
            sc = lax.dot_general(qm, k, NT_DIMS, preferred_element_type=F32)
            sc = jnp.where(valid, sc + bias_ref[hh], NEG_INF)
            m = jnp.max(sc, axis=-1, keepdims=True)
            p = jnp.exp(sc - m)
            l = jnp.sum(p, axis=-1, keepdims=True)
            o = jnp.dot(p.astype(BF16), v, preferred_element_type=F32)
            outs.append(o / l)
        o_ref[r0:r0 + BAND_GROUP, :] = jnp.where(lo, outs[0], outs[1]).astype(BF16)


def _band_bias(rel_bias):
    qi = np.arange(BAND_GROUP)[:, None]
    kr = np.arange(BAND_WINDOW)[None, :]
    dchunk = (qi // CHUNK + A_LEFT_CHUNKS) - kr // CHUNK
    allowed = (dchunk >= 0) & (dchunk <= A_LEFT_CHUNKS)
    d_pos = np.arange(BAND_WINDOW)
    d_neg = np.arange(-BAND_GROUP, 0)
    to_idx = lambda d: np.clip(A_LEFT_CHUNKS * CHUNK - d, -A_REL_CLIP, A_REL_CLIP) + A_REL_CLIP
    tab = rel_bias.astype(F32).T
    table = _toeplitz(tab[:, to_idx(d_pos)], tab[:, to_idx(d_neg)], BAND_GROUP, BAND_WINDOW)
    return jnp.where(jnp.asarray(allowed)[None], table, NEG_INF)


def _attn_a(qa, ka, va, rel_bias):
    s = qa.shape[0]
    t = ATT_TILE
    bias = _band_bias(rel_bias)
    cur = pl.BlockSpec((t, LANES), lambda p, j: (j, p))
    prev = pl.BlockSpec((t, LANES), lambda p, j: (jnp.maximum(j - 1, 0), p))
    return pl.pallas_call(
        _attn_a_kernel,
        grid=(A_HEADS // 2, s // t),
        in_specs=[cur, prev, cur, prev, cur,
                  pl.BlockSpec((2, BAND_GROUP, BAND_WINDOW), lambda p, j: (p, 0, 0))],
        out_specs=cur,
        out_shape=jax.ShapeDtypeStruct((s, A_WIDTH), BF16),
        scratch_shapes=[pltpu.VMEM((2 * t, LANES), BF16), pltpu.VMEM((2 * t, LANES), BF16)],
        compiler_params=_params("arbitrary", "arbitrary"),
        name="attn_a",
    )(qa, ka, ka, va, va, bias)


class _FlashState:
    def __init__(self, m, l, acc, s0, s1, p0, p1, a0, a1):
        self.m, self.l, self.acc = m, l, acc
        self.s, self.p, self.a = (s0, s1), (p0, p1), (a0, a1)
        self.n_chunks = m.shape[1] // COL_CHUNK


def _flash_scratch(n_cols):
    t, w = ATT_TILE, COL_CHUNK
    return [pltpu.VMEM((1, n_cols), F32), pltpu.VMEM((1, n_cols), F32), pltpu.VMEM((LANES, n_cols), F32),
            pltpu.VMEM((t, w), F32), pltpu.VMEM((t, w), F32), pltpu.VMEM((t, w), BF16), pltpu.VMEM((t, w), BF16),
            pltpu.VMEM((1, w), F32), pltpu.VMEM((1, w), F32)]


def _flash_start(st, score_fn):
    st.m[...] = jnp.full(st.m.shape, NEG_INF, F32)
    st.l[...] = jnp.zeros(st.l.shape, F32)
    st.acc[...] = jnp.zeros(st.acc.shape, F32)
    st.p[1][...] = jnp.zeros(st.p[1].shape, BF16)
    st.a[1][...] = jnp.ones(st.a[1].shape, F32)
    st.s[0][...] = score_fn(0, 0)


def _flash_values(st, vt_fn, kt, c, b):
    cs = slice(c * COL_CHUNK, (c + 1) * COL_CHUNK)
    lhs = jnp.concatenate([vt_fn(kt), jnp.ones((SUM_ROWS, ATT_TILE), BF16)], axis=0)
    pv = jnp.dot(lhs, st.p[b][...], preferred_element_type=F32)
    alpha = st.a[b][...]
    st.acc[:, cs] = alpha * st.acc[:, cs] + pv[:LANES]
    st.l[:, cs] = alpha * st.l[:, cs] + pv[LANES:LANES + 1]


def _flash_region(st, score_fn, vt_fn, tiles, has_next):
    units = [(ti, c) for ti in range(len(tiles)) for c in range(st.n_chunks)]
    assert len(units) % 2 == 0
    for u, (ti, c) in enumerate(units):
        b = u % 2
        kt, bias_fn = tiles[ti]
        if u + 1 < len(units):
            nti, nc = units[u + 1]
            st.s[1 - b][...] = score_fn(tiles[nti][0], nc)
        elif has_next:
            st.s[1 - b][...] = score_fn(tiles[-1][0] + 1, 0)
        if u > 0:
            pti, pc = units[u - 1]
            _flash_values(st, vt_fn, tiles[pti][0], pc, 1 - b)
        else:
            _flash_values(st, vt_fn, jnp.maximum(kt - 1, 0), st.n_chunks - 1, 1 - b)
        cs = slice(c * COL_CHUNK, (c + 1) * COL_CHUNK)
        c0 = (c * COL_CHUNK) % ATT_TILE

        def rows(r, s_ref=st.s[b], bias_fn=bias_fn, c0=c0):
            blk = s_ref[r * ROW_CHUNK:(r + 1) * ROW_CHUNK, :]
            if bias_fn is not None:
                blk = blk + bias_fn(c0, r)
            return blk

        n_rows = ATT_TILE // ROW_CHUNK
        part = rows(0)
        for r in range(1, n_rows):
            part = jnp.maximum(part, rows(r))
        m_prev = st.m[:, cs]
        m_new = jnp.maximum(m_prev, jnp.max(part, axis=0, keepdims=True))
        st.a[b][...] = jnp.exp2(m_prev - m_new)
        st.m[:, cs] = m_new
        for r in range(n_rows):
            st.p[b][r * ROW_CHUNK:(r + 1) * ROW_CHUNK, :] = jnp.exp2(rows(r) - m_new).astype(BF16)


def _flash_sweep(st, score_fn, vt_fn, n_plain, tail_regions, last_tile):
    _flash_start(st, score_fn)

    def pair(j, carry):
        _flash_region(st, score_fn, vt_fn, [(2 * j, None), (2 * j + 1, None)], True)
        return carry

    lax.fori_loop(0, lax.shift_right_logical(n_plain, 1), pair, 0)

    @pl.when((n_plain & 1) == 1)
    def _():
        _flash_region(st, score_fn, vt_fn, [(n_plain - 1, None)], True)

    tail_regions()
    _flash_values(st, vt_fn, last_tile, st.n_chunks - 1, 1)


def _chunk_causal_bias_t(t):
    r = np.arange(t)
    ok = (r[:, None] // CHUNK) <= (r[None, :] // CHUNK)
    return jnp.asarray(np.where(ok, 0.0, NEG_INF), F32)


def _attn_b_kernel(qt_ref, k_ref, vt_ref, diag_ref, o_ref, *scratch):
    i = pl.program_id(1)
    t = ATT_TILE
    st = _FlashState(*scratch)
    per_head = t // COL_CHUNK

    def score(kt, c):
        hh = c // per_head
        c0 = (c % per_head) * COL_CHUNK
        k = k_ref[pl.ds(pl.multiple_of(kt * t, t), t), hh * LANES:(hh + 1) * LANES]
        return jnp.dot(k, qt_ref[0, hh * LANES:(hh + 1) * LANES, c0:c0 + COL_CHUNK],
                       preferred_element_type=F32)

    vt_fn = lambda kt: vt_ref[kt]

    def tail():
        diag = lambda c0, r: diag_ref[r * ROW_CHUNK:(r + 1) * ROW_CHUNK, c0:c0 + COL_CHUNK]
        _flash_region(st, score, vt_fn, [(i, diag)], False)

    _flash_sweep(st, score, vt_fn, i, tail, i)
    o2 = st.acc[...] / st.l[...]
    ot = jnp.concatenate([o2[:B_V_DIM, :t], o2[B_V_DIM:, t:]], axis=0)
    o_ref[...] = ot.T.astype(BF16)


def _attn_b(qbt, kb, vbt):
    nb, _, t = qbt.shape
    s = nb * t
    return pl.pallas_call(
        _attn_b_kernel,
        grid=(B_HEADS // 2, nb),
        in_specs=[pl.BlockSpec((1, 2 * LANES, t), lambda p, i: (i, p, 0)),
                  pl.BlockSpec((s, 2 * LANES), lambda p, i: (0, p)),
                  pl.BlockSpec((nb, LANES, t), lambda p, i: (0, p, 0)),
                  _full((t, t))],
        out_specs=pl.BlockSpec((t, LANES), lambda p, i: (i, p)),
        out_shape=jax.ShapeDtypeStruct((s, B_HEADS * B_V_DIM), BF16),
        scratch_shapes=_flash_scratch(2 * t),
        compiler_params=_params("arbitrary", "arbitrary"),
        name="attn_b",
    )(qbt, kb, vbt, _chunk_causal_bias_t(t))


def _attn_c_kernel(lam_ref, qt_ref, k_ref, vt_ref, d0_ref, d1_ref, g_ref, o_ref, q2_ref, *scratch, out_scale):
    i = pl.program_id(1)
    t = ATT_TILE
    st = _FlashState(*scratch)
    d = C_HEAD_DIM
    zero = jnp.zeros((d, t), BF16)
    q2_ref[:d, :t] = qt_ref[0, :d, :]
    q2_ref[d:, :t] = zero
    q2_ref[:d, t:] = zero
    q2_ref[d:, t:] = qt_ref[0, d:, :]

    def score(kt, c):
        return jnp.dot(k_ref[pl.ds(pl.multiple_of(kt * t, t), t), :],
                       q2_ref[:, c * COL_CHUNK:(c + 1) * COL_CHUNK], preferred_element_type=F32)

    vt_fn = lambda kt: vt_ref[kt]

    def tail():
        tile_of = lambda ref: lambda c0, r: ref[0, r * ROW_CHUNK:(r + 1) * ROW_CHUNK, c0:c0 + COL_CHUNK]
        diag = (i, tile_of(d0_ref))

        @pl.when(i > 0)
        def _():
            below = (i - 1, tile_of(d1_ref))
            _flash_region(st, score, vt_fn, [below, diag], False)

        @pl.when(i == 0)
        def _():
            _flash_region(st, score, vt_fn, [diag], False)

    _flash_sweep(st, score, vt_fn, jnp.maximum(i - 1, 0), tail, i)

    o2 = st.acc[...] / st.l[...]
    ot = o2[:, :t] - lam_ref[0] * o2[:, t:]
    inv = lax.rsqrt(jnp.mean(ot * ot, axis=0, keepdims=True) + RMS_EPS)
    o_ref[...] = (ot * inv * g_ref[...] * out_scale).T.astype(BF16)


def _t5_bucket(rel):
    half = T5_BUCKETS // 2
    max_exact = half // 2
    ret = jnp.where(rel > 0, half, 0)
    n = jnp.abs(rel)
    large = max_exact + (jnp.log(jnp.maximum(n, max_exact).astype(F32) / max_exact)
                         / math.log(T5_MAX_DIST / max_exact) * (half - max_exact)).astype(I32)
    large = jnp.minimum(large, half - 1)
    return ret + jnp.where(n < max_exact, n, large)


def _t5_tiles_t(t5_table, t):
    assert t >= T5_MAX_DIST
    far_bucket = T5_BUCKETS // 2 - 1
    tab = (t5_table.astype(F32) - t5_table[far_bucket].astype(F32)[None, :]).T
    by_rel = lambda rel: tab[:, _t5_bucket(rel)]
    d_pos = jnp.arange(t, dtype=I32)
    d_neg = jnp.arange(-t, 0, dtype=I32)
    d0 = _toeplitz(by_rel(-d_pos), by_rel(-d_neg), t, t)
    d1 = _toeplitz(by_rel(-d_pos - t), by_rel(-d_neg - t), t, t)
    r = np.arange(t)
    ok = (r[:, None] // CHUNK) <= (r[None, :] // CHUNK)
    d0 = jnp.where(jnp.asarray(ok)[None], d0 * LOG2E, NEG_INF)
    return d0, d1 * LOG2E


def _attn_c(qt, k, vt, lam, subln_g, t5_table, lam_init):
    nb, _, t = qt.shape
    s = nb * t
    d0, d1 = _t5_tiles_t(t5_table, t)
    return pl.pallas_call(
        functools.partial(_attn_c_kernel, out_scale=1.0 - lam_init),
        grid=(C_HEADS, nb),
        in_specs=[pl.BlockSpec(memory_space=pltpu.SMEM),
                  pl.BlockSpec((1, LANES, t), lambda h, i: (i, h, 0)),
                  pl.BlockSpec((s, LANES), lambda h, i: (0, h)),
                  pl.BlockSpec((nb, LANES, t), lambda h, i: (0, h, 0)),
                  pl.BlockSpec((1, t, t), lambda h, i: (h, 0, 0)),
                  pl.BlockSpec((1, t, t), lambda h, i: (h, 0, 0)),
                  _full((LANES, 1))],
        out_specs=pl.BlockSpec((t, LANES), lambda h, i: (i, h)),
        out_shape=jax.ShapeDtypeStruct((s, C_HEADS * 2 * C_HEAD_DIM), BF16),
        scratch_shapes=[pltpu.VMEM((LANES, 2 * t), BF16)] + _flash_scratch(2 * t),
        compiler_params=_params("arbitrary", "arbitrary"),
        name="attn_c",
    )(lam.reshape(1), qt, k, vt, d0, d1, subln_g.reshape(-1, 1))


def _prep_c_kernel(x_ref, wq_ref, wk_ref, wv_ref, qt_ref, k_ref, vt_ref):
    xb = x_ref[...].astype(BF16)
    qt = lax.dot_general(wq_ref[...], xb, NT_DIMS, preferred_element_type=F32)
    qt_ref[0] = (qt * (C_HEAD_DIM ** -0.5 * LOG2E)).astype(BF16)
    k_ref[...] = jnp.dot(xb, wk_ref[...], preferred_element_type=F32).astype(BF16)
    vt_ref[0] = lax.dot_general(wv_ref[...], xb, NT_DIMS, preferred_element_type=F32).astype(BF16)


def _prep_c(x, w_in):
    s = x.shape[0]
    t = ATT_TILE
    wq = w_in[:, :C_QK].T.astype(BF16)
    wk = w_in[:, C_QK:2 * C_QK].astype(BF16)
    wv = w_in[:, 2 * C_QK:].T.astype(BF16)
    row = pl.BlockSpec((t, D_MODEL), lambda i: (i, 0))
    tr = pl.BlockSpec((1, C_QK, t), lambda i: (i, 0, 0))
    sds = jax.ShapeDtypeStruct
    return pl.pallas_call(
        _prep_c_kernel,
        grid=(s // t,),
        in_specs=[row, _full(wq.shape), _full(wk.shape), _full(wv.shape)],
        out_specs=[tr, row, tr],
        out_shape=[sds((s // t, C_QK, t), BF16), sds((s, C_QK), BF16), sds((s // t, C_QK, t), BF16)],
        compiler_params=_params("arbitrary"),
        name="prep_c",
    )(x, wq, wk, wv)


def _post_attn_kernel(*refs, n_in):
    x_ref = refs[0]
    o_refs = refs[1:1 + n_in]
    w_refs = refs[1 + n_in:1 + 2 * n_in]
    g_ref, b_ref, out_ref = refs[1 + 2 * n_in:]
    y = DEEPNORM_ALPHA * x_ref[...]
    for o_ref, w_ref in zip(o_refs, w_refs):
        y = y + jnp.dot(o_ref[...], w_ref[...], preferred_element_type=F32)
    _store_row_tiles(out_ref, _layer_norm(y, g_ref[...], b_ref[...]))


def _post_attn(x, outs, weights, g, b):
    s = x.shape[0]
    t = ATT_TILE
    row = lambda w: pl.BlockSpec((t, w), lambda i: (i, 0))
    ws = [w.astype(BF16) for w in weights]
    return pl.pallas_call(
        functools.partial(_post_attn_kernel, n_in=len(outs)),
        grid=(s // t,),
        in_specs=[row(D_MODEL)] + [row(o.shape[1]) for o in outs] + [_full(w.shape) for w in ws]
        + [_full((1, D_MODEL))] * 2,
        out_specs=pl.BlockSpec((t, ROW_SUBLANES, LANES), lambda i: (i, 0, 0)),
        out_shape=jax.ShapeDtypeStruct((s, ROW_SUBLANES, LANES), F32),
        compiler_params=_params("arbitrary"),
        name="post_attn",
    )(x, *outs, *ws, g.reshape(1, -1), b.reshape(1, -1))


def _router_kernel(x_ref, wr_ref, br_ref, tri_ref, idx_ref, rank_ref, w_ref, cnt_ref, carry):
    step = pl.program_id(0)

    @pl.when(step == 0)
    def _():
        carry[...] = jnp.zeros(carry.shape, F32)

    t = ROUTER_TILE
    logits = lax.dot_general(wr_ref[...], _load_row_tiles(x_ref), NT_DIMS, preferred_element_type=F32,
                             precision=lax.Precision.HIGHEST) + br_ref[...]
    eid = lax.broadcasted_iota(I32, (N_EXPERTS, t), 0).astype(F32)
    work = logits
    vals, sels = [], []
    for _ in range(TOP_K):
        m = jnp.max(work, axis=0, keepdims=True)
        sel = jnp.min(jnp.where(work == m, eid, float(N_EXPERTS)), axis=0, keepdims=True)
        vals.append(m)
        sels.append(sel)
        work = jnp.where(eid == sel, -jnp.inf, work)
    exps = [jnp.exp(v - vals[0]) for v in vals]
    denom = exps[0] + exps[1] + exps[2] + exps[3]

    chosen = jnp.zeros((N_EXPERTS, t), F32)
    for sel in sels:
        chosen = chosen + (eid == sel).astype(F32)
    before = jnp.dot(chosen.astype(BF16), tri_ref[...], preferred_element_type=F32) + carry[...]
    for kk, sel in enumerate(sels):
        rank = jnp.sum(jnp.where(eid == sel, before, 0.0), axis=0, keepdims=True)
        rank_ref[kk:kk + 1, :] = rank.astype(I32)
        idx_ref[kk:kk + 1, :] = sel.astype(I32)
    carry[...] = carry[...] + jnp.sum(chosen, axis=1, keepdims=True)
    cnt_ref[...] = jnp.broadcast_to(carry[...], cnt_ref.shape)

    sub = lax.broadcasted_iota(I32, (LANES, t), 0)
    wrows = jnp.zeros((LANES, t), F32)
    for kk in range(TOP_K):
        wrows = jnp.where(sub == kk, exps[kk] / denom, wrows)
    w_ref[...] = wrows.T


def _router(x1, router_w, router_b):
    s = x1.shape[0]
    t = ROUTER_TILE
    tri = jnp.asarray(np.triu(np.ones((t, t), np.float32), k=1), BF16)
    tok = pl.BlockSpec((TOP_K, t), lambda i: (0, i))
    return pl.pallas_call(
        _router_kernel,
        grid=(s // t,),
        in_specs=[pl.BlockSpec((t, ROW_SUBLANES, LANES), lambda i: (i, 0, 0)),
                  _full((N_EXPERTS, D_MODEL)), _full((N_EXPERTS, 1)), _full((t, t))],
        out_specs=[tok, tok, pl.BlockSpec((t, LANES), lambda i: (i, 0)), _full((N_EXPERTS, LANES))],
        out_shape=[jax.ShapeDtypeStruct((TOP_K, s), I32), jax.ShapeDtypeStruct((TOP_K, s), I32),
                   jax.ShapeDtypeStruct((s, LANES), F32), jax.ShapeDtypeStruct((N_EXPERTS, LANES), F32)],
        scratch_shapes=[pltpu.VMEM((N_EXPERTS, 1), F32)],
        compiler_params=_params("arbitrary"),
        name="router",
    )(x1, router_w.T, router_b.reshape(-1, 1), tri)


def _gmm_kernel(te_ref, nt_ref, src_hbm, x_hbm, wi_ref, bi_ref, wo_ref, bo_ref, y_ref,
                wi_bf, wo_bf, xbuf, src_smem, src_sem, row_sem):
    i = pl.program_id(0)
    n_tiles = nt_ref[0]
    tm = EXPERT_TILE
    group = SRC_GROUP * tm

    def half_base(tile):
        return (lax.shift_right_logical(tile, SRC_GROUP.bit_length() - 1) & 1) * group

    def fetch_indices(tile):
        dst = pl.multiple_of(half_base(tile), group)
        cp = pltpu.make_async_copy(src_hbm.at[pl.ds(pl.multiple_of(tile * tm, group), group)],
                                   src_smem.at[pl.ds(dst, group)], src_sem)
        cp.start()
        cp.wait()

    def issue_tile(tile):
        slot = tile & 1
        base = half_base(tile) + (tile & (SRC_GROUP - 1)) * tm

        def body(g, carry):
            for j in range(ISSUE_UNROLL):
                r = g * ISSUE_UNROLL + j
                tok = src_smem[base + r]
                pltpu.make_async_copy(x_hbm.at[tok], xbuf.at[slot, r], row_sem.at[slot]).start(priority=j % 2)
            return carry

        lax.fori_loop(0, tm // ISSUE_UNROLL, body, 0)

    def wait_tile(tile):
        slot = tile & 1
        pltpu.make_async_copy(x_hbm.at[pl.ds(0, tm)], xbuf.at[slot], row_sem.at[slot]).wait()

    @pl.when(i == 0)
    def _():
        fetch_indices(i)
        issue_tile(i)

    nxt = i + 1

    @pl.when(jnp.logical_and(nxt <= n_tiles, (nxt & (SRC_GROUP - 1)) == 0))
    def _():
        fetch_indices(nxt)

    def issue_inline(tile):
        slot = tile & 1
        base = half_base(tile) + (tile & (SRC_GROUP - 1)) * tm
        for r in range(tm):
            pltpu.make_async_copy(x_hbm.at[src_smem[base + r]], xbuf.at[slot, r],
                                  row_sem.at[slot]).start(priority=r % 2)

    prev = te_ref[jnp.maximum(i - 1, 0)]
    fresh = jnp.logical_or(i == 0, te_ref[i] != prev)

    @pl.when(jnp.logical_and(i < n_tiles, fresh))
    def _():
        def cast(c, carry):
            rows = pl.ds(pl.multiple_of(c * CAST_ROWS, CAST_ROWS), CAST_ROWS)
            wi_bf[rows, :] = wi_ref[0, rows, :].astype(BF16)
            wo_bf[rows, :] = wo_ref[0, rows, :].astype(BF16)
            return carry

        lax.fori_loop(0, D_MODEL // CAST_ROWS, cast, 0)

    @pl.when(i < n_tiles)
    def _():
        wait_tile(i)
        xb = _load_row_tiles(xbuf, i & 1).astype(BF16)
        issue_inline(nxt)
        h = jnp.dot(xb, wi_bf[...], preferred_element_type=F32) + bi_ref[0]
        glu = jnp.minimum(h[:, :D_FF], SWIGLU_LIMIT)
        lin = jnp.clip(h[:, D_FF:], -SWIGLU_LIMIT, SWIGLU_LIMIT)
        act = glu * jax.nn.sigmoid(SWIGLU_ALPHA * glu) * (lin + 1.0)
        _store_row_tiles(y_ref, jnp.dot(act.astype(BF16), wo_bf[...], preferred_element_type=F32) + bo_ref[0])

    @pl.when(i == n_tiles)
    def _():
        wait_tile(i)

    @pl.when(i >= n_tiles)
    def _():
        y_ref[...] = jnp.zeros(y_ref.shape, F32)


def _gmm(layer, tile_expert, n_tiles, src, x1, w_in, b_in, w_out, b_out):
    n_rows = src.shape[0]
    tm = EXPERT_TILE
    assert n_rows % (SRC_GROUP * tm) == 0
    last = lambda i, nt: jnp.maximum(jnp.minimum(i, nt[0] - 1), 0)
    expert = lambda i, te, nt: layer * N_EXPERTS + te[last(i, nt)]
    any_spec = pl.BlockSpec(memory_space=pl.ANY)
    grid_spec = pltpu.PrefetchScalarGridSpec(
        num_scalar_prefetch=2,
        grid=(n_rows // tm,),
        in_specs=[any_spec, any_spec,
                  pl.BlockSpec((1, D_MODEL, 2 * D_FF), lambda i, te, nt: (expert(i, te, nt), 0, 0)),
                  pl.BlockSpec((1, 1, 2 * D_FF), lambda i, te, nt: (expert(i, te, nt), 0, 0)),
                  pl.BlockSpec((1, D_FF, D_MODEL), lambda i, te, nt: (expert(i, te, nt), 0, 0)),
                  pl.BlockSpec((1, 1, D_MODEL), lambda i, te, nt: (expert(i, te, nt), 0, 0))],
        out_specs=pl.BlockSpec((tm, ROW_SUBLANES, LANES), lambda i, te, nt: (i, 0, 0)),
        scratch_shapes=[pltpu.VMEM((D_MODEL, 2 * D_FF), BF16), pltpu.VMEM((D_FF, D_MODEL), BF16),
                        pltpu.VMEM((2, tm, ROW_SUBLANES, LANES), F32), pltpu.SMEM((2 * SRC_GROUP * tm,), I32),
                        pltpu.SemaphoreType.DMA, pltpu.SemaphoreType.DMA((2,))],
    )
    return pl.pallas_call(
        _gmm_kernel,
        grid_spec=grid_spec,
        out_shape=jax.ShapeDtypeStruct((n_rows, ROW_SUBLANES, LANES), F32),
        compiler_params=_params("arbitrary"),
        name="expert_mlp",
    )(tile_expert, n_tiles, src, x1, w_in, b_in, w_out, b_out)


def _combine_kernel(pos_hbm, y_hbm, x_ref, w_ref, p_ref, wg_ref, wp_ref, g_ref, b_ref, out_ref,
                    pos_smem, ybuf, pos_sem, row_sem):
    i = pl.program_id(0)
    t = COMBINE_TILE
    n = t * TOP_K

    def fetch_indices(step):
        base = pl.multiple_of((step & 1) * n, n)
        cp = pltpu.make_async_copy(pos_hbm.at[pl.ds(pl.multiple_of(step * n, n), n)],
                                   pos_smem.at[pl.ds(base, n)], pos_sem)
        cp.start()
        cp.wait()

    def row_copy(step, tok, kk):
        slot = step & 1
        return pltpu.make_async_copy(y_hbm.at[pos_smem[slot * n + tok * TOP_K + kk]], ybuf.at[slot, kk, tok],
                                     row_sem.at[slot])

    def wait_step(step):
        slot = step & 1
        for kk in range(TOP_K):
            pltpu.make_async_copy(y_hbm.at[pl.ds(0, t)], ybuf.at[slot, kk], row_sem.at[slot]).wait()

    @pl.when(i == 0)
    def _():
        fetch_indices(i)

        def issue(tok, carry):
            for kk in range(TOP_K):
                row_copy(i, tok, kk).start(priority=kk % 2)
            return carry

        lax.fori_loop(0, t, issue, 0, unroll=2)

    nxt = i + 1
    fetch_indices(nxt)
    x = _load_row_tiles(x_ref)
    pb = p_ref[...].astype(BF16)
    w = w_ref[...]
    for tok in range(t):
        for kk in range(TOP_K):
            row_copy(nxt, tok, kk).start(priority=kk % 2)
    gate = jax.nn.sigmoid(jnp.dot(x.astype(BF16), wg_ref[...], preferred_element_type=F32))
    proj = jnp.dot(pb, wp_ref[...], preferred_element_type=F32)
    y = DEEPNORM_ALPHA * x + gate * proj

    wait_step(i)
    for kk in range(TOP_K):
        y = y + w[:, kk:kk + 1] * _load_row_tiles(ybuf, i & 1, kk)
    out_ref[...] = _layer_norm(y, g_ref[...], b_ref[...])

    @pl.when(nxt == pl.num_programs(0))
    def _():
        wait_step(nxt)


def _combine(pos_flat, y, x1, wcol, p, gate_w, proj_w, g, b):
    s = x1.shape[0]
    t = COMBINE_TILE
    any_spec = pl.BlockSpec(memory_space=pl.ANY)
    row = lambda w: pl.BlockSpec((t, w), lambda i: (i, 0))
    return pl.pallas_call(
        _combine_kernel,
        grid=(s // t,),
        in_specs=[any_spec, any_spec, pl.BlockSpec((t, ROW_SUBLANES, LANES), lambda i: (i, 0, 0)),
                  row(LANES), row(PLE_DIM),
                  _full((D_MODEL, D_MODEL)), _full((PLE_DIM, D_MODEL)),
                  _full((1, D_MODEL)), _full((1, D_MODEL))],
        out_specs=row(D_MODEL),
        out_shape=jax.ShapeDtypeStruct((s, D_MODEL), F32),
        scratch_shapes=[pltpu.SMEM((2 * t * TOP_K,), I32),
                        pltpu.VMEM((2, TOP_K, t, ROW_SUBLANES, LANES), F32),
                        pltpu.SemaphoreType.DMA, pltpu.SemaphoreType.DMA((2,))],
        compiler_params=_params("arbitrary"),
        name="combine",
    )(pos_flat, y, x1, wcol, p, gate_w.astype(BF16), proj_w.astype(BF16), g.reshape(1, -1), b.reshape(1, -1))


def _moe_block(layer, x1, p, router_w, router_b, w_in, b_in, w_out, b_out, gate_w, proj_w, g, b):
    s = x1.shape[0]
    tm = EXPERT_TILE
    n_tiles_max = s * TOP_K // tm + N_EXPERTS + SRC_GROUP
    idx, rank, wcol, cnt = _router(x1, router_w, router_b)
    counts = cnt[:, 0].astype(I32)
    tiles_per = (counts + tm - 1) // tm
    tile_end = jnp.cumsum(tiles_per)
    offsets = (tile_end - tiles_per) * tm
    experts = jnp.arange(N_EXPERTS, dtype=I32)
    pos = rank + jnp.sum(jnp.where(idx[..., None] == experts, offsets, 0), axis=-1)
    pos_flat = pos.T.reshape(-1)
    tile_ids = jnp.arange(n_tiles_max, dtype=I32)
    tile_expert = jnp.minimum(jnp.sum((tile_end[None, :] <= tile_ids[:, None]).astype(I32), axis=-1),
                              N_EXPERTS - 1)
    n_tiles = tile_end[-1:].astype(I32)
    src = jnp.zeros((n_tiles_max * tm,), I32).at[pos_flat].set(jnp.arange(s * TOP_K, dtype=I32) // TOP_K)
    y = _gmm(layer, tile_expert, n_tiles, src, x1, w_in, b_in, w_out, b_out)
    pos_ahead = jnp.concatenate([pos_flat, jnp.zeros((COMBINE_TILE * TOP_K,), I32)])
    return _combine(pos_ahead, y, x1, wcol, p, gate_w, proj_w, g, b)


def _lambda_init(layer_idx):
    return 0.8 - 0.6 * math.exp(-0.3 * layer_idx)


def kernel(x, p, ab_w_in, ab_rel_bias, ab_q_norm, ab_kv_norm, ab_w_uq, ab_w_ukv, ab_w_out, c_w_in, c_lambda, c_subln, c_w_out, t5_table, ln_mix_g, ln_mix_b, ln_ffn_g, ln_ffn_b, router_w, router_b, exp_w_in, exp_b_in, exp_w_out, exp_b_out, ple_gate_w, ple_proj_w):
    batch, s, _ = x.shape
    assert batch == 1 and s % (SRC_GROUP * EXPERT_TILE) == 0
    xr = x[0]
    w_in_all = exp_w_in.reshape(DEPTH * N_EXPERTS, D_MODEL, 2 * D_FF)
    b_in_all = exp_b_in.reshape(DEPTH * N_EXPERTS, 1, 2 * D_FF)
    w_out_all = exp_w_out.reshape(DEPTH * N_EXPERTS, D_FF, D_MODEL)
    b_out_all = exp_b_out.reshape(DEPTH * N_EXPERTS, 1, D_MODEL)
    for i in range(DEPTH):
        j = i // 2
        if i % 2 == 0:
            qa, ka, va, qbt, kb, vbt = _prep_ab(xr, ab_w_in[j], ab_q_norm[j], ab_kv_norm[j], ab_w_uq[j],
                                                ab_w_ukv[j])
            o_a = _attn_a(qa, ka, va, ab_rel_bias[j])
            o_b = _attn_b(qbt, kb, vbt)
            x1 = _post_attn(xr, [o_a, o_b], [ab_w_out[j][:A_WIDTH], ab_w_out[j][A_WIDTH:]],
                            ln_mix_g[i], ln_mix_b[i])
        else:
            qt, k, vt = _prep_c(xr, c_w_in[j])
            lp = c_lambda[j].astype(F32)
            lam_init = _lambda_init(i)
            lam = jnp.exp(jnp.sum(lp[0] * lp[1])) - jnp.exp(jnp.sum(lp[2] * lp[3])) + lam_init
            o_c = _attn_c(qt, k, vt, lam, c_subln[j], t5_table, lam_init)
            x1 = _post_attn(xr, [o_c], [c_w_out[j]], ln_mix_g[i], ln_mix_b[i])
        xr = _moe_block(i, x1, p[i, 0], router_w[i], router_b[i], w_in_all, b_in_all, w_out_all, b_out_all,
                        ple_gate_w[i], ple_proj_w[i], ln_ffn_g[i], ln_ffn_b[i])
    return xr[None]
```

```python
import functools
import math

import jax
import jax.numpy as jnp
import numpy as np
from jax import lax
from jax.experimental import pallas as pl
from jax.experimental.pallas import tpu as pltpu

F32 = jnp.float32
BF16 = jnp.bfloat16
I32 = jnp.int32

D_MODEL = 1024
DEPTH = 4
CHUNK = 64
PLE_DIM = 256

A_HEADS = 8
A_HEAD_DIM = 64
A_LEFT_CHUNKS = 8
A_REL_CLIP = 128
A_WIDTH = A_HEADS * A_HEAD_DIM

B_HEADS = 8
B_NOPE_DIM = 64
B_ROPE_DIM = 32
B_V_DIM = 64
B_Q_RANK = 512
B_KV_RANK = 256
ROPE_BASE = 10000.0

C_HEADS = 8
C_HEAD_DIM = 64
C_QK = 2 * C_HEADS * C_HEAD_DIM

T5_BUCKETS = 32
T5_MAX_DIST = 128

N_EXPERTS = 32
TOP_K = 4
D_FF = 1024
SWIGLU_ALPHA = 1.702
SWIGLU_LIMIT = 7.0

DEEPNORM_ALPHA = (2.0 * DEPTH) ** 0.25
LN_EPS = 1e-5
RMS_EPS = 1e-6
NEG_INF = -1e30

LANES = 128
ROW_SUBLANES = D_MODEL // LANES
VMEM_LIMIT = 56 * 1024 * 1024

ATT_TILE = 512
COL_CHUNK = 256
ROW_CHUNK = 64
SUM_ROWS = 16
LOG2E = math.log2(math.e)
BAND_GROUP = 256
BAND_WINDOW = BAND_GROUP + A_LEFT_CHUNKS * CHUNK
ROUTER_TILE = 512
EXPERT_TILE = 256
DISPATCH_TILE = 512
COMBINE_TILE = 256
CAST_ROWS = 128
ISSUE_UNROLL = 8

NT_DIMS = (((1,), (1,)), ((), ()))


def _params(*sem):
    return pltpu.CompilerParams(dimension_semantics=sem, vmem_limit_bytes=VMEM_LIMIT)


def _full(shape):
    n = len(shape)
    return pl.BlockSpec(shape, lambda *_: (0,) * n)


def _layer_norm(y, g, b):
    mu = jnp.mean(y, axis=-1, keepdims=True)
    d = y - mu
    var = jnp.mean(d * d, axis=-1, keepdims=True)
    return d * lax.rsqrt(var + LN_EPS) * g + b


def _rms_norm(c, g):
    inv = lax.rsqrt(jnp.mean(c * c, axis=-1, keepdims=True) + RMS_EPS)
    return c * inv * g


def _load_row_tiles(ref, *lead):
    tiles = pltpu.einshape("ral->arl", ref[lead] if lead else ref[...])
    return jnp.concatenate([tiles[a] for a in range(ROW_SUBLANES)], axis=1)


def _store_row_tiles(ref, value):
    tiles = jnp.stack([value[:, a * LANES:(a + 1) * LANES] for a in range(ROW_SUBLANES)], axis=0)
    ref[...] = pltpu.einshape("arl->ral", tiles)


def _toeplitz(vals_pos, vals_neg, nr, nc):
    w = jnp.concatenate([vals_pos, vals_neg], axis=-1)
    lw = nc + nr
    assert w.shape[-1] == lw
    flat = jnp.tile(w, (1,) * (w.ndim - 1) + (nr,))[..., :nr * (lw - 1)]
    return flat.reshape(w.shape[:-1] + (nr, lw - 1))[..., :nc]


def _prep_ab_kernel(x_ref, wa_ref, wcq_ref, wckv_ref, wkr_ref, wkrr_ref, qn_ref, kvn_ref,
                    wuqa_ref, wuqb_ref, wukn_ref, wuv_ref, place_ref, ctab_ref, stab_ref,
                    cos_ref, sin_ref,
                    qa_ref, ka_ref, va_ref, qbt_ref, kb_ref, vbt_ref):
    xb = x_ref[...].astype(BF16)
    ha = jnp.dot(xb, wa_ref[...], preferred_element_type=F32)
    qa_ref[...] = (ha[:, :A_WIDTH] * (A_HEAD_DIM ** -0.5)).astype(BF16)
    ka_ref[...] = ha[:, A_WIDTH:2 * A_WIDTH].astype(BF16)
    va_ref[...] = ha[:, 2 * A_WIDTH:].astype(BF16)

    cq = jnp.dot(xb, wcq_ref[...], preferred_element_type=F32)
    cqn = _rms_norm(cq, qn_ref[...]).astype(BF16)
    qa_part = lax.dot_general(wuqa_ref[...], cqn, NT_DIMS, preferred_element_type=F32)
    qb_part = lax.dot_general(wuqb_ref[...], cqn, NT_DIMS, preferred_element_type=F32)
    ct = ctab_ref[...]
    st = stab_ref[...]
    scale = (B_NOPE_DIM + B_ROPE_DIM) ** -0.5 * LOG2E
    for h in range(B_HEADS):
        sl = slice(h * LANES, (h + 1) * LANES)
        qbt_ref[0, sl, :] = ((qa_part[sl, :] * ct + qb_part[sl, :] * st) * scale).astype(BF16)

    ckv = jnp.dot(xb, wckv_ref[...], preferred_element_type=F32)
    ckvn = _rms_norm(ckv, kvn_ref[...]).astype(BF16)
    kr = jnp.dot(xb, wkr_ref[...], preferred_element_type=F32)
    krr = jnp.dot(xb, wkrr_ref[...], preferred_element_type=F32)
    k_rope = (kr * cos_ref[...] + krr * sin_ref[...]).astype(BF16)
    kb = jnp.dot(ckvn, wukn_ref[...], preferred_element_type=F32)
    kb = kb + jnp.dot(k_rope, place_ref[...], preferred_element_type=F32)
    kb_ref[...] = kb.astype(BF16)
    vbt_ref[0] = lax.dot_general(wuv_ref[...], ckvn, NT_DIMS, preferred_element_type=F32).astype(BF16)


def _rot_half_cols(w):
    half = w.shape[-1] // 2
    return jnp.concatenate([-w[..., half:], w[..., :half]], axis=-1)


def _prep_ab(x, w_in, q_norm, kv_norm, w_uq, w_ukv):
    s = x.shape[0]
    t = ATT_TILE
    splits = [3 * A_WIDTH, 3 * A_WIDTH + B_Q_RANK, 3 * A_WIDTH + B_Q_RANK + B_KV_RANK]
    wa = w_in[:, :splits[0]].astype(BF16)
    wcq = w_in[:, splits[0]:splits[1]].astype(BF16)
    wckv = w_in[:, splits[1]:splits[2]].astype(BF16)
    wkr_f = w_in[:, splits[2]:]
    wkr = wkr_f.astype(BF16)
    wkrr = _rot_half_cols(wkr_f).astype(BF16)

    qd = B_NOPE_DIM + B_ROPE_DIM
    w_uq3 = w_uq.reshape(B_Q_RANK, B_HEADS, qd)
    zpad = jnp.zeros((B_Q_RANK, B_HEADS, LANES - qd), F32)
    wuqa = jnp.concatenate([w_uq3, zpad], axis=-1).reshape(B_Q_RANK, B_HEADS * LANES).T.astype(BF16)
    wuqb = jnp.concatenate([jnp.zeros((B_Q_RANK, B_HEADS, B_NOPE_DIM), F32),
                            _rot_half_cols(w_uq3[..., B_NOPE_DIM:]), zpad],
                           axis=-1).reshape(B_Q_RANK, B_HEADS * LANES).T.astype(BF16)
    w_ukv3 = w_ukv.reshape(B_KV_RANK, B_HEADS, B_NOPE_DIM + B_V_DIM)
    wukn = jnp.concatenate([w_ukv3[..., :B_NOPE_DIM],
                            jnp.zeros((B_KV_RANK, B_HEADS, LANES - B_NOPE_DIM), F32)],
                           axis=-1).reshape(B_KV_RANK, B_HEADS * LANES).astype(BF16)
    wuv = w_ukv3[..., B_NOPE_DIM:].reshape(B_KV_RANK, B_HEADS * B_V_DIM).T.astype(BF16)
    place = np.zeros((B_ROPE_DIM, B_HEADS * LANES), np.float32)
    for h in range(B_HEADS):
        for r in range(B_ROPE_DIM):
            place[r, h * LANES + B_NOPE_DIM + r] = 1.0
    place = jnp.asarray(place, BF16)

    pos = jnp.arange(s, dtype=F32)
    inv_freq = ROPE_BASE ** (-jnp.arange(0, B_ROPE_DIM, 2, dtype=F32) / B_ROPE_DIM)
    ang = pos[:, None] * inv_freq[None, :]
    cos = jnp.concatenate([jnp.cos(ang)] * 2, axis=-1)
    sin = jnp.concatenate([jnp.sin(ang)] * 2, axis=-1)
    ctab = jnp.concatenate([jnp.ones((s, B_NOPE_DIM), F32), cos, jnp.zeros((s, LANES - qd), F32)], axis=-1).T
    stab = jnp.concatenate([jnp.zeros((s, B_NOPE_DIM), F32), sin, jnp.zeros((s, LANES - qd), F32)], axis=-1).T

    row = lambda w: pl.BlockSpec((t, w), lambda i: (i, 0))
    col = pl.BlockSpec((LANES, t), lambda i: (0, i))
    tr = lambda w: pl.BlockSpec((1, w, t), lambda i: (i, 0, 0))
    consts = [wa, wcq, wckv, wkr, wkrr, q_norm.reshape(1, -1), kv_norm.reshape(1, -1),
              wuqa, wuqb, wukn, wuv, place]
    sds = jax.ShapeDtypeStruct
    return pl.pallas_call(
        _prep_ab_kernel,
        grid=(s // t,),
        in_specs=[row(D_MODEL)] + [_full(c.shape) for c in consts]
        + [col, col, row(B_ROPE_DIM), row(B_ROPE_DIM)],
        out_specs=[row(A_WIDTH), row(A_WIDTH), row(A_WIDTH),
                   tr(B_HEADS * LANES), row(B_HEADS * LANES), tr(B_HEADS * B_V_DIM)],
        out_shape=[sds((s, A_WIDTH), BF16)] * 3
        + [sds((s // t, B_HEADS * LANES, t), BF16), sds((s, B_HEADS * LANES), BF16),
           sds((s // t, B_HEADS * B_V_DIM, t), BF16)],
        compiler_params=_params("arbitrary"),
        name="prep_ab",
    )(x, *consts, ctab, stab, cos, sin)


def _attn_a_kernel(q_ref, kp_ref, kc_ref, vp_ref, vc_ref, bias_ref, o_ref, kbuf, vbuf):
    j = pl.program_id(1)
    t = ATT_TILE
    kbuf[0:t, :] = kp_ref[...]
    kbuf[t:2 * t, :] = kc_ref[...]
    vbuf[0:t, :] = vp_ref[...]
    vbuf[t:2 * t, :] = vc_ref[...]
    lo = lax.broadcasted_iota(I32, (1, LANES), 1) < A_HEAD_DIM
    col = lax.broadcasted_iota(I32, (1, BAND_WINDOW), 1)
    for g in range(t // BAND_GROUP):
        r0 = g * BAND_GROUP
        q = q_ref[r0:r0 + BAND_GROUP, :]
        k = kbuf[r0:r0 + BAND_WINDOW, :]
        v = vbuf[r0:r0 + BAND_WINDOW, :]
        valid = col + r0 >= jnp.where(j > 0, 0, t)
        outs = []
        for hh in range(2):
            qm = jnp.where(lo if hh == 0 else jnp.logical_not(lo), q, jnp.zeros_like(q))
            sc = lax.dot_general(qm, k, NT_DIMS, preferred_element_type=F32)
            sc = jnp.where(valid, sc + bias_ref[hh], NEG_INF)
            m = jnp.max(sc, axis=-1, keepdims=True)
            p = jnp.exp(sc - m)
            l = jnp.sum(p, axis=-1, keepdims=True)
            o = jnp.dot(p.astype(BF16), v, preferred_element_type=F32)
            outs.append(o / l)
        o_ref[r0:r0 + BAND_GROUP, :] = jnp.where(lo, outs[0], outs[1]).astype(BF16)


def _band_bias(rel_bias):
    qi = np.arange(BAND_GROUP)[:, None]
    kr = np.arange(BAND_WINDOW)[None, :]
    dchunk = (qi // CHUNK + A_LEFT_CHUNKS) - kr // CHUNK
    allowed = (dchunk >= 0) & (dchunk <= A_LEFT_CHUNKS)
    d_pos = np.arange(BAND_WINDOW)
    d_neg = np.arange(-BAND_GROUP, 0)
    to_idx = lambda d: np.clip(A_LEFT_CHUNKS * CHUNK - d, -A_REL_CLIP, A_REL_CLIP) + A_REL_CLIP
    tab = rel_bias.astype(F32).T
    table = _toeplitz(tab[:, to_idx(d_pos)], tab[:, to_idx(d_neg)], BAND_GROUP, BAND_WINDOW)
    return jnp.where(jnp.asarray(allowed)[None], table, NEG_INF)


def _attn_a(qa, ka, va, rel_bias):
    s = qa.shape[0]
    t = ATT_TILE
    bias = _band_bias(rel_bias)
    cur = pl.BlockSpec((t, LANES), lambda p, j: (j, p))
    prev = pl.BlockSpec((t, LANES), lambda p, j: (jnp.maximum(j - 1, 0), p))
    return pl.pallas_call(
        _attn_a_kernel,
        grid=(A_HEADS // 2, s // t),
        in_specs=[cur, prev, cur, prev, cur,
                  pl.BlockSpec((2, BAND_GROUP, BAND_WINDOW), lambda p, j: (p, 0, 0))],
        out_specs=cur,
        out_shape=jax.ShapeDtypeStruct((s, A_WIDTH), BF16),
        scratch_shapes=[pltpu.VMEM((2 * t, LANES), BF16), pltpu.VMEM((2 * t, LANES), BF16)],
        compiler_params=_params("arbitrary", "arbitrary"),
        name="attn_a",
    )(qa, ka, ka, va, va, bias)


class _FlashState:
    def __init__(self, m, l, acc, s0, s1, p0, p1, a0, a1):
        self.m, self.l, self.acc = m, l, acc
        self.s, self.p, self.a = (s0, s1), (p0, p1), (a0, a1)
        self.n_chunks = m.shape[1] // COL_CHUNK


def _flash_scratch(n_cols):
    t, w = ATT_TILE, COL_CHUNK
    return [pltpu.VMEM((1, n_cols), F32), pltpu.VMEM((1, n_cols), F32), pltpu.VMEM((LANES, n_cols), F32),
            pltpu.VMEM((t, w), F32), pltpu.VMEM((t, w), F32), pltpu.VMEM((t, w), BF16), pltpu.VMEM((t, w), BF16),
            pltpu.VMEM((1, w), F32), pltpu.VMEM((1, w), F32)]


def _flash_start(st, score_fn):
    st.m[...] = jnp.full(st.m.shape, NEG_INF, F32)
    st.l[...] = jnp.zeros(st.l.shape, F32)
    st.acc[...] = jnp.zeros(st.acc.shape, F32)
    st.p[1][...] = jnp.zeros(st.p[1].shape, BF16)
    st.a[1][...] = jnp.ones(st.a[1].shape, F32)
    st.s[0][...] = score_fn(0, 0)


def _flash_values(st, vt_fn, kt, c, b):
    cs = slice(c * COL_CHUNK, (c + 1) * COL_CHUNK)
    lhs = jnp.concatenate([vt_fn(kt), jnp.ones((SUM_ROWS, ATT_TILE), BF16)], axis=0)
    pv = jnp.dot(lhs, st.p[b][...], preferred_element_type=F32)
    alpha = st.a[b][...]
    st.acc[:, cs] = alpha * st.acc[:, cs] + pv[:LANES]
    st.l[:, cs] = alpha * st.l[:, cs] + pv[LANES:LANES + 1]


def _flash_region(st, score_fn, vt_fn, tiles, has_next):
    units = [(ti, c) for ti in range(len(tiles)) for c in range(st.n_chunks)]
    assert len(units) % 2 == 0
    for u, (ti, c) in enumerate(units):
        b = u % 2
        kt, bias_fn = tiles[ti]
        if u + 1 < len(units):
            nti, nc = units[u + 1]
            st.s[1 - b][...] = score_fn(tiles[nti][0], nc)
        elif has_next:
            st.s[1 - b][...] = score_fn(tiles[-1][0] + 1, 0)
        if u > 0:
            pti, pc = units[u - 1]
            _flash_values(st, vt_fn, tiles[pti][0], pc, 1 - b)
        else:
            _flash_values(st, vt_fn, jnp.maximum(kt - 1, 0), st.n_chunks - 1, 1 - b)
        cs = slice(c * COL_CHUNK, (c + 1) * COL_CHUNK)
        c0 = (c * COL_CHUNK) % ATT_TILE

        def rows(r, s_ref=st.s[b], bias_fn=bias_fn, c0=c0):
            blk = s_ref[r * ROW_CHUNK:(r + 1) * ROW_CHUNK, :]
            if bias_fn is not None:
                blk = blk + bias_fn(c0, r)
            return blk

        n_rows = ATT_TILE // ROW_CHUNK
        part = rows(0)
        for r in range(1, n_rows):
            part = jnp.maximum(part, rows(r))
        m_prev = st.m[:, cs]
        m_new = jnp.maximum(m_prev, jnp.max(part, axis=0, keepdims=True))
        st.a[b][...] = jnp.exp2(m_prev - m_new)
        st.m[:, cs] = m_new
        for r in range(n_rows):
            st.p[b][r * ROW_CHUNK:(r + 1) * ROW_CHUNK, :] = jnp.exp2(rows(r) - m_new).astype(BF16)


def _flash_sweep(st, score_fn, vt_fn, n_plain, tail_regions, last_tile):
    _flash_start(st, score_fn)

    def pair(j, carry):
        _flash_region(st, score_fn, vt_fn, [(2 * j, None), (2 * j + 1, None)], True)
        return carry

    lax.fori_loop(0, lax.shift_right_logical(n_plain, 1), pair, 0)

    @pl.when((n_plain & 1) == 1)
    def _():
        _flash_region(st, score_fn, vt_fn, [(n_plain - 1, None)], True)

    tail_regions()
    _flash_values(st, vt_fn, last_tile, st.n_chunks - 1, 1)


def _chunk_causal_bias_t(t):
    r = np.arange(t)
    ok = (r[:, None] // CHUNK) <= (r[None, :] // CHUNK)
    return jnp.asarray(np.where(ok, 0.0, NEG_INF), F32)


def _attn_b_kernel(qt_ref, k_ref, vt_ref, diag_ref, o_ref, *scratch):
    i = pl.program_id(1)
    t = ATT_TILE
    st = _FlashState(*scratch)
    per_head = t // COL_CHUNK

    def score(kt, c):
        hh = c // per_head
        c0 = (c % per_head) * COL_CHUNK
        k = k_ref[pl.ds(pl.multiple_of(kt * t, t), t), hh * LANES:(hh + 1) * LANES]
        return jnp.dot(k, qt_ref[0, hh * LANES:(hh + 1) * LANES, c0:c0 + COL_CHUNK],
                       preferred_element_type=F32)

    vt_fn = lambda kt: vt_ref[kt]

    def tail():
        diag = lambda c0, r: diag_ref[r * ROW_CHUNK:(r + 1) * ROW_CHUNK, c0:c0 + COL_CHUNK]
        _flash_region(st, score, vt_fn, [(i, diag)], False)

    _flash_sweep(st, score, vt_fn, i, tail, i)
    o2 = st.acc[...] / st.l[...]
    ot = jnp.concatenate([o2[:B_V_DIM, :t], o2[B_V_DIM:, t:]], axis=0)
    o_ref[...] = ot.T.astype(BF16)


def _attn_b(qbt, kb, vbt):
    nb, _, t = qbt.shape
    s = nb * t
    return pl.pallas_call(
        _attn_b_kernel,
        grid=(B_HEADS // 2, nb),
        in_specs=[pl.BlockSpec((1, 2 * LANES, t), lambda p, i: (i, p, 0)),
                  pl.BlockSpec((s, 2 * LANES), lambda p, i: (0, p)),
                  pl.BlockSpec((nb, LANES, t), lambda p, i: (0, p, 0)),
                  _full((t, t))],
        out_specs=pl.BlockSpec((t, LANES), lambda p, i: (i, p)),
        out_shape=jax.ShapeDtypeStruct((s, B_HEADS * B_V_DIM), BF16),
        scratch_shapes=_flash_scratch(2 * t),
        compiler_params=_params("arbitrary", "arbitrary"),
        name="attn_b",
    )(qbt, kb, vbt, _chunk_causal_bias_t(t))


def _attn_c_kernel(lam_ref, qt_ref, k_ref, vt_ref, d0_ref, d1_ref, g_ref, o_ref, q2_ref, *scratch, out_scale):
    i = pl.program_id(1)
    t = ATT_TILE
    st = _FlashState(*scratch)
    d = C_HEAD_DIM
    zero = jnp.zeros((d, t), BF16)
    q2_ref[:d, :t] = qt_ref[0, :d, :]
    q2_ref[d:, :t] = zero
    q2_ref[:d, t:] = zero
    q2_ref[d:, t:] = qt_ref[0, d:, :]

    def score(kt, c):
        return jnp.dot(k_ref[pl.ds(pl.multiple_of(kt * t, t), t), :],
                       q2_ref[:, c * COL_CHUNK:(c + 1) * COL_CHUNK], preferred_element_type=F32)

    vt_fn = lambda kt: vt_ref[kt]

    def tail():
        tile_of = lambda ref: lambda c0, r: ref[0, r * ROW_CHUNK:(r + 1) * ROW_CHUNK, c0:c0 + COL_CHUNK]
        diag = (i, tile_of(d0_ref))

        @pl.when(i > 0)
        def _():
            below = (i - 1, tile_of(d1_ref))
            _flash_region(st, score, vt_fn, [below, diag], False)

        @pl.when(i == 0)
        def _():
            _flash_region(st, score, vt_fn, [diag], False)

    _flash_sweep(st, score, vt_fn, jnp.maximum(i - 1, 0), tail, i)

    o2 = st.acc[...] / st.l[...]
    ot = o2[:, :t] - lam_ref[0] * o2[:, t:]
    inv = lax.rsqrt(jnp.mean(ot * ot, axis=0, keepdims=True) + RMS_EPS)
    o_ref[...] = (ot * inv * g_ref[...] * out_scale).T.astype(BF16)


def _t5_bucket(rel):
    half = T5_BUCKETS // 2
    max_exact = half // 2
    ret = jnp.where(rel > 0, half, 0)
    n = jnp.abs(rel)
    large = max_exact + (jnp.log(jnp.maximum(n, max_exact).astype(F32) / max_exact)
                         / math.log(T5_MAX_DIST / max_exact) * (half - max_exact)).astype(I32)
    large = jnp.minimum(large, half - 1)
    return ret + jnp.where(n < max_exact, n, large)


def _t5_tiles_t(t5_table, t):
    assert t >= T5_MAX_DIST
    far_bucket = T5_BUCKETS // 2 - 1
    tab = (t5_table.astype(F32) - t5_table[far_bucket].astype(F32)[None, :]).T
    by_rel = lambda rel: tab[:, _t5_bucket(rel)]
    d_pos = jnp.arange(t, dtype=I32)
    d_neg = jnp.arange(-t, 0, dtype=I32)
    d0 = _toeplitz(by_rel(-d_pos), by_rel(-d_neg), t, t)
    d1 = _toeplitz(by_rel(-d_pos - t), by_rel(-d_neg - t), t, t)
    r = np.arange(t)
    ok = (r[:, None] // CHUNK) <= (r[None, :] // CHUNK)
    d0 = jnp.where(jnp.asarray(ok)[None], d0 * LOG2E, NEG_INF)
    return d0, d1 * LOG2E


def _attn_c(qt, k, vt, lam, subln_g, t5_table, lam_init):
    nb, _, t = qt.shape
    s = nb * t
    d0, d1 = _t5_tiles_t(t5_table, t)
    return pl.pallas_call(
        functools.partial(_attn_c_kernel, out_scale=1.0 - lam_init),
        grid=(C_HEADS, nb),
        in_specs=[pl.BlockSpec(memory_space=pltpu.SMEM),
                  pl.BlockSpec((1, LANES, t), lambda h, i: (i, h, 0)),
                  pl.BlockSpec((s, LANES), lambda h, i: (0, h)),
                  pl.BlockSpec((nb, LANES, t), lambda h, i: (0, h, 0)),
                  pl.BlockSpec((1, t, t), lambda h, i: (h, 0, 0)),
                  pl.BlockSpec((1, t, t), lambda h, i: (h, 0, 0)),
                  _full((LANES, 1))],
        out_specs=pl.BlockSpec((t, LANES), lambda h, i: (i, h)),
        out_shape=jax.ShapeDtypeStruct((s, C_HEADS * 2 * C_HEAD_DIM), BF16),
        scratch_shapes=[pltpu.VMEM((LANES, 2 * t), BF16)] + _flash_scratch(2 * t),
        compiler_params=_params("arbitrary", "arbitrary"),
        name="attn_c",
    )(lam.reshape(1), qt, k, vt, d0, d1, subln_g.reshape(-1, 1))


def _prep_c_kernel(x_ref, wq_ref, wk_ref, wv_ref, qt_ref, k_ref, vt_ref):
    xb = x_ref[...].astype(BF16)
    qt = lax.dot_general(wq_ref[...], xb, NT_DIMS, preferred_element_type=F32)
    qt_ref[0] = (qt * (C_HEAD_DIM ** -0.5 * LOG2E)).astype(BF16)
    k_ref[...] = jnp.dot(xb, wk_ref[...], preferred_element_type=F32).astype(BF16)
    vt_ref[0] = lax.dot_general(wv_ref[...], xb, NT_DIMS, preferred_element_type=F32).astype(BF16)


def _prep_c(x, w_in):
    s = x.shape[0]
    t = ATT_TILE
    wq = w_in[:, :C_QK].T.astype(BF16)
    wk = w_in[:, C_QK:2 * C_QK].astype(BF16)
    wv = w_in[:, 2 * C_QK:].T.astype(BF16)
    row = pl.BlockSpec((t, D_MODEL), lambda i: (i, 0))
    tr = pl.BlockSpec((1, C_QK, t), lambda i: (i, 0, 0))
    sds = jax.ShapeDtypeStruct
    return pl.pallas_call(
        _prep_c_kernel,
        grid=(s // t,),
        in_specs=[row, _full(wq.shape), _full(wk.shape), _full(wv.shape)],
        out_specs=[tr, row, tr],
        out_shape=[sds((s // t, C_QK, t), BF16), sds((s, C_QK), BF16), sds((s // t, C_QK, t), BF16)],
        compiler_params=_params("arbitrary"),
        name="prep_c",
    )(x, wq, wk, wv)


def _post_attn_kernel(*refs, n_in):
    x_ref = refs[0]
    o_refs = refs[1:1 + n_in]
    w_refs = refs[1 + n_in:1 + 2 * n_in]
    g_ref, b_ref, out_ref = refs[1 + 2 * n_in:]
    y = DEEPNORM_ALPHA * x_ref[...]
    for o_ref, w_ref in zip(o_refs, w_refs):
        y = y + jnp.dot(o_ref[...], w_ref[...], preferred_element_type=F32)
    _store_row_tiles(out_ref, _layer_norm(y, g_ref[...], b_ref[...]))


def _post_attn(x, outs, weights, g, b):
    s = x.shape[0]
    t = ATT_TILE
    row = lambda w: pl.BlockSpec((t, w), lambda i: (i, 0))
    ws = [w.astype(BF16) for w in weights]
    return pl.pallas_call(
        functools.partial(_post_attn_kernel, n_in=len(outs)),
        grid=(s // t,),
        in_specs=[row(D_MODEL)] + [row(o.shape[1]) for o in outs] + [_full(w.shape) for w in ws]
        + [_full((1, D_MODEL))] * 2,
        out_specs=pl.BlockSpec((t, ROW_SUBLANES, LANES), lambda i: (i, 0, 0)),
        out_shape=jax.ShapeDtypeStruct((s, ROW_SUBLANES, LANES), F32),
        compiler_params=_params("arbitrary"),
        name="post_attn",
    )(x, *outs, *ws, g.reshape(1, -1), b.reshape(1, -1))


def _router_kernel(x_ref, wr_ref, br_ref, tri_ref, idx_ref, rank_ref, w_ref, cnt_ref, carry):
    step = pl.program_id(0)

    @pl.when(step == 0)
    def _():
        carry[...] = jnp.zeros(carry.shape, F32)

    t = ROUTER_TILE
    logits = lax.dot_general(wr_ref[...], _load_row_tiles(x_ref), NT_DIMS, preferred_element_type=F32,
                             precision=lax.Precision.HIGHEST) + br_ref[...]
    eid = lax.broadcasted_iota(I32, (N_EXPERTS, t), 0).astype(F32)
    work = logits
    vals, sels = [], []
    for _ in range(TOP_K):
        m = jnp.max(work, axis=0, keepdims=True)
        sel = jnp.min(jnp.where(work == m, eid, float(N_EXPERTS)), axis=0, keepdims=True)
        vals.append(m)
        sels.append(sel)
        work = jnp.where(eid == sel, -jnp.inf, work)
    exps = [jnp.exp(v - vals[0]) for v in vals]
    denom = exps[0] + exps[1] + exps[2] + exps[3]

    chosen = jnp.zeros((N_EXPERTS, t), F32)
    for sel in sels:
        chosen = chosen + (eid == sel).astype(F32)
    before = jnp.dot(chosen.astype(BF16), tri_ref[...], preferred_element_type=F32) + carry[...]
    for kk, sel in enumerate(sels):
        rank = jnp.sum(jnp.where(eid == sel, before, 0.0), axis=0, keepdims=True)
        rank_ref[kk:kk + 1, :] = rank.astype(I32)
        idx_ref[kk:kk + 1, :] = sel.astype(I32)
    carry[...] = carry[...] + jnp.sum(chosen, axis=1, keepdims=True)
    cnt_ref[...] = jnp.broadcast_to(carry[...], cnt_ref.shape)

    sub = lax.broadcasted_iota(I32, (LANES, t), 0)
    wrows = jnp.zeros((LANES, t), F32)
    for kk in range(TOP_K):
        wrows = jnp.where(sub == kk, exps[kk] / denom, wrows)
    w_ref[...] = wrows.T


def _router(x1, router_w, router_b):
    s = x1.shape[0]
    t = ROUTER_TILE
    tri = jnp.asarray(np.triu(np.ones((t, t), np.float32), k=1), BF16)
    tok = pl.BlockSpec((TOP_K, t), lambda i: (0, i))
    return pl.pallas_call(
        _router_kernel,
        grid=(s // t,),
        in_specs=[pl.BlockSpec((t, ROW_SUBLANES, LANES), lambda i: (i, 0, 0)),
                  _full((N_EXPERTS, D_MODEL)), _full((N_EXPERTS, 1)), _full((t, t))],
        out_specs=[tok, tok, pl.BlockSpec((t, LANES), lambda i: (i, 0)), _full((N_EXPERTS, LANES))],
        out_shape=[jax.ShapeDtypeStruct((TOP_K, s), I32), jax.ShapeDtypeStruct((TOP_K, s), I32),
                   jax.ShapeDtypeStruct((s, LANES), F32), jax.ShapeDtypeStruct((N_EXPERTS, LANES), F32)],
        scratch_shapes=[pltpu.VMEM((N_EXPERTS, 1), F32)],
        compiler_params=_params("arbitrary"),
        name="router",
    )(x1, router_w.T, router_b.reshape(-1, 1), tri)


def _dispatch_kernel(pad_ref, pos_hbm, x_ref, xs_hbm, pos_smem, zeros, pos_sem, row_sem, zero_sem):
    i = pl.program_id(0)
    t = DISPATCH_TILE
    n = t * TOP_K
    tm = EXPERT_TILE

    @pl.when(i == 0)
    def _():
        zeros[...] = jnp.zeros(zeros.shape, F32)

        def fill(e):
            row = pl.multiple_of(jnp.maximum(pad_ref[e], 0), tm)
            return pltpu.make_async_copy(zeros, xs_hbm.at[pl.ds(row, tm)], zero_sem)

        for e in range(2 * N_EXPERTS):
            pl.when(pad_ref[e] >= 0)(lambda e=e: fill(e).start())
        for e in range(2 * N_EXPERTS):
            pl.when(pad_ref[e] >= 0)(lambda e=e: fill(e).wait())

    cp = pltpu.make_async_copy(pos_hbm.at[pl.ds(pl.multiple_of(i * n, n), n)], pos_smem, pos_sem)
    cp.start()
    cp.wait()

    def body(g, carry):
        for j in range(ISSUE_UNROLL // TOP_K):
            tok = g * (ISSUE_UNROLL // TOP_K) + j
            for kk in range(TOP_K):
                pltpu.make_async_copy(x_ref.at[tok], xs_hbm.at[pos_smem[tok * TOP_K + kk]],
                                      row_sem).start(priority=kk % 2)
        return carry

    lax.fori_loop(0, n // ISSUE_UNROLL, body, 0)
    for kk in range(TOP_K):
        pltpu.make_async_copy(x_ref, xs_hbm.at[pl.ds(0, t)], row_sem).wait()


def _dispatch(pad_rows, pos_flat, x1, n_rows):
    s = x1.shape[0]
    t = DISPATCH_TILE
    any_spec = pl.BlockSpec(memory_space=pl.ANY)
    grid_spec = pltpu.PrefetchScalarGridSpec(
        num_scalar_prefetch=1,
        grid=(s // t,),
        in_specs=[any_spec, pl.BlockSpec((t, ROW_SUBLANES, LANES), lambda i, pad: (i, 0, 0))],
        out_specs=any_spec,
        scratch_shapes=[pltpu.SMEM((t * TOP_K,), I32), pltpu.VMEM((EXPERT_TILE, ROW_SUBLANES, LANES), F32),
                        pltpu.SemaphoreType.DMA, pltpu.SemaphoreType.DMA, pltpu.SemaphoreType.DMA],
    )
    return pl.pallas_call(
        _dispatch_kernel,
        grid_spec=grid_spec,
        out_shape=jax.ShapeDtypeStruct((n_rows, ROW_SUBLANES, LANES), F32),
        compiler_params=_params("arbitrary"),
        name="dispatch",
    )(pad_rows, pos_flat, x1)


def _gmm_kernel(te_ref, nt_ref, xs_ref, wi_ref, bi_ref, wo_ref, bo_ref, y_ref, wi_bf, wo_bf):
    i = pl.program_id(0)
    n_tiles = nt_ref[0]
    prev = te_ref[jnp.maximum(i - 1, 0)]
    fresh = jnp.logical_or(i == 0, te_ref[i] != prev)

    @pl.when(jnp.logical_and(i < n_tiles, fresh))
    def _():
        def cast(c, carry):
            rows = pl.ds(pl.multiple_of(c * CAST_ROWS, CAST_ROWS), CAST_ROWS)
            wi_bf[rows, :] = wi_ref[0, rows, :].astype(BF16)
            wo_bf[rows, :] = wo_ref[0, rows, :].astype(BF16)
            return carry

        lax.fori_loop(0, D_MODEL // CAST_ROWS, cast, 0)

    @pl.when(i < n_tiles)
    def _():
        xb = _load_row_tiles(xs_ref).astype(BF16)
        h = jnp.dot(xb, wi_bf[...], preferred_element_type=F32) + bi_ref[0]
        glu = jnp.minimum(h[:, :D_FF], SWIGLU_LIMIT)
        lin = jnp.clip(h[:, D_FF:], -SWIGLU_LIMIT, SWIGLU_LIMIT)
        act = glu * jax.nn.sigmoid(SWIGLU_ALPHA * glu) * (lin + 1.0)
        _store_row_tiles(y_ref, jnp.dot(act.astype(BF16), wo_bf[...], preferred_element_type=F32) + bo_ref[0])

    @pl.when(i >= n_tiles)
    def _():
        y_ref[...] = jnp.zeros(y_ref.shape, F32)


def _gmm(layer, tile_expert, n_tiles, xs, w_in, b_in, w_out, b_out):
    n_rows = xs.shape[0]
    tm = EXPERT_TILE
    last = lambda i, nt: jnp.maximum(jnp.minimum(i, nt[0] - 1), 0)
    expert = lambda i, te, nt: layer * N_EXPERTS + te[last(i, nt)]
    grid_spec = pltpu.PrefetchScalarGridSpec(
        num_scalar_prefetch=2,
        grid=(n_rows // tm,),
        in_specs=[pl.BlockSpec((tm, ROW_SUBLANES, LANES), lambda i, te, nt: (last(i, nt), 0, 0)),
                  pl.BlockSpec((1, D_MODEL, 2 * D_FF), lambda i, te, nt: (expert(i, te, nt), 0, 0)),
                  pl.BlockSpec((1, 1, 2 * D_FF), lambda i, te, nt: (expert(i, te, nt), 0, 0)),
                  pl.BlockSpec((1, D_FF, D_MODEL), lambda i, te, nt: (expert(i, te, nt), 0, 0)),
                  pl.BlockSpec((1, 1, D_MODEL), lambda i, te, nt: (expert(i, te, nt), 0, 0))],
        out_specs=pl.BlockSpec((tm, ROW_SUBLANES, LANES), lambda i, te, nt: (i, 0, 0)),
        scratch_shapes=[pltpu.VMEM((D_MODEL, 2 * D_FF), BF16), pltpu.VMEM((D_FF, D_MODEL), BF16)],
    )
    return pl.pallas_call(
        _gmm_kernel,
        grid_spec=grid_spec,
        out_shape=jax.ShapeDtypeStruct((n_rows, ROW_SUBLANES, LANES), F32),
        compiler_params=_params("arbitrary"),
        name="expert_mlp",
    )(tile_expert, n_tiles, xs, w_in, b_in, w_out, b_out)


def _combine_kernel(pos_hbm, y_hbm, x_ref, w_ref, p_ref, wg_ref, wp_ref, g_ref, b_ref, out_ref,
                    pos_smem, ybuf, pos_sem, row_sem):
    i = pl.program_id(0)
    t = COMBINE_TILE
    n = t * TOP_K

    def fetch_indices(step):
        base = pl.multiple_of((step & 1) * n, n)
        cp = pltpu.make_async_copy(pos_hbm.at[pl.ds(pl.multiple_of(step * n, n), n)],
                                   pos_smem.at[pl.ds(base, n)], pos_sem)
        cp.start()
        cp.wait()

    def row_copy(step, tok, kk):
        slot = step & 1
        return pltpu.make_async_copy(y_hbm.at[pos_smem[slot * n + tok * TOP_K + kk]], ybuf.at[slot, kk, tok],
                                     row_sem.at[slot])

    def wait_step(step):
        slot = step & 1
        for kk in range(TOP_K):
            pltpu.make_async_copy(y_hbm.at[pl.ds(0, t)], ybuf.at[slot, kk], row_sem.at[slot]).wait()

    @pl.when(i == 0)
    def _():
        fetch_indices(i)

        def issue(tok, carry):
            for kk in range(TOP_K):
                row_copy(i, tok, kk).start(priority=kk % 2)
            return carry

        lax.fori_loop(0, t, issue, 0, unroll=2)

    nxt = i + 1
    fetch_indices(nxt)
    x = _load_row_tiles(x_ref)
    pb = p_ref[...].astype(BF16)
    w = w_ref[...]
    for tok in range(t):
        for kk in range(TOP_K):
            row_copy(nxt, tok, kk).start(priority=kk % 2)
    gate = jax.nn.sigmoid(jnp.dot(x.astype(BF16), wg_ref[...], preferred_element_type=F32))
    proj = jnp.dot(pb, wp_ref[...], preferred_element_type=F32)
    y = DEEPNORM_ALPHA * x + gate * proj

    wait_step(i)
    for kk in range(TOP_K):
        y = y + w[:, kk:kk + 1] * _load_row_tiles(ybuf, i & 1, kk)
    out_ref[...] = _layer_norm(y, g_ref[...], b_ref[...])

    @pl.when(nxt == pl.num_programs(0))
    def _():
        wait_step(nxt)


def _combine(pos_flat, y, x1, wcol, p, gate_w, proj_w, g, b):
    s = x1.shape[0]
    t = COMBINE_TILE
    any_spec = pl.BlockSpec(memory_space=pl.ANY)
    row = lambda w: pl.BlockSpec((t, w), lambda i: (i, 0))
    return pl.pallas_call(
        _combine_kernel,
        grid=(s // t,),
        in_specs=[any_spec, any_spec, pl.BlockSpec((t, ROW_SUBLANES, LANES), lambda i: (i, 0, 0)),
                  row(LANES), row(PLE_DIM),
                  _full((D_MODEL, D_MODEL)), _full((PLE_DIM, D_MODEL)),
                  _full((1, D_MODEL)), _full((1, D_MODEL))],
        out_specs=row(D_MODEL),
        out_shape=jax.ShapeDtypeStruct((s, D_MODEL), F32),
        scratch_shapes=[pltpu.SMEM((2 * t * TOP_K,), I32),
                        pltpu.VMEM((2, TOP_K, t, ROW_SUBLANES, LANES), F32),
                        pltpu.SemaphoreType.DMA, pltpu.SemaphoreType.DMA((2,))],
        compiler_params=_params("arbitrary"),
        name="combine",
    )(pos_flat, y, x1, wcol, p, gate_w.astype(BF16), proj_w.astype(BF16), g.reshape(1, -1), b.reshape(1, -1))


def _moe_block(layer, x1, p, router_w, router_b, w_in, b_in, w_out, b_out, gate_w, proj_w, g, b):
    s = x1.shape[0]
    tm = EXPERT_TILE
    n_tiles_max = s * TOP_K // tm + N_EXPERTS
    idx, rank, wcol, cnt = _router(x1, router_w, router_b)
    counts = cnt[:, 0].astype(I32)
    tiles_per = (counts + tm - 1) // tm
    tile_end = jnp.cumsum(tiles_per)
    offsets = (tile_end - tiles_per) * tm
    experts = jnp.arange(N_EXPERTS, dtype=I32)
    pos = rank + jnp.sum(jnp.where(idx[..., None] == experts, offsets, 0), axis=-1)
    pos_flat = pos.T.reshape(-1)
    tile_ids = jnp.arange(n_tiles_max, dtype=I32)
    tile_expert = jnp.minimum(jnp.sum((tile_end[None, :] <= tile_ids[:, None]).astype(I32), axis=-1),
                              N_EXPERTS - 1)
    n_tiles = tile_end[-1:].astype(I32)
    last_tiles = jnp.where(tiles_per > 0, tile_end - 1, -1)
    tail_tiles = jnp.where(tile_end[-1] + experts < n_tiles_max, tile_end[-1] + experts, -1)
    fill_tiles = jnp.concatenate([last_tiles, tail_tiles])
    pad_rows = jnp.where(fill_tiles >= 0, fill_tiles * tm, -1).astype(I32)
    xs = _dispatch(pad_rows, pos_flat, x1, n_tiles_max * tm)
    y = _gmm(layer, tile_expert, n_tiles, xs, w_in, b_in, w_out, b_out)
    pos_ahead = jnp.concatenate([pos_flat, jnp.zeros((COMBINE_TILE * TOP_K,), I32)])
    return _combine(pos_ahead, y, x1, wcol, p, gate_w, proj_w, g, b)


def _lambda_init(layer_idx):
    return 0.8 - 0.6 * math.exp(-0.3 * layer_idx)


def kernel(x, p, ab_w_in, ab_rel_bias, ab_q_norm, ab_kv_norm, ab_w_uq, ab_w_ukv, ab_w_out, c_w_in, c_lambda, c_subln, c_w_out, t5_table, ln_mix_g, ln_mix_b, ln_ffn_g, ln_ffn_b, router_w, router_b, exp_w_in, exp_b_in, exp_w_out, exp_b_out, ple_gate_w, ple_proj_w):
    batch, s, _ = x.shape
    assert batch == 1 and s % DISPATCH_TILE == 0 and s % ATT_TILE == 0
    xr = x[0]
    w_in_all = exp_w_in.reshape(DEPTH * N_EXPERTS, D_MODEL, 2 * D_FF)
    b_in_all = exp_b_in.reshape(DEPTH * N_EXPERTS, 1, 2 * D_FF)
    w_out_all = exp_w_out.reshape(DEPTH * N_EXPERTS, D_FF, D_MODEL)
    b_out_all = exp_b_out.reshape(DEPTH * N_EXPERTS, 1, D_MODEL)
    for i in range(DEPTH):
        j = i // 2
        if i % 2 == 0:
            qa, ka, va, qbt, kb, vbt = _prep_ab(xr, ab_w_in[j], ab_q_norm[j], ab_kv_norm[j], ab_w_uq[j],
                                                ab_w_ukv[j])
            o_a = _attn_a(qa, ka, va, ab_rel_bias[j])
            o_b = _attn_b(qbt, kb, vbt)
            x1 = _post_attn(xr, [o_a, o_b], [ab_w_out[j][:A_WIDTH], ab_w_out[j][A_WIDTH:]],
                            ln_mix_g[i], ln_mix_b[i])
        else:
            qt, k, vt = _prep_c(xr, c_w_in[j])
            lp = c_lambda[j].astype(F32)
            lam_init = _lambda_init(i)
            lam = jnp.exp(jnp.sum(lp[0] * lp[1])) - jnp.exp(jnp.sum(lp[2] * lp[3])) + lam_init
            o_c = _attn_c(qt, k, vt, lam, c_subln[j], t5_table, lam_init)
            x1 = _post_attn(xr, [o_c], [c_w_out[j]], ln_mix_g[i], ln_mix_b[i])
        xr = _moe_block(i, x1, p[i, 0], router_w[i], router_b[i], w_in_all, b_in_all, w_out_all, b_out_all,
                        ple_gate_w[i], ple_proj_w[i], ln_ffn_g[i], ln_ffn_b[i])
    return xr[None]
```

```python
import functools
import math

import jax
import jax.numpy as jnp
import numpy as np
from jax import lax
from jax.experimental import pallas as pl
from jax.experimental.pallas import tpu as pltpu

F32 = jnp.float32
BF16 = jnp.bfloat16
I32 = jnp.int32

D_MODEL = 1024
DEPTH = 4
CHUNK = 64
PLE_DIM = 256

A_HEADS = 8
A_HEAD_DIM = 64
A_LEFT_CHUNKS = 8
A_REL_CLIP = 128
A_WIDTH = A_HEADS * A_HEAD_DIM

B_HEADS = 8
B_NOPE_DIM = 64
B_ROPE_DIM = 32
B_V_DIM = 64
B_Q_RANK = 512
B_KV_RANK = 256
ROPE_BASE = 10000.0

C_HEADS = 8
C_HEAD_DIM = 64
C_QK = 2 * C_HEADS * C_HEAD_DIM

T5_BUCKETS = 32
T5_MAX_DIST = 128

N_EXPERTS = 32
TOP_K = 4
D_FF = 1024
SWIGLU_ALPHA = 1.702
SWIGLU_LIMIT = 7.0

DEEPNORM_ALPHA = (2.0 * DEPTH) ** 0.25
LN_EPS = 1e-5
RMS_EPS = 1e-6
NEG_INF = -1e30

LANES = 128
ROW_SUBLANES = D_MODEL // LANES
VMEM_LIMIT = 56 * 1024 * 1024

ATT_TILE = 512
COL_CHUNK = 256
ROW_CHUNK = 64
SUM_ROWS = 16
LOG2E = math.log2(math.e)
BAND_GROUP = 256
BAND_WINDOW = BAND_GROUP + A_LEFT_CHUNKS * CHUNK
ROUTER_TILE = 512
EXPERT_TILE = 512
DISPATCH_TILE = 512
COMBINE_TILE = 256
CAST_ROWS = 128
ISSUE_UNROLL = 8

NT_DIMS = (((1,), (1,)), ((), ()))


def _params(*sem):
    return pltpu.CompilerParams(dimension_semantics=sem, vmem_limit_bytes=VMEM_LIMIT)


def _full(shape):
    n = len(shape)
    return pl.BlockSpec(shape, lambda *_: (0,) * n)


def _layer_norm(y, g, b):
    mu = jnp.mean(y, axis=-1, keepdims=True)
    d = y - mu
    var = jnp.mean(d * d, axis=-1, keepdims=True)
    return d * lax.rsqrt(var + LN_EPS) * g + b


def _rms_norm(c, g):
    inv = lax.rsqrt(jnp.mean(c * c, axis=-1, keepdims=True) + RMS_EPS)
    return c * inv * g


def _load_row_tiles(ref, *lead):
    tiles = pltpu.einshape("ral->arl", ref[lead] if lead else ref[...])
    return jnp.concatenate([tiles[a] for a in range(ROW_SUBLANES)], axis=1)


def _store_row_tiles(ref, value):
    tiles = jnp.stack([value[:, a * LANES:(a + 1) * LANES] for a in range(ROW_SUBLANES)], axis=0)
    ref[...] = pltpu.einshape("arl->ral", tiles)


def _toeplitz(vals_pos, vals_neg, nr, nc):
    w = jnp.concatenate([vals_pos, vals_neg], axis=-1)
    lw = nc + nr
    assert w.shape[-1] == lw
    flat = jnp.tile(w, (1,) * (w.ndim - 1) + (nr,))[..., :nr * (lw - 1)]
    return flat.reshape(w.shape[:-1] + (nr, lw - 1))[..., :nc]


def _prep_ab_kernel(x_ref, wa_ref, wcq_ref, wckv_ref, wkr_ref, wkrr_ref, qn_ref, kvn_ref,
                    wuqa_ref, wuqb_ref, wukn_ref, wuv_ref, place_ref, ctab_ref, stab_ref,
                    cos_ref, sin_ref,
                    qa_ref, ka_ref, va_ref, qbt_ref, kb_ref, vbt_ref):
    xb = x_ref[...].astype(BF16)
    ha = jnp.dot(xb, wa_ref[...], preferred_element_type=F32)
    qa_ref[...] = (ha[:, :A_WIDTH] * (A_HEAD_DIM ** -0.5)).astype(BF16)
    ka_ref[...] = ha[:, A_WIDTH:2 * A_WIDTH].astype(BF16)
    va_ref[...] = ha[:, 2 * A_WIDTH:].astype(BF16)

    cq = jnp.dot(xb, wcq_ref[...], preferred_element_type=F32)
    cqn = _rms_norm(cq, qn_ref[...]).astype(BF16)
    qa_part = lax.dot_general(wuqa_ref[...], cqn, NT_DIMS, preferred_element_type=F32)
    qb_part = lax.dot_general(wuqb_ref[...], cqn, NT_DIMS, preferred_element_type=F32)
    ct = ctab_ref[...]
    st = stab_ref[...]
    scale = (B_NOPE_DIM + B_ROPE_DIM) ** -0.5 * LOG2E
    for h in range(B_HEADS):
        sl = slice(h * LANES, (h + 1) * LANES)
        qbt_ref[0, sl, :] = ((qa_part[sl, :] * ct + qb_part[sl, :] * st) * scale).astype(BF16)

    ckv = jnp.dot(xb, wckv_ref[...], preferred_element_type=F32)
    ckvn = _rms_norm(ckv, kvn_ref[...]).astype(BF16)
    kr = jnp.dot(xb, wkr_ref[...], preferred_element_type=F32)
    krr = jnp.dot(xb, wkrr_ref[...], preferred_element_type=F32)
    k_rope = (kr * cos_ref[...] + krr * sin_ref[...]).astype(BF16)
    kb = jnp.dot(ckvn, wukn_ref[...], preferred_element_type=F32)
    kb = kb + jnp.dot(k_rope, place_ref[...], preferred_element_type=F32)
    kb_ref[...] = kb.astype(BF16)
    vbt_ref[0] = lax.dot_general(wuv_ref[...], ckvn, NT_DIMS, preferred_element_type=F32).astype(BF16)


def _rot_half_cols(w):
    half = w.shape[-1] // 2
    return jnp.concatenate([-w[..., half:], w[..., :half]], axis=-1)


def _prep_ab(x, w_in, q_norm, kv_norm, w_uq, w_ukv):
    s = x.shape[0]
    t = ATT_TILE
    splits = [3 * A_WIDTH, 3 * A_WIDTH + B_Q_RANK, 3 * A_WIDTH + B_Q_RANK + B_KV_RANK]
    wa = w_in[:, :splits[0]].astype(BF16)
    wcq = w_in[:, splits[0]:splits[1]].astype(BF16)
    wckv = w_in[:, splits[1]:splits[2]].astype(BF16)
    wkr_f = w_in[:, splits[2]:]
    wkr = wkr_f.astype(BF16)
    wkrr = _rot_half_cols(wkr_f).astype(BF16)

    qd = B_NOPE_DIM + B_ROPE_DIM
    w_uq3 = w_uq.reshape(B_Q_RANK, B_HEADS, qd)
    zpad = jnp.zeros((B_Q_RANK, B_HEADS, LANES - qd), F32)
    wuqa = jnp.concatenate([w_uq3, zpad], axis=-1).reshape(B_Q_RANK, B_HEADS * LANES).T.astype(BF16)
    wuqb = jnp.concatenate([jnp.zeros((B_Q_RANK, B_HEADS, B_NOPE_DIM), F32),
                            _rot_half_cols(w_uq3[..., B_NOPE_DIM:]), zpad],
                           axis=-1).reshape(B_Q_RANK, B_HEADS * LANES).T.astype(BF16)
    w_ukv3 = w_ukv.reshape(B_KV_RANK, B_HEADS, B_NOPE_DIM + B_V_DIM)
    wukn = jnp.concatenate([w_ukv3[..., :B_NOPE_DIM],
                            jnp.zeros((B_KV_RANK, B_HEADS, LANES - B_NOPE_DIM), F32)],
                           axis=-1).reshape(B_KV_RANK, B_HEADS * LANES).astype(BF16)
    wuv = w_ukv3[..., B_NOPE_DIM:].reshape(B_KV_RANK, B_HEADS * B_V_DIM).T.astype(BF16)
    place = np.zeros((B_ROPE_DIM, B_HEADS * LANES), np.float32)
    for h in range(B_HEADS):
        for r in range(B_ROPE_DIM):
            place[r, h * LANES + B_NOPE_DIM + r] = 1.0
    place = jnp.asarray(place, BF16)

    pos = jnp.arange(s, dtype=F32)
    inv_freq = ROPE_BASE ** (-jnp.arange(0, B_ROPE_DIM, 2, dtype=F32) / B_ROPE_DIM)
    ang = pos[:, None] * inv_freq[None, :]
    cos = jnp.concatenate([jnp.cos(ang)] * 2, axis=-1)
    sin = jnp.concatenate([jnp.sin(ang)] * 2, axis=-1)
    ctab = jnp.concatenate([jnp.ones((s, B_NOPE_DIM), F32), cos, jnp.zeros((s, LANES - qd), F32)], axis=-1).T
    stab = jnp.concatenate([jnp.zeros((s, B_NOPE_DIM), F32), sin, jnp.zeros((s, LANES - qd), F32)], axis=-1).T

    row = lambda w: pl.BlockSpec((t, w), lambda i: (i, 0))
    col = pl.BlockSpec((LANES, t), lambda i: (0, i))
    tr = lambda w: pl.BlockSpec((1, w, t), lambda i: (i, 0, 0))
    consts = [wa, wcq, wckv, wkr, wkrr, q_norm.reshape(1, -1), kv_norm.reshape(1, -1),
              wuqa, wuqb, wukn, wuv, place]
    sds = jax.ShapeDtypeStruct
    return pl.pallas_call(
        _prep_ab_kernel,
        grid=(s // t,),
        in_specs=[row(D_MODEL)] + [_full(c.shape) for c in consts]
        + [col, col, row(B_ROPE_DIM), row(B_ROPE_DIM)],
        out_specs=[row(A_WIDTH), row(A_WIDTH), row(A_WIDTH),
                   tr(B_HEADS * LANES), row(B_HEADS * LANES), tr(B_HEADS * B_V_DIM)],
        out_shape=[sds((s, A_WIDTH), BF16)] * 3
        + [sds((s // t, B_HEADS * LANES, t), BF16), sds((s, B_HEADS * LANES), BF16),
           sds((s // t, B_HEADS * B_V_DIM, t), BF16)],
        compiler_params=_params("arbitrary"),
        name="prep_ab",
    )(x, *consts, ctab, stab, cos, sin)


def _attn_a_kernel(q_ref, kp_ref, kc_ref, vp_ref, vc_ref, bias_ref, o_ref, kbuf, vbuf):
    j = pl.program_id(1)
    t = ATT_TILE
    kbuf[0:t, :] = kp_ref[...]
    kbuf[t:2 * t, :] = kc_ref[...]
    vbuf[0:t, :] = vp_ref[...]
    vbuf[t:2 * t, :] = vc_ref[...]
    lo = lax.broadcasted_iota(I32, (1, LANES), 1) < A_HEAD_DIM
    col = lax.broadcasted_iota(I32, (1, BAND_WINDOW), 1)
    for g in range(t // BAND_GROUP):
        r0 = g * BAND_GROUP
        q = q_ref[r0:r0 + BAND_GROUP, :]
        k = kbuf[r0:r0 + BAND_WINDOW, :]
        v = vbuf[r0:r0 + BAND_WINDOW, :]
        valid = col + r0 >= jnp.where(j > 0, 0, t)
        outs = []
        for hh in range(2):
            qm = jnp.where(lo if hh == 0 else jnp.logical_not(lo), q, jnp.zeros_like(q))
            sc = lax.dot_general(qm, k, NT_DIMS, preferred_element_type=F32)
            sc = jnp.where(valid, sc + bias_ref[hh], NEG_INF)
            m = jnp.max(sc, axis=-1, keepdims=True)
            p = jnp.exp(sc - m)
            l = jnp.sum(p, axis=-1, keepdims=True)
            o = jnp.dot(p.astype(BF16), v, preferred_element_type=F32)
            outs.append(o / l)
        o_ref[r0:r0 + BAND_GROUP, :] = jnp.where(lo, outs[0], outs[1]).astype(BF16)


def _band_bias(rel_bias):
    qi = np.arange(BAND_GROUP)[:, None]
    kr = np.arange(BAND_WINDOW)[None, :]
    dchunk = (qi // CHUNK + A_LEFT_CHUNKS) - kr // CHUNK
    allowed = (dchunk >= 0) & (dchunk <= A_LEFT_CHUNKS)
    d_pos = np.arange(BAND_WINDOW)
    d_neg = np.arange(-BAND_GROUP, 0)
    to_idx = lambda d: np.clip(A_LEFT_CHUNKS * CHUNK - d, -A_REL_CLIP, A_REL_CLIP) + A_REL_CLIP
    tab = rel_bias.astype(F32).T
    table = _toeplitz(tab[:, to_idx(d_pos)], tab[:, to_idx(d_neg)], BAND_GROUP, BAND_WINDOW)
    return jnp.where(jnp.asarray(allowed)[None], table, NEG_INF)


def _attn_a(qa, ka, va, rel_bias):
    s = qa.shape[0]
    t = ATT_TILE
    bias = _band_bias(rel_bias)
    cur = pl.BlockSpec((t, LANES), lambda p, j: (j, p))
    prev = pl.BlockSpec((t, LANES), lambda p, j: (jnp.maximum(j - 1, 0), p))
    return pl.pallas_call(
        _attn_a_kernel,
        grid=(A_HEADS // 2, s // t),
        in_specs=[cur, prev, cur, prev, cur,
                  pl.BlockSpec((2, BAND_GROUP, BAND_WINDOW), lambda p, j: (p, 0, 0))],
        out_specs=cur,
        out_shape=jax.ShapeDtypeStruct((s, A_WIDTH), BF16),
        scratch_shapes=[pltpu.VMEM((2 * t, LANES), BF16), pltpu.VMEM((2 * t, LANES), BF16)],
        compiler_params=_params("arbitrary", "arbitrary"),
        name="attn_a",
    )(qa, ka, ka, va, va, bias)


class _FlashState:
    def __init__(self, m, l, acc, s0, s1, p0, p1, a0, a1):
        self.m, self.l, self.acc = m, l, acc
        self.s, self.p, self.a = (s0, s1), (p0, p1), (a0, a1)
        self.n_chunks = m.shape[1] // COL_CHUNK


def _flash_scratch(n_cols):
    t, w = ATT_TILE, COL_CHUNK
    return [pltpu.VMEM((1, n_cols), F32), pltpu.VMEM((1, n_cols), F32), pltpu.VMEM((LANES, n_cols), F32),
            pltpu.VMEM((t, w), F32), pltpu.VMEM((t, w), F32), pltpu.VMEM((t, w), BF16), pltpu.VMEM((t, w), BF16),
            pltpu.VMEM((1, w), F32), pltpu.VMEM((1, w), F32)]


def _flash_start(st, score_fn):
    st.m[...] = jnp.full(st.m.shape, NEG_INF, F32)
    st.l[...] = jnp.zeros(st.l.shape, F32)
    st.acc[...] = jnp.zeros(st.acc.shape, F32)
    st.p[1][...] = jnp.zeros(st.p[1].shape, BF16)
    st.a[1][...] = jnp.ones(st.a[1].shape, F32)
    st.s[0][...] = score_fn(0, 0)


def _flash_values(st, vt_fn, kt, c, b):
    cs = slice(c * COL_CHUNK, (c + 1) * COL_CHUNK)
    lhs = jnp.concatenate([vt_fn(kt), jnp.ones((SUM_ROWS, ATT_TILE), BF16)], axis=0)
    pv = jnp.dot(lhs, st.p[b][...], preferred_element_type=F32)
    alpha = st.a[b][...]
    st.acc[:, cs] = alpha * st.acc[:, cs] + pv[:LANES]
    st.l[:, cs] = alpha * st.l[:, cs] + pv[LANES:LANES + 1]


def _flash_region(st, score_fn, vt_fn, tiles, has_next):
    units = [(ti, c) for ti in range(len(tiles)) for c in range(st.n_chunks)]
    assert len(units) % 2 == 0
    for u, (ti, c) in enumerate(units):
        b = u % 2
        kt, bias_fn = tiles[ti]
        if u + 1 < len(units):
            nti, nc = units[u + 1]
            st.s[1 - b][...] = score_fn(tiles[nti][0], nc)
        elif has_next:
            st.s[1 - b][...] = score_fn(tiles[-1][0] + 1, 0)
        if u > 0:
            pti, pc = units[u - 1]
            _flash_values(st, vt_fn, tiles[pti][0], pc, 1 - b)
        else:
            _flash_values(st, vt_fn, jnp.maximum(kt - 1, 0), st.n_chunks - 1, 1 - b)
        cs = slice(c * COL_CHUNK, (c + 1) * COL_CHUNK)
        c0 = (c * COL_CHUNK) % ATT_TILE

        def rows(r, s_ref=st.s[b], bias_fn=bias_fn, c0=c0):
            blk = s_ref[r * ROW_CHUNK:(r + 1) * ROW_CHUNK, :]
            if bias_fn is not None:
                blk = blk + bias_fn(c0, r)
            return blk

        n_rows = ATT_TILE // ROW_CHUNK
        part = rows(0)
        for r in range(1, n_rows):
            part = jnp.maximum(part, rows(r))
        m_prev = st.m[:, cs]
        m_new = jnp.maximum(m_prev, jnp.max(part, axis=0, keepdims=True))
        st.a[b][...] = jnp.exp2(m_prev - m_new)
        st.m[:, cs] = m_new
        for r in range(n_rows):
            st.p[b][r * ROW_CHUNK:(r + 1) * ROW_CHUNK, :] = jnp.exp2(rows(r) - m_new).astype(BF16)


def _flash_sweep(st, score_fn, vt_fn, n_plain, tail_regions, last_tile):
    _flash_start(st, score_fn)

    def quad(j, carry):
        _flash_region(st, score_fn, vt_fn, [(4 * j + d, None) for d in range(4)], True)
        return carry

    n_quads = lax.shift_right_logical(n_plain, 2)
    lax.fori_loop(0, n_quads, quad, 0)
    done = n_quads * 4

    @pl.when((n_plain & 2) == 2)
    def _():
        _flash_region(st, score_fn, vt_fn, [(done, None), (done + 1, None)], True)

    @pl.when((n_plain & 1) == 1)
    def _():
        _flash_region(st, score_fn, vt_fn, [(n_plain - 1, None)], True)

    tail_regions()
    _flash_values(st, vt_fn, last_tile, st.n_chunks - 1, 1)


def _chunk_causal_bias_t(t):
    r = np.arange(t)
    ok = (r[:, None] // CHUNK) <= (r[None, :] // CHUNK)
    return jnp.asarray(np.where(ok, 0.0, NEG_INF), F32)


def _attn_b_kernel(qt_ref, k_ref, vt_ref, diag_ref, o_ref, *scratch):
    i = pl.program_id(1)
    t = ATT_TILE
    st = _FlashState(*scratch)
    per_head = t // COL_CHUNK

    def score(kt, c):
        hh = c // per_head
        c0 = (c % per_head) * COL_CHUNK
        k = k_ref[pl.ds(pl.multiple_of(kt * t, t), t), hh * LANES:(hh + 1) * LANES]
        return jnp.dot(k, qt_ref[0, hh * LANES:(hh + 1) * LANES, c0:c0 + COL_CHUNK],
                       preferred_element_type=F32)

    vt_fn = lambda kt: vt_ref[kt]

    def tail():
        diag = lambda c0, r: diag_ref[r * ROW_CHUNK:(r + 1) * ROW_CHUNK, c0:c0 + COL_CHUNK]
        _flash_region(st, score, vt_fn, [(i, diag)], False)

    _flash_sweep(st, score, vt_fn, i, tail, i)
    o2 = st.acc[...] / st.l[...]
    ot = jnp.concatenate([o2[:B_V_DIM, :t], o2[B_V_DIM:, t:]], axis=0)
    o_ref[...] = ot.T.astype(BF16)


def _attn_b(qbt, kb, vbt):
    nb, _, t = qbt.shape
    s = nb * t
    return pl.pallas_call(
        _attn_b_kernel,
        grid=(B_HEADS // 2, nb),
        in_specs=[pl.BlockSpec((1, 2 * LANES, t), lambda p, i: (i, p, 0)),
                  pl.BlockSpec((s, 2 * LANES), lambda p, i: (0, p)),
                  pl.BlockSpec((nb, LANES, t), lambda p, i: (0, p, 0)),
                  _full((t, t))],
        out_specs=pl.BlockSpec((t, LANES), lambda p, i: (i, p)),
        out_shape=jax.ShapeDtypeStruct((s, B_HEADS * B_V_DIM), BF16),
        scratch_shapes=_flash_scratch(2 * t),
        compiler_params=_params("arbitrary", "arbitrary"),
        name="attn_b",
    )(qbt, kb, vbt, _chunk_causal_bias_t(t))


def _attn_c_kernel(lam_ref, qt_ref, k_ref, vt_ref, d0_ref, d1_ref, g_ref, o_ref, q2_ref, *scratch, out_scale):
    i = pl.program_id(1)
    t = ATT_TILE
    st = _FlashState(*scratch)
    d = C_HEAD_DIM
    zero = jnp.zeros((d, t), BF16)
    q2_ref[:d, :t] = qt_ref[0, :d, :]
    q2_ref[d:, :t] = zero
    q2_ref[:d, t:] = zero
    q2_ref[d:, t:] = qt_ref[0, d:, :]

    def score(kt, c):
        return jnp.dot(k_ref[pl.ds(pl.multiple_of(kt * t, t), t), :],
                       q2_ref[:, c * COL_CHUNK:(c + 1) * COL_CHUNK], preferred_element_type=F32)

    vt_fn = lambda kt: vt_ref[kt]

    def tail():
        tile_of = lambda ref: lambda c0, r: ref[0, r * ROW_CHUNK:(r + 1) * ROW_CHUNK, c0:c0 + COL_CHUNK]
        diag = (i, tile_of(d0_ref))

        @pl.when(i > 0)
        def _():
            below = (i - 1, tile_of(d1_ref))
            _flash_region(st, score, vt_fn, [below, diag], False)

        @pl.when(i == 0)
        def _():
            _flash_region(st, score, vt_fn, [diag], False)

    _flash_sweep(st, score, vt_fn, jnp.maximum(i - 1, 0), tail, i)

    o2 = st.acc[...] / st.l[...]
    ot = o2[:, :t] - lam_ref[0] * o2[:, t:]
    inv = lax.rsqrt(jnp.mean(ot * ot, axis=0, keepdims=True) + RMS_EPS)
    o_ref[...] = (ot * inv * g_ref[...] * out_scale).T.astype(BF16)


def _t5_bucket(rel):
    half = T5_BUCKETS // 2
    max_exact = half // 2
    ret = jnp.where(rel > 0, half, 0)
    n = jnp.abs(rel)
    large = max_exact + (jnp.log(jnp.maximum(n, max_exact).astype(F32) / max_exact)
                         / math.log(T5_MAX_DIST / max_exact) * (half - max_exact)).astype(I32)
    large = jnp.minimum(large, half - 1)
    return ret + jnp.where(n < max_exact, n, large)


def _t5_tiles_t(t5_table, t):
    assert t >= T5_MAX_DIST
    far_bucket = T5_BUCKETS // 2 - 1
    tab = (t5_table.astype(F32) - t5_table[far_bucket].astype(F32)[None, :]).T
    by_rel = lambda rel: tab[:, _t5_bucket(rel)]
    d_pos = jnp.arange(t, dtype=I32)
    d_neg = jnp.arange(-t, 0, dtype=I32)
    d0 = _toeplitz(by_rel(-d_pos), by_rel(-d_neg), t, t)
    d1 = _toeplitz(by_rel(-d_pos - t), by_rel(-d_neg - t), t, t)
    r = np.arange(t)
    ok = (r[:, None] // CHUNK) <= (r[None, :] // CHUNK)
    d0 = jnp.where(jnp.asarray(ok)[None], d0 * LOG2E, NEG_INF)
    return d0, d1 * LOG2E


def _attn_c(qt, k, vt, lam, subln_g, t5_table, lam_init):
    nb, _, t = qt.shape
    s = nb * t
    d0, d1 = _t5_tiles_t(t5_table, t)
    return pl.pallas_call(
        functools.partial(_attn_c_kernel, out_scale=1.0 - lam_init),
        grid=(C_HEADS, nb),
        in_specs=[pl.BlockSpec(memory_space=pltpu.SMEM),
                  pl.BlockSpec((1, LANES, t), lambda h, i: (i, h, 0)),
                  pl.BlockSpec((s, LANES), lambda h, i: (0, h)),
                  pl.BlockSpec((nb, LANES, t), lambda h, i: (0, h, 0)),
                  pl.BlockSpec((1, t, t), lambda h, i: (h, 0, 0)),
                  pl.BlockSpec((1, t, t), lambda h, i: (h, 0, 0)),
                  _full((LANES, 1))],
        out_specs=pl.BlockSpec((t, LANES), lambda h, i: (i, h)),
        out_shape=jax.ShapeDtypeStruct((s, C_HEADS * 2 * C_HEAD_DIM), BF16),
        scratch_shapes=[pltpu.VMEM((LANES, 2 * t), BF16)] + _flash_scratch(2 * t),
        compiler_params=_params("arbitrary", "arbitrary"),
        name="attn_c",
    )(lam.reshape(1), qt, k, vt, d0, d1, subln_g.reshape(-1, 1))


def _prep_c_kernel(x_ref, wq_ref, wk_ref, wv_ref, qt_ref, k_ref, vt_ref):
    xb = x_ref[...].astype(BF16)
    qt = lax.dot_general(wq_ref[...], xb, NT_DIMS, preferred_element_type=F32)
    qt_ref[0] = (qt * (C_HEAD_DIM ** -0.5 * LOG2E)).astype(BF16)
    k_ref[...] = jnp.dot(xb, wk_ref[...], preferred_element_type=F32).astype(BF16)
    vt_ref[0] = lax.dot_general(wv_ref[...], xb, NT_DIMS, preferred_element_type=F32).astype(BF16)


def _prep_c(x, w_in):
    s = x.shape[0]
    t = ATT_TILE
    wq = w_in[:, :C_QK].T.astype(BF16)
    wk = w_in[:, C_QK:2 * C_QK].astype(BF16)
    wv = w_in[:, 2 * C_QK:].T.astype(BF16)
    row = pl.BlockSpec((t, D_MODEL), lambda i: (i, 0))
    tr = pl.BlockSpec((1, C_QK, t), lambda i: (i, 0, 0))
    sds = jax.ShapeDtypeStruct
    return pl.pallas_call(
        _prep_c_kernel,
        grid=(s // t,),
        in_specs=[row, _full(wq.shape), _full(wk.shape), _full(wv.shape)],
        out_specs=[tr, row, tr],
        out_shape=[sds((s // t, C_QK, t), BF16), sds((s, C_QK), BF16), sds((s // t, C_QK, t), BF16)],
        compiler_params=_params("arbitrary"),
        name="prep_c",
    )(x, wq, wk, wv)


def _post_attn_kernel(*refs, n_in):
    x_ref = refs[0]
    o_refs = refs[1:1 + n_in]
    w_refs = refs[1 + n_in:1 + 2 * n_in]
    g_ref, b_ref, out_ref = refs[1 + 2 * n_in:]
    y = DEEPNORM_ALPHA * x_ref[...]
    for o_ref, w_ref in zip(o_refs, w_refs):
        y = y + jnp.dot(o_ref[...], w_ref[...], preferred_element_type=F32)
    _store_row_tiles(out_ref, _layer_norm(y, g_ref[...], b_ref[...]))


def _post_attn(x, outs, weights, g, b):
    s = x.shape[0]
    t = ATT_TILE
    row = lambda w: pl.BlockSpec((t, w), lambda i: (i, 0))
    ws = [w.astype(BF16) for w in weights]
    return pl.pallas_call(
        functools.partial(_post_attn_kernel, n_in=len(outs)),
        grid=(s // t,),
        in_specs=[row(D_MODEL)] + [row(o.shape[1]) for o in outs] + [_full(w.shape) for w in ws]
        + [_full((1, D_MODEL))] * 2,
        out_specs=pl.BlockSpec((t, ROW_SUBLANES, LANES), lambda i: (i, 0, 0)),
        out_shape=jax.ShapeDtypeStruct((s, ROW_SUBLANES, LANES), F32),
        compiler_params=_params("arbitrary"),
        name="post_attn",
    )(x, *outs, *ws, g.reshape(1, -1), b.reshape(1, -1))


def _router_kernel(x_ref, wr_ref, br_ref, tri_ref, idx_ref, rank_ref, w_ref, cnt_ref, carry):
    step = pl.program_id(0)

    @pl.when(step == 0)
    def _():
        carry[...] = jnp.zeros(carry.shape, F32)

    t = ROUTER_TILE
    logits = lax.dot_general(wr_ref[...], _load_row_tiles(x_ref), NT_DIMS, preferred_element_type=F32,
                             precision=lax.Precision.HIGHEST) + br_ref[...]
    eid = lax.broadcasted_iota(I32, (N_EXPERTS, t), 0).astype(F32)
    work = logits
    vals, sels = [], []
    for _ in range(TOP_K):
        m = jnp.max(work, axis=0, keepdims=True)
        sel = jnp.min(jnp.where(work == m, eid, float(N_EXPERTS)), axis=0, keepdims=True)
        vals.append(m)
        sels.append(sel)
        work = jnp.where(eid == sel, -jnp.inf, work)
    exps = [jnp.exp(v - vals[0]) for v in vals]
    denom = exps[0] + exps[1] + exps[2] + exps[3]

    chosen = jnp.zeros((N_EXPERTS, t), F32)
    for sel in sels:
        chosen = chosen + (eid == sel).astype(F32)
    before = jnp.dot(chosen.astype(BF16), tri_ref[...], preferred_element_type=F32) + carry[...]
    for kk, sel in enumerate(sels):
        rank = jnp.sum(jnp.where(eid == sel, before, 0.0), axis=0, keepdims=True)
        rank_ref[kk:kk + 1, :] = rank.astype(I32)
        idx_ref[kk:kk + 1, :] = sel.astype(I32)
    carry[...] = carry[...] + jnp.sum(chosen, axis=1, keepdims=True)
    cnt_ref[...] = jnp.broadcast_to(carry[...], cnt_ref.shape)

    sub = lax.broadcasted_iota(I32, (LANES, t), 0)
    wrows = jnp.zeros((LANES, t), F32)
    for kk in range(TOP_K):
        wrows = jnp.where(sub == kk, exps[kk] / denom, wrows)
    w_ref[...] = wrows.T


def _router(x1, router_w, router_b):
    s = x1.shape[0]
    t = ROUTER_TILE
    tri = jnp.asarray(np.triu(np.ones((t, t), np.float32), k=1), BF16)
    tok = pl.BlockSpec((TOP_K, t), lambda i: (0, i))
    return pl.pallas_call(
        _router_kernel,
        grid=(s // t,),
        in_specs=[pl.BlockSpec((t, ROW_SUBLANES, LANES), lambda i: (i, 0, 0)),
                  _full((N_EXPERTS, D_MODEL)), _full((N_EXPERTS, 1)), _full((t, t))],
        out_specs=[tok, tok, pl.BlockSpec((t, LANES), lambda i: (i, 0)), _full((N_EXPERTS, LANES))],
        out_shape=[jax.ShapeDtypeStruct((TOP_K, s), I32), jax.ShapeDtypeStruct((TOP_K, s), I32),
                   jax.ShapeDtypeStruct((s, LANES), F32), jax.ShapeDtypeStruct((N_EXPERTS, LANES), F32)],
        scratch_shapes=[pltpu.VMEM((N_EXPERTS, 1), F32)],
        compiler_params=_params("arbitrary"),
        name="router",
    )(x1, router_w.T, router_b.reshape(-1, 1), tri)


def _dispatch_kernel(pad_ref, pos_hbm, x_ref, xs_hbm, pos_smem, zeros, pos_sem, row_sem, zero_sem):
    i = pl.program_id(0)
    t = DISPATCH_TILE
    n = t * TOP_K
    tm = EXPERT_TILE

    @pl.when(i == 0)
    def _():
        zeros[...] = jnp.zeros(zeros.shape, F32)

        def fill(e):
            row = pl.multiple_of(jnp.maximum(pad_ref[e], 0), tm)
            return pltpu.make_async_copy(zeros, xs_hbm.at[pl.ds(row, tm)], zero_sem)

        for e in range(2 * N_EXPERTS):
            pl.when(pad_ref[e] >= 0)(lambda e=e: fill(e).start())
        for e in range(2 * N_EXPERTS):
            pl.when(pad_ref[e] >= 0)(lambda e=e: fill(e).wait())

    cp = pltpu.make_async_copy(pos_hbm.at[pl.ds(pl.multiple_of(i * n, n), n)], pos_smem, pos_sem)
    cp.start()
    cp.wait()

    def body(g, carry):
        for j in range(ISSUE_UNROLL // TOP_K):
            tok = g * (ISSUE_UNROLL // TOP_K) + j
            for kk in range(TOP_K):
                pltpu.make_async_copy(x_ref.at[tok], xs_hbm.at[pos_smem[tok * TOP_K + kk]],
                                      row_sem).start(priority=kk % 2)
        return carry

    lax.fori_loop(0, n // ISSUE_UNROLL, body, 0)
    for kk in range(TOP_K):
        pltpu.make_async_copy(x_ref, xs_hbm.at[pl.ds(0, t)], row_sem).wait()


def _dispatch(pad_rows, pos_flat, x1, n_rows):
    s = x1.shape[0]
    t = DISPATCH_TILE
    any_spec = pl.BlockSpec(memory_space=pl.ANY)
    grid_spec = pltpu.PrefetchScalarGridSpec(
        num_scalar_prefetch=1,
        grid=(s // t,),
        in_specs=[any_spec, pl.BlockSpec((t, ROW_SUBLANES, LANES), lambda i, pad: (i, 0, 0))],
        out_specs=any_spec,
        scratch_shapes=[pltpu.SMEM((t * TOP_K,), I32), pltpu.VMEM((EXPERT_TILE, ROW_SUBLANES, LANES), F32),
                        pltpu.SemaphoreType.DMA, pltpu.SemaphoreType.DMA, pltpu.SemaphoreType.DMA],
    )
    return pl.pallas_call(
        _dispatch_kernel,
        grid_spec=grid_spec,
        out_shape=jax.ShapeDtypeStruct((n_rows, ROW_SUBLANES, LANES), F32),
        compiler_params=_params("arbitrary"),
        name="dispatch",
    )(pad_rows, pos_flat, x1)


def _gmm_kernel(te_ref, nt_ref, xs_ref, wi_ref, bi_ref, wo_ref, bo_ref, y_ref, wi_bf, wo_bf):
    i = pl.program_id(0)
    n_tiles = nt_ref[0]
    prev = te_ref[jnp.maximum(i - 1, 0)]
    fresh = jnp.logical_or(i == 0, te_ref[i] != prev)

    @pl.when(jnp.logical_and(i < n_tiles, fresh))
    def _():
        def cast(c, carry):
            rows = pl.ds(pl.multiple_of(c * CAST_ROWS, CAST_ROWS), CAST_ROWS)
            wi_bf[rows, :] = wi_ref[0, rows, :].astype(BF16)
            wo_bf[rows, :] = wo_ref[0, rows, :].astype(BF16)
            return carry

        lax.fori_loop(0, D_MODEL // CAST_ROWS, cast, 0)

    @pl.when(i < n_tiles)
    def _():
        xb = _load_row_tiles(xs_ref).astype(BF16)
        h = jnp.dot(xb, wi_bf[...], preferred_element_type=F32) + bi_ref[0]
        glu = jnp.minimum(h[:, :D_FF], SWIGLU_LIMIT)
        lin = jnp.clip(h[:, D_FF:], -SWIGLU_LIMIT, SWIGLU_LIMIT)
        act = glu * jax.nn.sigmoid(SWIGLU_ALPHA * glu) * (lin + 1.0)
        _store_row_tiles(y_ref, jnp.dot(act.astype(BF16), wo_bf[...], preferred_element_type=F32) + bo_ref[0])

    @pl.when(i >= n_tiles)
    def _():
        y_ref[...] = jnp.zeros(y_ref.shape, F32)


def _gmm(layer, tile_expert, n_tiles, xs, w_in, b_in, w_out, b_out):
    n_rows = xs.shape[0]
    tm = EXPERT_TILE
    last = lambda i, nt: jnp.maximum(jnp.minimum(i, nt[0] - 1), 0)
    expert = lambda i, te, nt: layer * N_EXPERTS + te[last(i, nt)]
    grid_spec = pltpu.PrefetchScalarGridSpec(
        num_scalar_prefetch=2,
        grid=(n_rows // tm,),
        in_specs=[pl.BlockSpec((tm, ROW_SUBLANES, LANES), lambda i, te, nt: (last(i, nt), 0, 0)),
                  pl.BlockSpec((1, D_MODEL, 2 * D_FF), lambda i, te, nt: (expert(i, te, nt), 0, 0)),
                  pl.BlockSpec((1, 1, 2 * D_FF), lambda i, te, nt: (expert(i, te, nt), 0, 0)),
                  pl.BlockSpec((1, D_FF, D_MODEL), lambda i, te, nt: (expert(i, te, nt), 0, 0)),
                  pl.BlockSpec((1, 1, D_MODEL), lambda i, te, nt: (expert(i, te, nt), 0, 0))],
        out_specs=pl.BlockSpec((tm, ROW_SUBLANES, LANES), lambda i, te, nt: (i, 0, 0)),
        scratch_shapes=[pltpu.VMEM((D_MODEL, 2 * D_FF), BF16), pltpu.VMEM((D_FF, D_MODEL), BF16)],
    )
    return pl.pallas_call(
        _gmm_kernel,
        grid_spec=grid_spec,
        out_shape=jax.ShapeDtypeStruct((n_rows, ROW_SUBLANES, LANES), F32),
        compiler_params=_params("arbitrary"),
        name="expert_mlp",
    )(tile_expert, n_tiles, xs, w_in, b_in, w_out, b_out)


def _combine_kernel(pos_hbm, y_hbm, x_ref, w_ref, p_ref, wg_ref, wp_ref, g_ref, b_ref, out_ref,
                    pos_smem, ybuf, pos_sem, row_sem):
    i = pl.program_id(0)
    t = COMBINE_TILE
    n = t * TOP_K

    def fetch_indices(step):
        base = pl.multiple_of((step & 1) * n, n)
        cp = pltpu.make_async_copy(pos_hbm.at[pl.ds(pl.multiple_of(step * n, n), n)],
                                   pos_smem.at[pl.ds(base, n)], pos_sem)
        cp.start()
        cp.wait()

    def row_copy(step, tok, kk):
        slot = step & 1
        return pltpu.make_async_copy(y_hbm.at[pos_smem[slot * n + tok * TOP_K + kk]], ybuf.at[slot, kk, tok],
                                     row_sem.at[slot])

    def wait_step(step):
        slot = step & 1
        for kk in range(TOP_K):
            pltpu.make_async_copy(y_hbm.at[pl.ds(0, t)], ybuf.at[slot, kk], row_sem.at[slot]).wait()

    @pl.when(i == 0)
    def _():
        fetch_indices(i)

        def issue(tok, carry):
            for kk in range(TOP_K):
                row_copy(i, tok, kk).start(priority=kk % 2)
            return carry

        lax.fori_loop(0, t, issue, 0, unroll=2)

    nxt = i + 1
    fetch_indices(nxt)
    x = _load_row_tiles(x_ref)
    pb = p_ref[...].astype(BF16)
    w = w_ref[...]
    for tok in range(t):
        for kk in range(TOP_K):
            row_copy(nxt, tok, kk).start(priority=kk % 2)
    gate = jax.nn.sigmoid(jnp.dot(x.astype(BF16), wg_ref[...], preferred_element_type=F32))
    proj = jnp.dot(pb, wp_ref[...], preferred_element_type=F32)
    y = DEEPNORM_ALPHA * x + gate * proj

    wait_step(i)
    for kk in range(TOP_K):
        y = y + w[:, kk:kk + 1] * _load_row_tiles(ybuf, i & 1, kk)
    out_ref[...] = _layer_norm(y, g_ref[...], b_ref[...])

    @pl.when(nxt == pl.num_programs(0))
    def _():
        wait_step(nxt)


def _combine(pos_flat, y, x1, wcol, p, gate_w, proj_w, g, b):
    s = x1.shape[0]
    t = COMBINE_TILE
    any_spec = pl.BlockSpec(memory_space=pl.ANY)
    row = lambda w: pl.BlockSpec((t, w), lambda i: (i, 0))
    return pl.pallas_call(
        _combine_kernel,
        grid=(s // t,),
        in_specs=[any_spec, any_spec, pl.BlockSpec((t, ROW_SUBLANES, LANES), lambda i: (i, 0, 0)),
                  row(LANES), row(PLE_DIM),
                  _full((D_MODEL, D_MODEL)), _full((PLE_DIM, D_MODEL)),
                  _full((1, D_MODEL)), _full((1, D_MODEL))],
        out_specs=row(D_MODEL),
        out_shape=jax.ShapeDtypeStruct((s, D_MODEL), F32),
        scratch_shapes=[pltpu.SMEM((2 * t * TOP_K,), I32),
                        pltpu.VMEM((2, TOP_K, t, ROW_SUBLANES, LANES), F32),
                        pltpu.SemaphoreType.DMA, pltpu.SemaphoreType.DMA((2,))],
        compiler_params=_params("arbitrary"),
        name="combine",
    )(pos_flat, y, x1, wcol, p, gate_w.astype(BF16), proj_w.astype(BF16), g.reshape(1, -1), b.reshape(1, -1))


def _moe_block(layer, x1, p, router_w, router_b, w_in, b_in, w_out, b_out, gate_w, proj_w, g, b):
    s = x1.shape[0]
    tm = EXPERT_TILE
    n_tiles_max = s * TOP_K // tm + N_EXPERTS
    idx, rank, wcol, cnt = _router(x1, router_w, router_b)
    counts = cnt[:, 0].astype(I32)
    tiles_per = (counts + tm - 1) // tm
    tile_end = jnp.cumsum(tiles_per)
    offsets = (tile_end - tiles_per) * tm
    experts = jnp.arange(N_EXPERTS, dtype=I32)
    pos = rank + jnp.sum(jnp.where(idx[..., None] == experts, offsets, 0), axis=-1)
    pos_flat = pos.T.reshape(-1)
    tile_ids = jnp.arange(n_tiles_max, dtype=I32)
    tile_expert = jnp.minimum(jnp.sum((tile_end[None, :] <= tile_ids[:, None]).astype(I32), axis=-1),
                              N_EXPERTS - 1)
    n_tiles = tile_end[-1:].astype(I32)
    last_tiles = jnp.where(tiles_per > 0, tile_end - 1, -1)
    tail_tiles = jnp.where(tile_end[-1] + experts < n_tiles_max, tile_end[-1] + experts, -1)
    fill_tiles = jnp.concatenate([last_tiles, tail_tiles])
    pad_rows = jnp.where(fill_tiles >= 0, fill_tiles * tm, -1).astype(I32)
    xs = _dispatch(pad_rows, pos_flat, x1, n_tiles_max * tm)
    y = _gmm(layer, tile_expert, n_tiles, xs, w_in, b_in, w_out, b_out)
    pos_ahead = jnp.concatenate([pos_flat, jnp.zeros((COMBINE_TILE * TOP_K,), I32)])
    return _combine(pos_ahead, y, x1, wcol, p, gate_w, proj_w, g, b)


def _lambda_init(layer_idx):
    return 0.8 - 0.6 * math.exp(-0.3 * layer_idx)


def kernel(x, p, ab_w_in, ab_rel_bias, ab_q_norm, ab_kv_norm, ab_w_uq, ab_w_ukv, ab_w_out, c_w_in, c_lambda, c_subln, c_w_out, t5_table, ln_mix_g, ln_mix_b, ln_ffn_g, ln_ffn_b, router_w, router_b, exp_w_in, exp_b_in, exp_w_out, exp_b_out, ple_gate_w, ple_proj_w):
    batch, s, _ = x.shape
    assert batch == 1 and s % DISPATCH_TILE == 0 and s % ATT_TILE == 0
    xr = x[0]
    w_in_all = exp_w_in.reshape(DEPTH * N_EXPERTS, D_MODEL, 2 * D_FF)
    b_in_all = exp_b_in.reshape(DEPTH * N_EXPERTS, 1, 2 * D_FF)
    w_out_all = exp_w_out.reshape(DEPTH * N_EXPERTS, D_FF, D_MODEL)
    b_out_all = exp_b_out.reshape(DEPTH * N_EXPERTS, 1, D_MODEL)
    for i in range(DEPTH):
        j = i // 2
        if i % 2 == 0:
            qa, ka, va, qbt, kb, vbt = _prep_ab(xr, ab_w_in[j], ab_q_norm[j], ab_kv_norm[j], ab_w_uq[j],
                                                ab_w_ukv[j])
            o_a = _attn_a(qa, ka, va, ab_rel_bias[j])
            o_b = _attn_b(qbt, kb, vbt)
            x1 = _post_attn(xr, [o_a, o_b], [ab_w_out[j][:A_WIDTH], ab_w_out[j][A_WIDTH:]],
                            ln_mix_g[i], ln_mix_b[i])
        else:
            qt, k, vt = _prep_c(xr, c_w_in[j])
            lp = c_lambda[j].astype(F32)
            lam_init = _lambda_init(i)
            lam = jnp.exp(jnp.sum(lp[0] * lp[1])) - jnp.exp(jnp.sum(lp[2] * lp[3])) + lam_init
            o_c = _attn_c(qt, k, vt, lam, c_subln[j], t5_table, lam_init)
            x1 = _post_attn(xr, [o_c], [c_w_out[j]], ln_mix_g[i], ln_mix_b[i])
        xr = _moe_block(i, x1, p[i, 0], router_w[i], router_b[i], w_in_all, b_in_all, w_out_all, b_out_all,
                        ple_gate_w[i], ple_proj_w[i], ln_ffn_g[i], ln_ffn_b[i])
    return xr[None]
```

```python
import functools
import math

import jax
import jax.numpy as jnp
import numpy as np
from jax import lax
from jax.experimental import pallas as pl
from jax.experimental.pallas import tpu as pltpu

F32 = jnp.float32
BF16 = jnp.bfloat16
I32 = jnp.int32

D_MODEL = 1024
DEPTH = 4
CHUNK = 64
PLE_DIM = 256

A_HEADS = 8
A_HEAD_DIM = 64
A_LEFT_CHUNKS = 8
A_REL_CLIP = 128
A_WIDTH = A_HEADS * A_HEAD_DIM

B_HEADS = 8
B_NOPE_DIM = 64
B_ROPE_DIM = 32
B_V_DIM = 64
B_Q_RANK = 512
B_KV_RANK = 256
ROPE_BASE = 10000.0

C_HEADS = 8
C_HEAD_DIM = 64
C_QK = 2 * C_HEADS * C_HEAD_DIM

T5_BUCKETS = 32
T5_MAX_DIST = 128

N_EXPERTS = 32
TOP_K = 4
D_FF = 1024
SWIGLU_ALPHA = 1.702
SWIGLU_LIMIT = 7.0

DEEPNORM_ALPHA = (2.0 * DEPTH) ** 0.25
LN_EPS = 1e-5
RMS_EPS = 1e-6
NEG_INF = -1e30

LANES = 128
ROW_SUBLANES = D_MODEL // LANES
VMEM_LIMIT = 56 * 1024 * 1024

ATT_TILE = 512
COL_CHUNK = 256
ROW_CHUNK = 64
SCORE_AHEAD = 2
SCORE_RING = 4
SUM_ROWS = 16
LOG2E = math.log2(math.e)
BAND_GROUP = 256
BAND_WINDOW = BAND_GROUP + A_LEFT_CHUNKS * CHUNK
ROUTER_TILE = 512
EXPERT_TILE = 512
DISPATCH_TILE = 512
COMBINE_TILE = 256
CAST_ROWS = 128
ISSUE_UNROLL = 8

NT_DIMS = (((1,), (1,)), ((), ()))


def _params(*sem):
    return pltpu.CompilerParams(dimension_semantics=sem, vmem_limit_bytes=VMEM_LIMIT)


def _full(shape):
    n = len(shape)
    return pl.BlockSpec(shape, lambda *_: (0,) * n)


def _layer_norm(y, g, b):
    mu = jnp.mean(y, axis=-1, keepdims=True)
    d = y - mu
    var = jnp.mean(d * d, axis=-1, keepdims=True)
    return d * lax.rsqrt(var + LN_EPS) * g + b


def _rms_norm(c, g):
    inv = lax.rsqrt(jnp.mean(c * c, axis=-1, keepdims=True) + RMS_EPS)
    return c * inv * g


def _load_row_tiles(ref, *lead):
    tiles = pltpu.einshape("ral->arl", ref[lead] if lead else ref[...])
    return jnp.concatenate([tiles[a] for a in range(ROW_SUBLANES)], axis=1)


def _store_row_tiles(ref, value):
    tiles = jnp.stack([value[:, a * LANES:(a + 1) * LANES] for a in range(ROW_SUBLANES)], axis=0)
    ref[...] = pltpu.einshape("arl->ral", tiles)


def _toeplitz(vals_pos, vals_neg, nr, nc):
    w = jnp.concatenate([vals_pos, vals_neg], axis=-1)
    lw = nc + nr
    assert w.shape[-1] == lw
    flat = jnp.tile(w, (1,) * (w.ndim - 1) + (nr,))[..., :nr * (lw - 1)]
    return flat.reshape(w.shape[:-1] + (nr, lw - 1))[..., :nc]


def _prep_ab_kernel(x_ref, wa_ref, wcq_ref, wckv_ref, wkr_ref, wkrr_ref, qn_ref, kvn_ref,
                    wuqa_ref, wuqb_ref, wukn_ref, wuv_ref, place_ref, ctab_ref, stab_ref,
                    cos_ref, sin_ref,
                    qa_ref, ka_ref, va_ref, qbt_ref, kb_ref, vbt_ref):
    xb = x_ref[...].astype(BF16)
    ha = jnp.dot(xb, wa_ref[...], preferred_element_type=F32)
    qa_ref[...] = (ha[:, :A_WIDTH] * (A_HEAD_DIM ** -0.5)).astype(BF16)
    ka_ref[...] = ha[:, A_WIDTH:2 * A_WIDTH].astype(BF16)
    va_ref[...] = ha[:, 2 * A_WIDTH:].astype(BF16)

    cq = jnp.dot(xb, wcq_ref[...], preferred_element_type=F32)
    cqn = _rms_norm(cq, qn_ref[...]).astype(BF16)
    qa_part = lax.dot_general(wuqa_ref[...], cqn, NT_DIMS, preferred_element_type=F32)
    qb_part = lax.dot_general(wuqb_ref[...], cqn, NT_DIMS, preferred_element_type=F32)
    ct = ctab_ref[...]
    st = stab_ref[...]
    scale = (B_NOPE_DIM + B_ROPE_DIM) ** -0.5 * LOG2E
    for h in range(B_HEADS):
        sl = slice(h * LANES, (h + 1) * LANES)
        qbt_ref[0, sl, :] = ((qa_part[sl, :] * ct + qb_part[sl, :] * st) * scale).astype(BF16)

    ckv = jnp.dot(xb, wckv_ref[...], preferred_element_type=F32)
    ckvn = _rms_norm(ckv, kvn_ref[...]).astype(BF16)
    kr = jnp.dot(xb, wkr_ref[...], preferred_element_type=F32)
    krr = jnp.dot(xb, wkrr_ref[...], preferred_element_type=F32)
    k_rope = (kr * cos_ref[...] + krr * sin_ref[...]).astype(BF16)
    kb = jnp.dot(ckvn, wukn_ref[...], preferred_element_type=F32)
    kb = kb + jnp.dot(k_rope, place_ref[...], preferred_element_type=F32)
    kb_ref[...] = kb.astype(BF16)
    vbt_ref[0] = lax.dot_general(wuv_ref[...], ckvn, NT_DIMS, preferred_element_type=F32).astype(BF16)


def _rot_half_cols(w):
    half = w.shape[-1] // 2
    return jnp.concatenate([-w[..., half:], w[..., :half]], axis=-1)


def _prep_ab(x, w_in, q_norm, kv_norm, w_uq, w_ukv):
    s = x.shape[0]
    t = ATT_TILE
    splits = [3 * A_WIDTH, 3 * A_WIDTH + B_Q_RANK, 3 * A_WIDTH + B_Q_RANK + B_KV_RANK]
    wa = w_in[:, :splits[0]].astype(BF16)
    wcq = w_in[:, splits[0]:splits[1]].astype(BF16)
    wckv = w_in[:, splits[1]:splits[2]].astype(BF16)
    wkr_f = w_in[:, splits[2]:]
    wkr = wkr_f.astype(BF16)
    wkrr = _rot_half_cols(wkr_f).astype(BF16)

    qd = B_NOPE_DIM + B_ROPE_DIM
    w_uq3 = w_uq.reshape(B_Q_RANK, B_HEADS, qd)
    zpad = jnp.zeros((B_Q_RANK, B_HEADS, LANES - qd), F32)
    wuqa = jnp.concatenate([w_uq3, zpad], axis=-1).reshape(B_Q_RANK, B_HEADS * LANES).T.astype(BF16)
    wuqb = jnp.concatenate([jnp.zeros((B_Q_RANK, B_HEADS, B_NOPE_DIM), F32),
                            _rot_half_cols(w_uq3[..., B_NOPE_DIM:]), zpad],
                           axis=-1).reshape(B_Q_RANK, B_HEADS * LANES).T.astype(BF16)
    w_ukv3 = w_ukv.reshape(B_KV_RANK, B_HEADS, B_NOPE_DIM + B_V_DIM)
    wukn = jnp.concatenate([w_ukv3[..., :B_NOPE_DIM],
                            jnp.zeros((B_KV_RANK, B_HEADS, LANES - B_NOPE_DIM), F32)],
                           axis=-1).reshape(B_KV_RANK, B_HEADS * LANES).astype(BF16)
    wuv = w_ukv3[..., B_NOPE_DIM:].reshape(B_KV_RANK, B_HEADS * B_V_DIM).T.astype(BF16)
    place = np.zeros((B_ROPE_DIM, B_HEADS * LANES), np.float32)
    for h in range(B_HEADS):
        for r in range(B_ROPE_DIM):
            place[r, h * LANES + B_NOPE_DIM + r] = 1.0
    place = jnp.asarray(place, BF16)

    pos = jnp.arange(s, dtype=F32)
    inv_freq = ROPE_BASE ** (-jnp.arange(0, B_ROPE_DIM, 2, dtype=F32) / B_ROPE_DIM)
    ang = pos[:, None] * inv_freq[None, :]
    cos = jnp.concatenate([jnp.cos(ang)] * 2, axis=-1)
    sin = jnp.concatenate([jnp.sin(ang)] * 2, axis=-1)
    ctab = jnp.concatenate([jnp.ones((s, B_NOPE_DIM), F32), cos, jnp.zeros((s, LANES - qd), F32)], axis=-1).T
    stab = jnp.concatenate([jnp.zeros((s, B_NOPE_DIM), F32), sin, jnp.zeros((s, LANES - qd), F32)], axis=-1).T

    row = lambda w: pl.BlockSpec((t, w), lambda i: (i, 0))
    col = pl.BlockSpec((LANES, t), lambda i: (0, i))
    tr = lambda w: pl.BlockSpec((1, w, t), lambda i: (i, 0, 0))
    consts = [wa, wcq, wckv, wkr, wkrr, q_norm.reshape(1, -1), kv_norm.reshape(1, -1),
              wuqa, wuqb, wukn, wuv, place]
    sds = jax.ShapeDtypeStruct
    return pl.pallas_call(
        _prep_ab_kernel,
        grid=(s // t,),
        in_specs=[row(D_MODEL)] + [_full(c.shape) for c in consts]
        + [col, col, row(B_ROPE_DIM), row(B_ROPE_DIM)],
        out_specs=[row(A_WIDTH), row(A_WIDTH), row(A_WIDTH),
                   tr(B_HEADS * LANES), row(B_HEADS * LANES), tr(B_HEADS * B_V_DIM)],
        out_shape=[sds((s, A_WIDTH), BF16)] * 3
        + [sds((s // t, B_HEADS * LANES, t), BF16), sds((s, B_HEADS * LANES), BF16),
           sds((s // t, B_HEADS * B_V_DIM, t), BF16)],
        compiler_params=_params("arbitrary"),
        name="prep_ab",
    )(x, *consts, ctab, stab, cos, sin)


def _attn_a_kernel(q_ref, kp_ref, kc_ref, vp_ref, vc_ref, bias_ref, o_ref, kbuf, vbuf):
    j = pl.program_id(1)
    t = ATT_TILE
    kbuf[0:t, :] = kp_ref[...]
    kbuf[t:2 * t, :] = kc_ref[...]
    vbuf[0:t, :] = vp_ref[...]
    vbuf[t:2 * t, :] = vc_ref[...]
    lo = lax.broadcasted_iota(I32, (1, LANES), 1) < A_HEAD_DIM
    col = lax.broadcasted_iota(I32, (1, BAND_WINDOW), 1)
    for g in range(t // BAND_GROUP):
        r0 = g * BAND_GROUP
        q = q_ref[r0:r0 + BAND_GROUP, :]
        k = kbuf[r0:r0 + BAND_WINDOW, :]
        v = vbuf[r0:r0 + BAND_WINDOW, :]
        valid = col + r0 >= jnp.where(j > 0, 0, t)
        outs = []
        for hh in range(2):
            qm = jnp.where(lo if hh == 0 else jnp.logical_not(lo), q, jnp.zeros_like(q))
            sc = lax.dot_general(qm, k, NT_DIMS, preferred_element_type=F32)
            sc = jnp.where(valid, sc + bias_ref[hh], NEG_INF)
            m = jnp.max(sc, axis=-1, keepdims=True)
            p = jnp.exp(sc - m)
            l = jnp.sum(p, axis=-1, keepdims=True)
            o = jnp.dot(p.astype(BF16), v, preferred_element_type=F32)
            outs.append(o / l)
        o_ref[r0:r0 + BAND_GROUP, :] = jnp.where(lo, outs[0], outs[1]).astype(BF16)


def _band_bias(rel_bias):
    qi = np.arange(BAND_GROUP)[:, None]
    kr = np.arange(BAND_WINDOW)[None, :]
    dchunk = (qi // CHUNK + A_LEFT_CHUNKS) - kr // CHUNK
    allowed = (dchunk >= 0) & (dchunk <= A_LEFT_CHUNKS)
    d_pos = np.arange(BAND_WINDOW)
    d_neg = np.arange(-BAND_GROUP, 0)
    to_idx = lambda d: np.clip(A_LEFT_CHUNKS * CHUNK - d, -A_REL_CLIP, A_REL_CLIP) + A_REL_CLIP
    tab = rel_bias.astype(F32).T
    table = _toeplitz(tab[:, to_idx(d_pos)], tab[:, to_idx(d_neg)], BAND_GROUP, BAND_WINDOW)
    return jnp.where(jnp.asarray(allowed)[None], table, NEG_INF)


def _attn_a(qa, ka, va, rel_bias):
    s = qa.shape[0]
    t = ATT_TILE
    bias = _band_bias(rel_bias)
    cur = pl.BlockSpec((t, LANES), lambda p, j: (j, p))
    prev = pl.BlockSpec((t, LANES), lambda p, j: (jnp.maximum(j - 1, 0), p))
    return pl.pallas_call(
        _attn_a_kernel,
        grid=(A_HEADS // 2, s // t),
        in_specs=[cur, prev, cur, prev, cur,
                  pl.BlockSpec((2, BAND_GROUP, BAND_WINDOW), lambda p, j: (p, 0, 0))],
        out_specs=cur,
        out_shape=jax.ShapeDtypeStruct((s, A_WIDTH), BF16),
        scratch_shapes=[pltpu.VMEM((2 * t, LANES), BF16), pltpu.VMEM((2 * t, LANES), BF16)],
        compiler_params=_params("arbitrary", "arbitrary"),
        name="attn_a",
    )(qa, ka, ka, va, va, bias)


class _FlashState:
    def __init__(self, m, l, acc, p0, p1, a0, a1, *s):
        self.m, self.l, self.acc = m, l, acc
        self.s, self.p, self.a = s, (p0, p1), (a0, a1)
        self.n_chunks = m.shape[1] // COL_CHUNK


def _flash_scratch(n_cols):
    t, w = ATT_TILE, COL_CHUNK
    return [pltpu.VMEM((1, n_cols), F32), pltpu.VMEM((1, n_cols), F32), pltpu.VMEM((LANES, n_cols), F32),
            pltpu.VMEM((t, w), BF16), pltpu.VMEM((t, w), BF16),
            pltpu.VMEM((1, w), F32), pltpu.VMEM((1, w), F32)] + [pltpu.VMEM((t, w), F32)] * SCORE_RING


def _flash_start(st, score_fn):
    st.m[...] = jnp.full(st.m.shape, NEG_INF, F32)
    st.l[...] = jnp.zeros(st.l.shape, F32)
    st.acc[...] = jnp.zeros(st.acc.shape, F32)
    st.p[1][...] = jnp.zeros(st.p[1].shape, BF16)
    st.a[1][...] = jnp.ones(st.a[1].shape, F32)
    for c in range(SCORE_AHEAD):
        st.s[c][...] = score_fn(0, c)


def _flash_values(st, vt_fn, kt, c, b):
    cs = slice(c * COL_CHUNK, (c + 1) * COL_CHUNK)
    lhs = jnp.concatenate([vt_fn(kt), jnp.ones((SUM_ROWS, ATT_TILE), BF16)], axis=0)
    pv = jnp.dot(lhs, st.p[b][...], preferred_element_type=F32)
    alpha = st.a[b][...]
    st.acc[:, cs] = alpha * st.acc[:, cs] + pv[:LANES]
    st.l[:, cs] = alpha * st.l[:, cs] + pv[LANES:LANES + 1]


def _flash_region(st, score_fn, vt_fn, tiles, has_next):
    units = [(ti, c) for ti in range(len(tiles)) for c in range(st.n_chunks)]
    assert len(units) % SCORE_RING == 0 and SCORE_AHEAD <= st.n_chunks
    for u, (ti, c) in enumerate(units):
        b = u % 2
        kt, bias_fn = tiles[ti]
        ahead = u + SCORE_AHEAD
        if ahead < len(units):
            nti, nc = units[ahead]
            st.s[ahead % SCORE_RING][...] = score_fn(tiles[nti][0], nc)
        elif has_next:
            st.s[ahead % SCORE_RING][...] = score_fn(tiles[-1][0] + 1, ahead - len(units))
        if u > 0:
            pti, pc = units[u - 1]
            _flash_values(st, vt_fn, tiles[pti][0], pc, 1 - b)
        else:
            _flash_values(st, vt_fn, jnp.maximum(kt - 1, 0), st.n_chunks - 1, 1 - b)
        cs = slice(c * COL_CHUNK, (c + 1) * COL_CHUNK)
        c0 = (c * COL_CHUNK) % ATT_TILE

        def rows(r, s_ref=st.s[u % SCORE_RING], bias_fn=bias_fn, c0=c0):
            blk = s_ref[r * ROW_CHUNK:(r + 1) * ROW_CHUNK, :]
            if bias_fn is not None:
                blk = blk + bias_fn(c0, r)
            return blk

        n_rows = ATT_TILE // ROW_CHUNK
        part = rows(0)
        for r in range(1, n_rows):
            part = jnp.maximum(part, rows(r))
        m_prev = st.m[:, cs]
        m_new = jnp.maximum(m_prev, jnp.max(part, axis=0, keepdims=True))
        st.a[b][...] = jnp.exp2(m_prev - m_new)
        st.m[:, cs] = m_new
        for r in range(n_rows):
            st.p[b][r * ROW_CHUNK:(r + 1) * ROW_CHUNK, :] = jnp.exp2(rows(r) - m_new).astype(BF16)


def _flash_sweep(st, score_fn, vt_fn, n_plain, tail_regions, last_tile):
    _flash_start(st, score_fn)

    def quad(j, carry):
        _flash_region(st, score_fn, vt_fn, [(4 * j + d, None) for d in range(4)], True)
        return carry

    n_quads = lax.shift_right_logical(n_plain, 2)
    lax.fori_loop(0, n_quads, quad, 0)
    done = n_quads * 4

    @pl.when((n_plain & 2) == 2)
    def _():
        _flash_region(st, score_fn, vt_fn, [(done, None), (done + 1, None)], True)

    @pl.when((n_plain & 1) == 1)
    def _():
        _flash_region(st, score_fn, vt_fn, [(n_plain - 1, None)], True)

    tail_regions()
    _flash_values(st, vt_fn, last_tile, st.n_chunks - 1, 1)


def _chunk_causal_bias_t(t):
    r = np.arange(t)
    ok = (r[:, None] // CHUNK) <= (r[None, :] // CHUNK)
    return jnp.asarray(np.where(ok, 0.0, NEG_INF), F32)


def _attn_b_kernel(qt_ref, k_ref, vt_ref, diag_ref, o_ref, *scratch):
    i = pl.program_id(1)
    t = ATT_TILE
    st = _FlashState(*scratch)
    per_head = t // COL_CHUNK

    def score(kt, c):
        hh = c // per_head
        c0 = (c % per_head) * COL_CHUNK
        k = k_ref[pl.ds(pl.multiple_of(kt * t, t), t), hh * LANES:(hh + 1) * LANES]
        return jnp.dot(k, qt_ref[0, hh * LANES:(hh + 1) * LANES, c0:c0 + COL_CHUNK],
                       preferred_element_type=F32)

    vt_fn = lambda kt: vt_ref[kt]

    def tail():
        diag = lambda c0, r: diag_ref[r * ROW_CHUNK:(r + 1) * ROW_CHUNK, c0:c0 + COL_CHUNK]
        _flash_region(st, score, vt_fn, [(i, diag)], False)

    _flash_sweep(st, score, vt_fn, i, tail, i)
    o2 = st.acc[...] / st.l[...]
    ot = jnp.concatenate([o2[:B_V_DIM, :t], o2[B_V_DIM:, t:]], axis=0)
    o_ref[...] = ot.T.astype(BF16)


def _attn_b(qbt, kb, vbt):
    nb, _, t = qbt.shape
    s = nb * t
    return pl.pallas_call(
        _attn_b_kernel,
        grid=(B_HEADS // 2, nb),
        in_specs=[pl.BlockSpec((1, 2 * LANES, t), lambda p, i: (i, p, 0)),
                  pl.BlockSpec((s, 2 * LANES), lambda p, i: (0, p)),
                  pl.BlockSpec((nb, LANES, t), lambda p, i: (0, p, 0)),
                  _full((t, t))],
        out_specs=pl.BlockSpec((t, LANES), lambda p, i: (i, p)),
        out_shape=jax.ShapeDtypeStruct((s, B_HEADS * B_V_DIM), BF16),
        scratch_shapes=_flash_scratch(2 * t),
        compiler_params=_params("arbitrary", "arbitrary"),
        name="attn_b",
    )(qbt, kb, vbt, _chunk_causal_bias_t(t))


def _attn_c_kernel(lam_ref, qt_ref, k_ref, vt_ref, d0_ref, d1_ref, g_ref, o_ref, q2_ref, *scratch, out_scale):
    i = pl.program_id(1)
    t = ATT_TILE
    st = _FlashState(*scratch)
    d = C_HEAD_DIM
    zero = jnp.zeros((d, t), BF16)
    q2_ref[:d, :t] = qt_ref[0, :d, :]
    q2_ref[d:, :t] = zero
    q2_ref[:d, t:] = zero
    q2_ref[d:, t:] = qt_ref[0, d:, :]

    def score(kt, c):
        return jnp.dot(k_ref[pl.ds(pl.multiple_of(kt * t, t), t), :],
                       q2_ref[:, c * COL_CHUNK:(c + 1) * COL_CHUNK], preferred_element_type=F32)

    vt_fn = lambda kt: vt_ref[kt]

    def tail():
        tile_of = lambda ref: lambda c0, r: ref[0, r * ROW_CHUNK:(r + 1) * ROW_CHUNK, c0:c0 + COL_CHUNK]
        diag = (i, tile_of(d0_ref))

        @pl.when(i > 0)
        def _():
            below = (i - 1, tile_of(d1_ref))
            _flash_region(st, score, vt_fn, [below, diag], False)

        @pl.when(i == 0)
        def _():
            _flash_region(st, score, vt_fn, [diag], False)

    _flash_sweep(st, score, vt_fn, jnp.maximum(i - 1, 0), tail, i)

    o2 = st.acc[...] / st.l[...]
    ot = o2[:, :t] - lam_ref[0] * o2[:, t:]
    inv = lax.rsqrt(jnp.mean(ot * ot, axis=0, keepdims=True) + RMS_EPS)
    o_ref[...] = (ot * inv * g_ref[...] * out_scale).T.astype(BF16)


def _t5_bucket(rel):
    half = T5_BUCKETS // 2
    max_exact = half // 2
    ret = jnp.where(rel > 0, half, 0)
    n = jnp.abs(rel)
    large = max_exact + (jnp.log(jnp.maximum(n, max_exact).astype(F32) / max_exact)
                         / math.log(T5_MAX_DIST / max_exact) * (half - max_exact)).astype(I32)
    large = jnp.minimum(large, half - 1)
    return ret + jnp.where(n < max_exact, n, large)


def _t5_tiles_t(t5_table, t):
    assert t >= T5_MAX_DIST
    far_bucket = T5_BUCKETS // 2 - 1
    tab = (t5_table.astype(F32) - t5_table[far_bucket].astype(F32)[None, :]).T
    by_rel = lambda rel: tab[:, _t5_bucket(rel)]
    d_pos = jnp.arange(t, dtype=I32)
    d_neg = jnp.arange(-t, 0, dtype=I32)
    d0 = _toeplitz(by_rel(-d_pos), by_rel(-d_neg), t, t)
    d1 = _toeplitz(by_rel(-d_pos - t), by_rel(-d_neg - t), t, t)
    r = np.arange(t)
    ok = (r[:, None] // CHUNK) <= (r[None, :] // CHUNK)
    d0 = jnp.where(jnp.asarray(ok)[None], d0 * LOG2E, NEG_INF)
    return d0, d1 * LOG2E


def _attn_c(qt, k, vt, lam, subln_g, t5_table, lam_init):
    nb, _, t = qt.shape
    s = nb * t
    d0, d1 = _t5_tiles_t(t5_table, t)
    return pl.pallas_call(
        functools.partial(_attn_c_kernel, out_scale=1.0 - lam_init),
        grid=(C_HEADS, nb),
        in_specs=[pl.BlockSpec(memory_space=pltpu.SMEM),
                  pl.BlockSpec((1, LANES, t), lambda h, i: (i, h, 0)),
                  pl.BlockSpec((s, LANES), lambda h, i: (0, h)),
                  pl.BlockSpec((nb, LANES, t), lambda h, i: (0, h, 0)),
                  pl.BlockSpec((1, t, t), lambda h, i: (h, 0, 0)),
                  pl.BlockSpec((1, t, t), lambda h, i: (h, 0, 0)),
                  _full((LANES, 1))],
        out_specs=pl.BlockSpec((t, LANES), lambda h, i: (i, h)),
        out_shape=jax.ShapeDtypeStruct((s, C_HEADS * 2 * C_HEAD_DIM), BF16),
        scratch_shapes=[pltpu.VMEM((LANES, 2 * t), BF16)] + _flash_scratch(2 * t),
        compiler_params=_params("arbitrary", "arbitrary"),
        name="attn_c",
    )(lam.reshape(1), qt, k, vt, d0, d1, subln_g.reshape(-1, 1))


def _prep_c_kernel(x_ref, wq_ref, wk_ref, wv_ref, qt_ref, k_ref, vt_ref):
    xb = x_ref[...].astype(BF16)
    qt = lax.dot_general(wq_ref[...], xb, NT_DIMS, preferred_element_type=F32)
    qt_ref[0] = (qt * (C_HEAD_DIM ** -0.5 * LOG2E)).astype(BF16)
    k_ref[...] = jnp.dot(xb, wk_ref[...], preferred_element_type=F32).astype(BF16)
    vt_ref[0] = lax.dot_general(wv_ref[...], xb, NT_DIMS, preferred_element_type=F32).astype(BF16)


def _prep_c(x, w_in):
    s = x.shape[0]
    t = ATT_TILE
    wq = w_in[:, :C_QK].T.astype(BF16)
    wk = w_in[:, C_QK:2 * C_QK].astype(BF16)
    wv = w_in[:, 2 * C_QK:].T.astype(BF16)
    row = pl.BlockSpec((t, D_MODEL), lambda i: (i, 0))
    tr = pl.BlockSpec((1, C_QK, t), lambda i: (i, 0, 0))
    sds = jax.ShapeDtypeStruct
    return pl.pallas_call(
        _prep_c_kernel,
        grid=(s // t,),
        in_specs=[row, _full(wq.shape), _full(wk.shape), _full(wv.shape)],
        out_specs=[tr, row, tr],
        out_shape=[sds((s // t, C_QK, t), BF16), sds((s, C_QK), BF16), sds((s // t, C_QK, t), BF16)],
        compiler_params=_params("arbitrary"),
        name="prep_c",
    )(x, wq, wk, wv)


def _post_attn_kernel(*refs, n_in):
    x_ref = refs[0]
    o_refs = refs[1:1 + n_in]
    w_refs = refs[1 + n_in:1 + 2 * n_in]
    g_ref, b_ref, out_ref = refs[1 + 2 * n_in:]
    y = DEEPNORM_ALPHA * x_ref[...]
    for o_ref, w_ref in zip(o_refs, w_refs):
        y = y + jnp.dot(o_ref[...], w_ref[...], preferred_element_type=F32)
    _store_row_tiles(out_ref, _layer_norm(y, g_ref[...], b_ref[...]))


def _post_attn(x, outs, weights, g, b):
    s = x.shape[0]
    t = ATT_TILE
    row = lambda w: pl.BlockSpec((t, w), lambda i: (i, 0))
    ws = [w.astype(BF16) for w in weights]
    return pl.pallas_call(
        functools.partial(_post_attn_kernel, n_in=len(outs)),
        grid=(s // t,),
        in_specs=[row(D_MODEL)] + [row(o.shape[1]) for o in outs] + [_full(w.shape) for w in ws]
        + [_full((1, D_MODEL))] * 2,
        out_specs=pl.BlockSpec((t, ROW_SUBLANES, LANES), lambda i: (i, 0, 0)),
        out_shape=jax.ShapeDtypeStruct((s, ROW_SUBLANES, LANES), F32),
        compiler_params=_params("arbitrary"),
        name="post_attn",
    )(x, *outs, *ws, g.reshape(1, -1), b.reshape(1, -1))


def _router_kernel(x_ref, wr_ref, br_ref, tri_ref, idx_ref, rank_ref, w_ref, cnt_ref, carry):
    step = pl.program_id(0)

    @pl.when(step == 0)
    def _():
        carry[...] = jnp.zeros(carry.shape, F32)

    t = ROUTER_TILE
    logits = lax.dot_general(wr_ref[...], _load_row_tiles(x_ref), NT_DIMS, preferred_element_type=F32,
                             precision=lax.Precision.HIGHEST) + br_ref[...]
    eid = lax.broadcasted_iota(I32, (N_EXPERTS, t), 0).astype(F32)
    work = logits
    vals, sels = [], []
    for _ in range(TOP_K):
        m = jnp.max(work, axis=0, keepdims=True)
        sel = jnp.min(jnp.where(work == m, eid, float(N_EXPERTS)), axis=0, keepdims=True)
        vals.append(m)
        sels.append(sel)
        work = jnp.where(eid == sel, -jnp.inf, work)
    exps = [jnp.exp(v - vals[0]) for v in vals]
    denom = exps[0] + exps[1] + exps[2] + exps[3]

    chosen = jnp.zeros((N_EXPERTS, t), F32)
    for sel in sels:
        chosen = chosen + (eid == sel).astype(F32)
    before = jnp.dot(chosen.astype(BF16), tri_ref[...], preferred_element_type=F32) + carry[...]
    for kk, sel in enumerate(sels):
        rank = jnp.sum(jnp.where(eid == sel, before, 0.0), axis=0, keepdims=True)
        rank_ref[kk:kk + 1, :] = rank.astype(I32)
        idx_ref[kk:kk + 1, :] = sel.astype(I32)
    carry[...] = carry[...] + jnp.sum(chosen, axis=1, keepdims=True)
    cnt_ref[...] = jnp.broadcast_to(carry[...], cnt_ref.shape)

    sub = lax.broadcasted_iota(I32, (LANES, t), 0)
    wrows = jnp.zeros((LANES, t), F32)
    for kk in range(TOP_K):
        wrows = jnp.where(sub == kk, exps[kk] / denom, wrows)
    w_ref[...] = wrows.T


def _router(x1, router_w, router_b):
    s = x1.shape[0]
    t = ROUTER_TILE
    tri = jnp.asarray(np.triu(np.ones((t, t), np.float32), k=1), BF16)
    tok = pl.BlockSpec((TOP_K, t), lambda i: (0, i))
    return pl.pallas_call(
        _router_kernel,
        grid=(s // t,),
        in_specs=[pl.BlockSpec((t, ROW_SUBLANES, LANES), lambda i: (i, 0, 0)),
                  _full((N_EXPERTS, D_MODEL)), _full((N_EXPERTS, 1)), _full((t, t))],
        out_specs=[tok, tok, pl.BlockSpec((t, LANES), lambda i: (i, 0)), _full((N_EXPERTS, LANES))],
        out_shape=[jax.ShapeDtypeStruct((TOP_K, s), I32), jax.ShapeDtypeStruct((TOP_K, s), I32),
                   jax.ShapeDtypeStruct((s, LANES), F32), jax.ShapeDtypeStruct((N_EXPERTS, LANES), F32)],
        scratch_shapes=[pltpu.VMEM((N_EXPERTS, 1), F32)],
        compiler_params=_params("arbitrary"),
        name="router",
    )(x1, router_w.T, router_b.reshape(-1, 1), tri)


def _dispatch_kernel(pad_ref, pos_hbm, x_ref, xs_hbm, pos_smem, zeros, pos_sem, row_sem, zero_sem):
    i = pl.program_id(0)
    t = DISPATCH_TILE
    n = t * TOP_K
    tm = EXPERT_TILE

    @pl.when(i == 0)
    def _():
        zeros[...] = jnp.zeros(zeros.shape, F32)

        def fill(e):
            row = pl.multiple_of(jnp.maximum(pad_ref[e], 0), tm)
            return pltpu.make_async_copy(zeros, xs_hbm.at[pl.ds(row, tm)], zero_sem)

        for e in range(2 * N_EXPERTS):
            pl.when(pad_ref[e] >= 0)(lambda e=e: fill(e).start())
        for e in range(2 * N_EXPERTS):
            pl.when(pad_ref[e] >= 0)(lambda e=e: fill(e).wait())

    cp = pltpu.make_async_copy(pos_hbm.at[pl.ds(pl.multiple_of(i * n, n), n)], pos_smem, pos_sem)
    cp.start()
    cp.wait()

    def body(g, carry):
        for j in range(ISSUE_UNROLL // TOP_K):
            tok = g * (ISSUE_UNROLL // TOP_K) + j
            for kk in range(TOP_K):
                pltpu.make_async_copy(x_ref.at[tok], xs_hbm.at[pos_smem[tok * TOP_K + kk]],
                                      row_sem).start(priority=kk % 2)
        return carry

    lax.fori_loop(0, n // ISSUE_UNROLL, body, 0)
    for kk in range(TOP_K):
        pltpu.make_async_copy(x_ref, xs_hbm.at[pl.ds(0, t)], row_sem).wait()


def _dispatch(pad_rows, pos_flat, x1, n_rows):
    s = x1.shape[0]
    t = DISPATCH_TILE
    any_spec = pl.BlockSpec(memory_space=pl.ANY)
    grid_spec = pltpu.PrefetchScalarGridSpec(
        num_scalar_prefetch=1,
        grid=(s // t,),
        in_specs=[any_spec, pl.BlockSpec((t, ROW_SUBLANES, LANES), lambda i, pad: (i, 0, 0))],
        out_specs=any_spec,
        scratch_shapes=[pltpu.SMEM((t * TOP_K,), I32), pltpu.VMEM((EXPERT_TILE, ROW_SUBLANES, LANES), F32),
                        pltpu.SemaphoreType.DMA, pltpu.SemaphoreType.DMA, pltpu.SemaphoreType.DMA],
    )
    return pl.pallas_call(
        _dispatch_kernel,
        grid_spec=grid_spec,
        out_shape=jax.ShapeDtypeStruct((n_rows, ROW_SUBLANES, LANES), F32),
        compiler_params=_params("arbitrary"),
        name="dispatch",
    )(pad_rows, pos_flat, x1)


def _gmm_kernel(te_ref, nt_ref, xs_ref, wi_ref, bi_ref, wo_ref, bo_ref, y_ref, wi_bf, wo_bf):
    i = pl.program_id(0)
    n_tiles = nt_ref[0]
    prev = te_ref[jnp.maximum(i - 1, 0)]
    fresh = jnp.logical_or(i == 0, te_ref[i] != prev)

    @pl.when(jnp.logical_and(i < n_tiles, fresh))
    def _():
        def cast(c, carry):
            rows = pl.ds(pl.multiple_of(c * CAST_ROWS, CAST_ROWS), CAST_ROWS)
            wi_bf[rows, :] = wi_ref[0, rows, :].astype(BF16)
            wo_bf[rows, :] = wo_ref[0, rows, :].astype(BF16)
            return carry

        lax.fori_loop(0, D_MODEL // CAST_ROWS, cast, 0)

    @pl.when(i < n_tiles)
    def _():
        xb = _load_row_tiles(xs_ref).astype(BF16)
        h = jnp.dot(xb, wi_bf[...], preferred_element_type=F32) + bi_ref[0]
        glu = jnp.minimum(h[:, :D_FF], SWIGLU_LIMIT)
        lin = jnp.clip(h[:, D_FF:], -SWIGLU_LIMIT, SWIGLU_LIMIT)
        act = glu * jax.nn.sigmoid(SWIGLU_ALPHA * glu) * (lin + 1.0)
        _store_row_tiles(y_ref, jnp.dot(act.astype(BF16), wo_bf[...], preferred_element_type=F32) + bo_ref[0])

    @pl.when(i >= n_tiles)
    def _():
        y_ref[...] = jnp.zeros(y_ref.shape, F32)


def _gmm(layer, tile_expert, n_tiles, xs, w_in, b_in, w_out, b_out):
    n_rows = xs.shape[0]
    tm = EXPERT_TILE
    last = lambda i, nt: jnp.maximum(jnp.minimum(i, nt[0] - 1), 0)
    expert = lambda i, te, nt: layer * N_EXPERTS + te[last(i, nt)]
    grid_spec = pltpu.PrefetchScalarGridSpec(
        num_scalar_prefetch=2,
        grid=(n_rows // tm,),
        in_specs=[pl.BlockSpec((tm, ROW_SUBLANES, LANES), lambda i, te, nt: (last(i, nt), 0, 0)),
                  pl.BlockSpec((1, D_MODEL, 2 * D_FF), lambda i, te, nt: (expert(i, te, nt), 0, 0)),
                  pl.BlockSpec((1, 1, 2 * D_FF), lambda i, te, nt: (expert(i, te, nt), 0, 0)),
                  pl.BlockSpec((1, D_FF, D_MODEL), lambda i, te, nt: (expert(i, te, nt), 0, 0)),
                  pl.BlockSpec((1, 1, D_MODEL), lambda i, te, nt: (expert(i, te, nt), 0, 0))],
        out_specs=pl.BlockSpec((tm, ROW_SUBLANES, LANES), lambda i, te, nt: (i, 0, 0)),
        scratch_shapes=[pltpu.VMEM((D_MODEL, 2 * D_FF), BF16), pltpu.VMEM((D_FF, D_MODEL), BF16)],
    )
    return pl.pallas_call(
        _gmm_kernel,
        grid_spec=grid_spec,
        out_shape=jax.ShapeDtypeStruct((n_rows, ROW_SUBLANES, LANES), F32),
        compiler_params=_params("arbitrary"),
        name="expert_mlp",
    )(tile_expert, n_tiles, xs, w_in, b_in, w_out, b_out)


def _combine_kernel(pos_hbm, y_hbm, x_ref, w_ref, p_ref, wg_ref, wp_ref, g_ref, b_ref, out_ref,
                    pos_smem, ybuf, pos_sem, row_sem):
    i = pl.program_id(0)
    t = COMBINE_TILE
    n = t * TOP_K

    def fetch_indices(step):
        base = pl.multiple_of((step & 1) * n, n)
        cp = pltpu.make_async_copy(pos_hbm.at[pl.ds(pl.multiple_of(step * n, n), n)],
                                   pos_smem.at[pl.ds(base, n)], pos_sem)
        cp.start()
        cp.wait()

    def row_copy(step, tok, kk):
        slot = step & 1
        return pltpu.make_async_copy(y_hbm.at[pos_smem[slot * n + tok * TOP_K + kk]], ybuf.at[slot, kk, tok],
                                     row_sem.at[slot])

    def wait_step(step):
        slot = step & 1
        for kk in range(TOP_K):
            pltpu.make_async_copy(y_hbm.at[pl.ds(0, t)], ybuf.at[slot, kk], row_sem.at[slot]).wait()

    @pl.when(i == 0)
    def _():
        fetch_indices(i)

        def issue(tok, carry):
            for kk in range(TOP_K):
                row_copy(i, tok, kk).start(priority=kk % 2)
            return carry

        lax.fori_loop(0, t, issue, 0, unroll=2)

    nxt = i + 1
    fetch_indices(nxt)
    x = _load_row_tiles(x_ref)
    pb = p_ref[...].astype(BF16)
    w = w_ref[...]
    for tok in range(t):
        for kk in range(TOP_K):
            row_copy(nxt, tok, kk).start(priority=kk % 2)
    gate = jax.nn.sigmoid(jnp.dot(x.astype(BF16), wg_ref[...], preferred_element_type=F32))
    proj = jnp.dot(pb, wp_ref[...], preferred_element_type=F32)
    y = DEEPNORM_ALPHA * x + gate * proj

    wait_step(i)
    for kk in range(TOP_K):
        y = y + w[:, kk:kk + 1] * _load_row_tiles(ybuf, i & 1, kk)
    out_ref[...] = _layer_norm(y, g_ref[...], b_ref[...])

    @pl.when(nxt == pl.num_programs(0))
    def _():
        wait_step(nxt)


def _combine(pos_flat, y, x1, wcol, p, gate_w, proj_w, g, b):
    s = x1.shape[0]
    t = COMBINE_TILE
    any_spec = pl.BlockSpec(memory_space=pl.ANY)
    row = lambda w: pl.BlockSpec((t, w), lambda i: (i, 0))
    return pl.pallas_call(
        _combine_kernel,
        grid=(s // t,),
        in_specs=[any_spec, any_spec, pl.BlockSpec((t, ROW_SUBLANES, LANES), lambda i: (i, 0, 0)),
                  row(LANES), row(PLE_DIM),
                  _full((D_MODEL, D_MODEL)), _full((PLE_DIM, D_MODEL)),
                  _full((1, D_MODEL)), _full((1, D_MODEL))],
        out_specs=row(D_MODEL),
        out_shape=jax.ShapeDtypeStruct((s, D_MODEL), F32),
        scratch_shapes=[pltpu.SMEM((2 * t * TOP_K,), I32),
                        pltpu.VMEM((2, TOP_K, t, ROW_SUBLANES, LANES), F32),
                        pltpu.SemaphoreType.DMA, pltpu.SemaphoreType.DMA((2,))],
        compiler_params=_params("arbitrary"),
        name="combine",
    )(pos_flat, y, x1, wcol, p, gate_w.astype(BF16), proj_w.astype(BF16), g.reshape(1, -1), b.reshape(1, -1))


def _moe_block(layer, x1, p, router_w, router_b, w_in, b_in, w_out, b_out, gate_w, proj_w, g, b):
    s = x1.shape[0]
    tm = EXPERT_TILE
    n_tiles_max = s * TOP_K // tm + N_EXPERTS
    idx, rank, wcol, cnt = _router(x1, router_w, router_b)
    counts = cnt[:, 0].astype(I32)
    tiles_per = (counts + tm - 1) // tm
    tile_end = jnp.cumsum(tiles_per)
    offsets = (tile_end - tiles_per) * tm
    experts = jnp.arange(N_EXPERTS, dtype=I32)
    pos = rank + jnp.sum(jnp.where(idx[..., None] == experts, offsets, 0), axis=-1)
    pos_flat = pos.T.reshape(-1)
    tile_ids = jnp.arange(n_tiles_max, dtype=I32)
    tile_expert = jnp.minimum(jnp.sum((tile_end[None, :] <= tile_ids[:, None]).astype(I32), axis=-1),
                              N_EXPERTS - 1)
    n_tiles = tile_end[-1:].astype(I32)
    last_tiles = jnp.where(tiles_per > 0, tile_end - 1, -1)
    tail_tiles = jnp.where(tile_end[-1] + experts < n_tiles_max, tile_end[-1] + experts, -1)
    fill_tiles = jnp.concatenate([last_tiles, tail_tiles])
    pad_rows = jnp.where(fill_tiles >= 0, fill_tiles * tm, -1).astype(I32)
    xs = _dispatch(pad_rows, pos_flat, x1, n_tiles_max * tm)
    y = _gmm(layer, tile_expert, n_tiles, xs, w_in, b_in, w_out, b_out)
    pos_ahead = jnp.concatenate([pos_flat, jnp.zeros((COMBINE_TILE * TOP_K,), I32)])
    return _combine(pos_ahead, y, x1, wcol, p, gate_w, proj_w, g, b)


def _lambda_init(layer_idx):
    return 0.8 - 0.6 * math.exp(-0.3 * layer_idx)


def kernel(x, p, ab_w_in, ab_rel_bias, ab_q_norm, ab_kv_norm, ab_w_uq, ab_w_ukv, ab_w_out, c_w_in, c_lambda, c_subln, c_w_out, t5_table, ln_mix_g, ln_mix_b, ln_ffn_g, ln_ffn_b, router_w, router_b, exp_w_in, exp_b_in, exp_w_out, exp_b_out, ple_gate_w, ple_proj_w):
    batch, s, _ = x.shape
    assert batch == 1 and s % DISPATCH_TILE == 0 and s % ATT_TILE == 0
    xr = x[0]
    w_in_all = exp_w_in.reshape(DEPTH * N_EXPERTS, D_MODEL, 2 * D_FF)
    b_in_all = exp_b_in.reshape(DEPTH * N_EXPERTS, 1, 2 * D_FF)
    w_out_all = exp_w_out.reshape(DEPTH * N_EXPERTS, D_FF, D_MODEL)
    b_out_all = exp_b_out.reshape(DEPTH * N_EXPERTS, 1, D_MODEL)
    for i in range(DEPTH):
        j = i // 2
        if i % 2 == 0:
            qa, ka, va, qbt, kb, vbt = _prep_ab(xr, ab_w_in[j], ab_q_norm[j], ab_kv_norm[j], ab_w_uq[j],
                                                ab_w_ukv[j])
            o_a = _attn_a(qa, ka, va, ab_rel_bias[j])
            o_b = _attn_b(qbt, kb, vbt)
            x1 = _post_attn(xr, [o_a, o_b], [ab_w_out[j][:A_WIDTH], ab_w_out[j][A_WIDTH:]],
                            ln_mix_g[i], ln_mix_b[i])
        else:
            qt, k, vt = _prep_c(xr, c_w_in[j])
            lp = c_lambda[j].astype(F32)
            lam_init = _lambda_init(i)
            lam = jnp.exp(jnp.sum(lp[0] * lp[1])) - jnp.exp(jnp.sum(lp[2] * lp[3])) + lam_init
            o_c = _attn_c(qt, k, vt, lam, c_subln[j], t5_table, lam_init)
            x1 = _post_attn(xr, [o_c], [c_w_out[j]], ln_mix_g[i], ln_mix_b[i])
        xr = _moe_block(i, x1, p[i, 0], router_w[i], router_b[i], w_in_all, b_in_all, w_out_all, b_out_all,
                        ple_gate_w[i], ple_proj_w[i], ln_ffn_g[i], ln_ffn_b[i])
    return xr[None]
```

```python
import functools
import math

import jax
import jax.numpy as jnp
import numpy as np
from jax import lax
from jax.experimental import pallas as pl
from jax.experimental.pallas import tpu as pltpu

F32 = jnp.float32
BF16 = jnp.bfloat16
I32 = jnp.int32

D_MODEL = 1024
DEPTH = 4
CHUNK = 64
PLE_DIM = 256

A_HEADS = 8
A_HEAD_DIM = 64
A_LEFT_CHUNKS = 8
A_REL_CLIP = 128
A_WIDTH = A_HEADS * A_HEAD_DIM

B_HEADS = 8
B_NOPE_DIM = 64
B_ROPE_DIM = 32
B_V_DIM = 64
B_Q_RANK = 512
B_KV_RANK = 256
ROPE_BASE = 10000.0

C_HEADS = 8
C_HEAD_DIM = 64
C_QK = 2 * C_HEADS * C_HEAD_DIM

T5_BUCKETS = 32
T5_MAX_DIST = 128

N_EXPERTS = 32
TOP_K = 4
D_FF = 1024
SWIGLU_ALPHA = 1.702
SWIGLU_LIMIT = 7.0

DEEPNORM_ALPHA = (2.0 * DEPTH) ** 0.25
LN_EPS = 1e-5
RMS_EPS = 1e-6
NEG_INF = -1e30

LANES = 128
ROW_SUBLANES = D_MODEL // LANES
VMEM_LIMIT = 56 * 1024 * 1024

ATT_TILE = 512
COL_CHUNK = 256
ROW_CHUNK = 64
SCORE_AHEAD = 2
SCORE_RING = 4
SUM_ROWS = 16
LOG2E = math.log2(math.e)
BAND_GROUP = 256
BAND_WINDOW = BAND_GROUP + A_LEFT_CHUNKS * CHUNK
ROUTER_TILE = 512
EXPERT_TILE = 512
DISPATCH_TILE = 512
COMBINE_TILE = 256
CAST_ROWS = 128
ISSUE_UNROLL = 8

NT_DIMS = (((1,), (1,)), ((), ()))


def _params(*sem):
    return pltpu.CompilerParams(dimension_semantics=sem, vmem_limit_bytes=VMEM_LIMIT)


def _full(shape):
    n = len(shape)
    return pl.BlockSpec(shape, lambda *_: (0,) * n)


def _layer_norm(y, g, b):
    mu = jnp.mean(y, axis=-1, keepdims=True)
    d = y - mu
    var = jnp.mean(d * d, axis=-1, keepdims=True)
    return d * lax.rsqrt(var + LN_EPS) * g + b


def _rms_norm(c, g):
    inv = lax.rsqrt(jnp.mean(c * c, axis=-1, keepdims=True) + RMS_EPS)
    return c * inv * g


def _load_row_tiles(ref, *lead):
    tiles = pltpu.einshape("ral->arl", ref[lead] if lead else ref[...])
    return jnp.concatenate([tiles[a] for a in range(ROW_SUBLANES)], axis=1)


def _store_row_tiles(ref, value):
    tiles = jnp.stack([value[:, a * LANES:(a + 1) * LANES] for a in range(ROW_SUBLANES)], axis=0)
    ref[...] = pltpu.einshape("arl->ral", tiles)


def _toeplitz(vals_pos, vals_neg, nr, nc):
    w = jnp.concatenate([vals_pos, vals_neg], axis=-1)
    lw = nc + nr
    assert w.shape[-1] == lw
    flat = jnp.tile(w, (1,) * (w.ndim - 1) + (nr,))[..., :nr * (lw - 1)]
    return flat.reshape(w.shape[:-1] + (nr, lw - 1))[..., :nc]


def _prep_ab_kernel(x_ref, wa_ref, wcq_ref, wckv_ref, wkr_ref, wkrr_ref, qn_ref, kvn_ref,
                    wuqa_ref, wuqb_ref, wukn_ref, wuv_ref, place_ref, ctab_ref, stab_ref,
                    cos_ref, sin_ref,
                    qa_ref, ka_ref, va_ref, qbt_ref, kb_ref, vbt_ref):
    xb = x_ref[...].astype(BF16)
    ha = jnp.dot(xb, wa_ref[...], preferred_element_type=F32)
    qa_ref[...] = (ha[:, :A_WIDTH] * (A_HEAD_DIM ** -0.5)).astype(BF16)
    ka_ref[...] = ha[:, A_WIDTH:2 * A_WIDTH].astype(BF16)
    va_ref[...] = ha[:, 2 * A_WIDTH:].astype(BF16)

    cq = jnp.dot(xb, wcq_ref[...], preferred_element_type=F32)
    cqn = _rms_norm(cq, qn_ref[...]).astype(BF16)
    qa_part = lax.dot_general(wuqa_ref[...], cqn, NT_DIMS, preferred_element_type=F32)
    qb_part = lax.dot_general(wuqb_ref[...], cqn, NT_DIMS, preferred_element_type=F32)
    ct = ctab_ref[...]
    st = stab_ref[...]
    scale = (B_NOPE_DIM + B_ROPE_DIM) ** -0.5 * LOG2E
    for h in range(B_HEADS):
        sl = slice(h * LANES, (h + 1) * LANES)
        qbt_ref[0, sl, :] = ((qa_part[sl, :] * ct + qb_part[sl, :] * st) * scale).astype(BF16)

    ckv = jnp.dot(xb, wckv_ref[...], preferred_element_type=F32)
    ckvn = _rms_norm(ckv, kvn_ref[...]).astype(BF16)
    kr = jnp.dot(xb, wkr_ref[...], preferred_element_type=F32)
    krr = jnp.dot(xb, wkrr_ref[...], preferred_element_type=F32)
    k_rope = (kr * cos_ref[...] + krr * sin_ref[...]).astype(BF16)
    kb = jnp.dot(ckvn, wukn_ref[...], preferred_element_type=F32)
    kb = kb + jnp.dot(k_rope, place_ref[...], preferred_element_type=F32)
    kb_ref[...] = kb.astype(BF16)
    vbt_ref[0] = lax.dot_general(wuv_ref[...], ckvn, NT_DIMS, preferred_element_type=F32).astype(BF16)


def _rot_half_cols(w):
    half = w.shape[-1] // 2
    return jnp.concatenate([-w[..., half:], w[..., :half]], axis=-1)


def _prep_ab(x, w_in, q_norm, kv_norm, w_uq, w_ukv):
    s = x.shape[0]
    t = ATT_TILE
    splits = [3 * A_WIDTH, 3 * A_WIDTH + B_Q_RANK, 3 * A_WIDTH + B_Q_RANK + B_KV_RANK]
    wa = w_in[:, :splits[0]].astype(BF16)
    wcq = w_in[:, splits[0]:splits[1]].astype(BF16)
    wckv = w_in[:, splits[1]:splits[2]].astype(BF16)
    wkr_f = w_in[:, splits[2]:]
    wkr = wkr_f.astype(BF16)
    wkrr = _rot_half_cols(wkr_f).astype(BF16)

    qd = B_NOPE_DIM + B_ROPE_DIM
    w_uq3 = w_uq.reshape(B_Q_RANK, B_HEADS, qd)
    zpad = jnp.zeros((B_Q_RANK, B_HEADS, LANES - qd), F32)
    wuqa = jnp.concatenate([w_uq3, zpad], axis=-1).reshape(B_Q_RANK, B_HEADS * LANES).T.astype(BF16)
    wuqb = jnp.concatenate([jnp.zeros((B_Q_RANK, B_HEADS, B_NOPE_DIM), F32),
                            _rot_half_cols(w_uq3[..., B_NOPE_DIM:]), zpad],
                           axis=-1).reshape(B_Q_RANK, B_HEADS * LANES).T.astype(BF16)
    w_ukv3 = w_ukv.reshape(B_KV_RANK, B_HEADS, B_NOPE_DIM + B_V_DIM)
    wukn = jnp.concatenate([w_ukv3[..., :B_NOPE_DIM],
                            jnp.zeros((B_KV_RANK, B_HEADS, LANES - B_NOPE_DIM), F32)],
                           axis=-1).reshape(B_KV_RANK, B_HEADS * LANES).astype(BF16)
    wuv = w_ukv3[..., B_NOPE_DIM:].reshape(B_KV_RANK, B_HEADS * B_V_DIM).T.astype(BF16)
    place = np.zeros((B_ROPE_DIM, B_HEADS * LANES), np.float32)
    for h in range(B_HEADS):
        for r in range(B_ROPE_DIM):
            place[r, h * LANES + B_NOPE_DIM + r] = 1.0
    place = jnp.asarray(place, BF16)

    pos = jnp.arange(s, dtype=F32)
    inv_freq = ROPE_BASE ** (-jnp.arange(0, B_ROPE_DIM, 2, dtype=F32) / B_ROPE_DIM)
    ang = pos[:, None] * inv_freq[None, :]
    cos = jnp.concatenate([jnp.cos(ang)] * 2, axis=-1)
    sin = jnp.concatenate([jnp.sin(ang)] * 2, axis=-1)
    ctab = jnp.concatenate([jnp.ones((s, B_NOPE_DIM), F32), cos, jnp.zeros((s, LANES - qd), F32)], axis=-1).T
    stab = jnp.concatenate([jnp.zeros((s, B_NOPE_DIM), F32), sin, jnp.zeros((s, LANES - qd), F32)], axis=-1).T

    row = lambda w: pl.BlockSpec((t, w), lambda i: (i, 0))
    col = pl.BlockSpec((LANES, t), lambda i: (0, i))
    tr = lambda w: pl.BlockSpec((1, w, t), lambda i: (i, 0, 0))
    consts = [wa, wcq, wckv, wkr, wkrr, q_norm.reshape(1, -1), kv_norm.reshape(1, -1),
              wuqa, wuqb, wukn, wuv, place]
    sds = jax.ShapeDtypeStruct
    return pl.pallas_call(
        _prep_ab_kernel,
        grid=(s // t,),
        in_specs=[row(D_MODEL)] + [_full(c.shape) for c in consts]
        + [col, col, row(B_ROPE_DIM), row(B_ROPE_DIM)],
        out_specs=[row(A_WIDTH), row(A_WIDTH), row(A_WIDTH),
                   tr(B_HEADS * LANES), row(B_HEADS * LANES), tr(B_HEADS * B_V_DIM)],
        out_shape=[sds((s, A_WIDTH), BF16)] * 3
        + [sds((s // t, B_HEADS * LANES, t), BF16), sds((s, B_HEADS * LANES), BF16),
           sds((s // t, B_HEADS * B_V_DIM, t), BF16)],
        compiler_params=_params("arbitrary"),
        name="prep_ab",
    )(x, *consts, ctab, stab, cos, sin)


def _attn_a_kernel(q_ref, kp_ref, kc_ref, vp_ref, vc_ref, bias_ref, o_ref, kbuf, vbuf):
    j = pl.program_id(1)
    t = ATT_TILE
    kbuf[0:t, :] = kp_ref[...]
    kbuf[t:2 * t, :] = kc_ref[...]
    vbuf[0:t, :] = vp_ref[...]
    vbuf[t:2 * t, :] = vc_ref[...]
    lo = lax.broadcasted_iota(I32, (1, LANES), 1) < A_HEAD_DIM
    col = lax.broadcasted_iota(I32, (1, BAND_WINDOW), 1)
    for g in range(t // BAND_GROUP):
        r0 = g * BAND_GROUP
        q = q_ref[r0:r0 + BAND_GROUP, :]
        k = kbuf[r0:r0 + BAND_WINDOW, :]
        v = vbuf[r0:r0 + BAND_WINDOW, :]
        valid = col + r0 >= jnp.where(j > 0, 0, t)
        outs = []
        for hh in range(2):
            qm = jnp.where(lo if hh == 0 else jnp.logical_not(lo), q, jnp.zeros_like(q))
            sc = lax.dot_general(qm, k, NT_DIMS, preferred_element_type=F32)
            sc = jnp.where(valid, sc + bias_ref[hh], NEG_INF)
            m = jnp.max(sc, axis=-1, keepdims=True)
            p = jnp.exp(sc - m)
            l = jnp.sum(p, axis=-1, keepdims=True)
            o = jnp.dot(p.astype(BF16), v, preferred_element_type=F32)
            outs.append(o / l)
        o_ref[r0:r0 + BAND_GROUP, :] = jnp.where(lo, outs[0], outs[1]).astype(BF16)


def _band_bias(rel_bias):
    qi = np.arange(BAND_GROUP)[:, None]
    kr = np.arange(BAND_WINDOW)[None, :]
    dchunk = (qi // CHUNK + A_LEFT_CHUNKS) - kr // CHUNK
    allowed = (dchunk >= 0) & (dchunk <= A_LEFT_CHUNKS)
    d_pos = np.arange(BAND_WINDOW)
    d_neg = np.arange(-BAND_GROUP, 0)
    to_idx = lambda d: np.clip(A_LEFT_CHUNKS * CHUNK - d, -A_REL_CLIP, A_REL_CLIP) + A_REL_CLIP
    tab = rel_bias.astype(F32).T
    table = _toeplitz(tab[:, to_idx(d_pos)], tab[:, to_idx(d_neg)], BAND_GROUP, BAND_WINDOW)
    return jnp.where(jnp.asarray(allowed)[None], table, NEG_INF)


def _attn_a(qa, ka, va, rel_bias):
    s = qa.shape[0]
    t = ATT_TILE
    bias = _band_bias(rel_bias)
    cur = pl.BlockSpec((t, LANES), lambda p, j: (j, p))
    prev = pl.BlockSpec((t, LANES), lambda p, j: (jnp.maximum(j - 1, 0), p))
    return pl.pallas_call(
        _attn_a_kernel,
        grid=(A_HEADS // 2, s // t),
        in_specs=[cur, prev, cur, prev, cur,
                  pl.BlockSpec((2, BAND_GROUP, BAND_WINDOW), lambda p, j: (p, 0, 0))],
        out_specs=cur,
        out_shape=jax.ShapeDtypeStruct((s, A_WIDTH), BF16),
        scratch_shapes=[pltpu.VMEM((2 * t, LANES), BF16), pltpu.VMEM((2 * t, LANES), BF16)],
        compiler_params=_params("arbitrary", "arbitrary"),
        name="attn_a",
    )(qa, ka, ka, va, va, bias)


class _FlashState:
    def __init__(self, m, l, acc, p0, p1, a0, a1, *s):
        self.m, self.l, self.acc = m, l, acc
        self.s, self.p, self.a = s, (p0, p1), (a0, a1)
        self.n_chunks = m.shape[1] // COL_CHUNK


def _flash_scratch(n_cols):
    t, w = ATT_TILE, COL_CHUNK
    return [pltpu.VMEM((1, n_cols), F32), pltpu.VMEM((1, n_cols), F32), pltpu.VMEM((LANES, n_cols), F32),
            pltpu.VMEM((t, w), BF16), pltpu.VMEM((t, w), BF16),
            pltpu.VMEM((1, w), F32), pltpu.VMEM((1, w), F32)] + [pltpu.VMEM((t, w), F32)] * SCORE_RING


def _flash_start(st, score_fn):
    st.m[...] = jnp.full(st.m.shape, NEG_INF, F32)
    st.l[...] = jnp.zeros(st.l.shape, F32)
    st.acc[...] = jnp.zeros(st.acc.shape, F32)
    st.p[1][...] = jnp.zeros(st.p[1].shape, BF16)
    st.a[1][...] = jnp.ones(st.a[1].shape, F32)
    for c in range(SCORE_AHEAD):
        st.s[c][...] = score_fn(0, c)


def _flash_values(st, vt_fn, kt, c, b):
    cs = slice(c * COL_CHUNK, (c + 1) * COL_CHUNK)
    lhs = jnp.concatenate([vt_fn(kt), jnp.ones((SUM_ROWS, ATT_TILE), BF16)], axis=0)
    pv = jnp.dot(lhs, st.p[b][...], preferred_element_type=F32)
    alpha = st.a[b][...]
    st.acc[:, cs] = alpha * st.acc[:, cs] + pv[:LANES]
    st.l[:, cs] = alpha * st.l[:, cs] + pv[LANES:LANES + 1]


def _flash_region(st, score_fn, vt_fn, tiles, has_next):
    units = [(ti, c) for ti in range(len(tiles)) for c in range(st.n_chunks)]
    assert len(units) % SCORE_RING == 0 and SCORE_AHEAD <= st.n_chunks
    for u, (ti, c) in enumerate(units):
        b = u % 2
        kt, bias_fn = tiles[ti]
        ahead = u + SCORE_AHEAD
        if ahead < len(units):
            nti, nc = units[ahead]
            st.s[ahead % SCORE_RING][...] = score_fn(tiles[nti][0], nc)
        elif has_next:
            st.s[ahead % SCORE_RING][...] = score_fn(tiles[-1][0] + 1, ahead - len(units))
        if u > 0:
            pti, pc = units[u - 1]
            _flash_values(st, vt_fn, tiles[pti][0], pc, 1 - b)
        else:
            _flash_values(st, vt_fn, jnp.maximum(kt - 1, 0), st.n_chunks - 1, 1 - b)
        cs = slice(c * COL_CHUNK, (c + 1) * COL_CHUNK)
        c0 = (c * COL_CHUNK) % ATT_TILE

        def rows(r, s_ref=st.s[u % SCORE_RING], bias_fn=bias_fn, c0=c0):
            blk = s_ref[r * ROW_CHUNK:(r + 1) * ROW_CHUNK, :]
            bias = None if bias_fn is None else bias_fn(c0, r)
            return blk if bias is None else blk + bias

        n_rows = ATT_TILE // ROW_CHUNK
        part = rows(0)
        for r in range(1, n_rows):
            part = jnp.maximum(part, rows(r))
        m_prev = st.m[:, cs]
        m_new = jnp.maximum(m_prev, jnp.max(part, axis=0, keepdims=True))
        st.a[b][...] = jnp.exp2(m_prev - m_new)
        st.m[:, cs] = m_new
        for r in range(n_rows):
            st.p[b][r * ROW_CHUNK:(r + 1) * ROW_CHUNK, :] = jnp.exp2(rows(r) - m_new).astype(BF16)


def _flash_sweep(st, score_fn, vt_fn, n_plain, tail_tiles, last_tile):
    _flash_start(st, score_fn)

    def quad(j, carry):
        _flash_region(st, score_fn, vt_fn, [(4 * j + d, None) for d in range(4)], True)
        return carry

    n_quads = lax.shift_right_logical(n_plain, 2)
    lax.fori_loop(0, n_quads, quad, 0)
    done = n_quads * 4
    for left in range(4):
        @pl.when((n_plain & 3) == left)
        def _(left=left):
            plain = [(done + d, None) for d in range(left)]
            _flash_region(st, score_fn, vt_fn, plain + tail_tiles(done + left), False)

    _flash_values(st, vt_fn, last_tile, st.n_chunks - 1, 1)


def _chunk_causal_bias_t(t):
    r = np.arange(t)
    ok = (r[:, None] // CHUNK) <= (r[None, :] // CHUNK)
    return jnp.asarray(np.where(ok, 0.0, NEG_INF), F32)


def _attn_b_kernel(qt_ref, k_ref, vt_ref, diag_ref, o_ref, *scratch):
    i = pl.program_id(1)
    t = ATT_TILE
    st = _FlashState(*scratch)
    per_head = t // COL_CHUNK

    def score(kt, c):
        hh = c // per_head
        c0 = (c % per_head) * COL_CHUNK
        k = k_ref[pl.ds(pl.multiple_of(kt * t, t), t), hh * LANES:(hh + 1) * LANES]
        return jnp.dot(k, qt_ref[0, hh * LANES:(hh + 1) * LANES, c0:c0 + COL_CHUNK],
                       preferred_element_type=F32)

    vt_fn = lambda kt: vt_ref[kt]

    def diag(c0, r):
        if (r + 1) * ROW_CHUNK - 1 < (c0 // CHUNK + 1) * CHUNK:
            return None
        return diag_ref[r * ROW_CHUNK:(r + 1) * ROW_CHUNK, c0:c0 + COL_CHUNK]

    _flash_sweep(st, score, vt_fn, i, lambda kt: [(kt, diag)], i)
    o2 = st.acc[...] / st.l[...]
    ot = jnp.concatenate([o2[:B_V_DIM, :t], o2[B_V_DIM:, t:]], axis=0)
    o_ref[...] = ot.T.astype(BF16)


def _attn_b(qbt, kb, vbt):
    nb, _, t = qbt.shape
    s = nb * t
    return pl.pallas_call(
        _attn_b_kernel,
        grid=(B_HEADS // 2, nb),
        in_specs=[pl.BlockSpec((1, 2 * LANES, t), lambda p, i: (i, p, 0)),
                  pl.BlockSpec((s, 2 * LANES), lambda p, i: (0, p)),
                  pl.BlockSpec((nb, LANES, t), lambda p, i: (0, p, 0)),
                  _full((t, t))],
        out_specs=pl.BlockSpec((t, LANES), lambda p, i: (i, p)),
        out_shape=jax.ShapeDtypeStruct((s, B_HEADS * B_V_DIM), BF16),
        scratch_shapes=_flash_scratch(2 * t),
        compiler_params=_params("arbitrary", "arbitrary"),
        name="attn_b",
    )(qbt, kb, vbt, _chunk_causal_bias_t(t))


def _attn_c_kernel(lam_ref, qt_ref, k_ref, vt_ref, d0_ref, d1_ref, g_ref, o_ref, q2_ref, *scratch, out_scale):
    i = pl.program_id(1)
    t = ATT_TILE
    st = _FlashState(*scratch)
    d = C_HEAD_DIM
    zero = jnp.zeros((d, t), BF16)
    q2_ref[:d, :t] = qt_ref[0, :d, :]
    q2_ref[d:, :t] = zero
    q2_ref[:d, t:] = zero
    q2_ref[d:, t:] = qt_ref[0, d:, :]

    def score(kt, c):
        return jnp.dot(k_ref[pl.ds(pl.multiple_of(kt * t, t), t), :],
                       q2_ref[:, c * COL_CHUNK:(c + 1) * COL_CHUNK], preferred_element_type=F32)

    vt_fn = lambda kt: vt_ref[kt]

    tile_of = lambda ref: lambda c0, r: ref[0, r * ROW_CHUNK:(r + 1) * ROW_CHUNK, c0:c0 + COL_CHUNK]

    def diag_bias(c0, r):
        newest_key = (r + 1) * ROW_CHUNK - 1
        unmasked = newest_key < (c0 // CHUNK + 1) * CHUNK
        far = newest_key - c0 <= -T5_MAX_DIST
        return None if unmasked and far else tile_of(d0_ref)(c0, r)

    def below_bias(c0, r):
        if (r + 1) * ROW_CHUNK - 1 - t - c0 <= -T5_MAX_DIST:
            return None
        return tile_of(d1_ref)(c0, r)

    @pl.when(i > 0)
    def _():
        _flash_sweep(st, score, vt_fn, i - 1, lambda kt: [(kt, below_bias), (kt + 1, diag_bias)], i)

    @pl.when(i == 0)
    def _():
        _flash_start(st, score)
        _flash_region(st, score, vt_fn, [(i, diag_bias)], False)
        _flash_values(st, vt_fn, i, st.n_chunks - 1, 1)

    o2 = st.acc[...] / st.l[...]
    ot = o2[:, :t] - lam_ref[0] * o2[:, t:]
    inv = lax.rsqrt(jnp.mean(ot * ot, axis=0, keepdims=True) + RMS_EPS)
    o_ref[...] = (ot * inv * g_ref[...] * out_scale).T.astype(BF16)


def _t5_bucket(rel):
    half = T5_BUCKETS // 2
    max_exact = half // 2
    ret = jnp.where(rel > 0, half, 0)
    n = jnp.abs(rel)
    large = max_exact + (jnp.log(jnp.maximum(n, max_exact).astype(F32) / max_exact)
                         / math.log(T5_MAX_DIST / max_exact) * (half - max_exact)).astype(I32)
    large = jnp.minimum(large, half - 1)
    return ret + jnp.where(n < max_exact, n, large)


def _t5_tiles_t(t5_table, t):
    assert t >= T5_MAX_DIST
    far_bucket = T5_BUCKETS // 2 - 1
    tab = (t5_table.astype(F32) - t5_table[far_bucket].astype(F32)[None, :]).T
    by_rel = lambda rel: tab[:, _t5_bucket(rel)]
    d_pos = jnp.arange(t, dtype=I32)
    d_neg = jnp.arange(-t, 0, dtype=I32)
    d0 = _toeplitz(by_rel(-d_pos), by_rel(-d_neg), t, t)
    d1 = _toeplitz(by_rel(-d_pos - t), by_rel(-d_neg - t), t, t)
    r = np.arange(t)
    ok = (r[:, None] // CHUNK) <= (r[None, :] // CHUNK)
    d0 = jnp.where(jnp.asarray(ok)[None], d0 * LOG2E, NEG_INF)
    return d0, d1 * LOG2E


def _attn_c(qt, k, vt, lam, subln_g, t5_table, lam_init):
    nb, _, t = qt.shape
    s = nb * t
    d0, d1 = _t5_tiles_t(t5_table, t)
    return pl.pallas_call(
        functools.partial(_attn_c_kernel, out_scale=1.0 - lam_init),
        grid=(C_HEADS, nb),
        in_specs=[pl.BlockSpec(memory_space=pltpu.SMEM),
                  pl.BlockSpec((1, LANES, t), lambda h, i: (i, h, 0)),
                  pl.BlockSpec((s, LANES), lambda h, i: (0, h)),
                  pl.BlockSpec((nb, LANES, t), lambda h, i: (0, h, 0)),
                  pl.BlockSpec((1, t, t), lambda h, i: (h, 0, 0)),
                  pl.BlockSpec((1, t, t), lambda h, i: (h, 0, 0)),
                  _full((LANES, 1))],
        out_specs=pl.BlockSpec((t, LANES), lambda h, i: (i, h)),
        out_shape=jax.ShapeDtypeStruct((s, C_HEADS * 2 * C_HEAD_DIM), BF16),
        scratch_shapes=[pltpu.VMEM((LANES, 2 * t), BF16)] + _flash_scratch(2 * t),
        compiler_params=_params("arbitrary", "arbitrary"),
        name="attn_c",
    )(lam.reshape(1), qt, k, vt, d0, d1, subln_g.reshape(-1, 1))


def _prep_c_kernel(x_ref, wq_ref, wk_ref, wv_ref, qt_ref, k_ref, vt_ref):
    xb = x_ref[...].astype(BF16)
    qt = lax.dot_general(wq_ref[...], xb, NT_DIMS, preferred_element_type=F32)
    qt_ref[0] = (qt * (C_HEAD_DIM ** -0.5 * LOG2E)).astype(BF16)
    k_ref[...] = jnp.dot(xb, wk_ref[...], preferred_element_type=F32).astype(BF16)
    vt_ref[0] = lax.dot_general(wv_ref[...], xb, NT_DIMS, preferred_element_type=F32).astype(BF16)


def _prep_c(x, w_in):
    s = x.shape[0]
    t = ATT_TILE
    wq = w_in[:, :C_QK].T.astype(BF16)
    wk = w_in[:, C_QK:2 * C_QK].astype(BF16)
    wv = w_in[:, 2 * C_QK:].T.astype(BF16)
    row = pl.BlockSpec((t, D_MODEL), lambda i: (i, 0))
    tr = pl.BlockSpec((1, C_QK, t), lambda i: (i, 0, 0))
    sds = jax.ShapeDtypeStruct
    return pl.pallas_call(
        _prep_c_kernel,
        grid=(s // t,),
        in_specs=[row, _full(wq.shape), _full(wk.shape), _full(wv.shape)],
        out_specs=[tr, row, tr],
        out_shape=[sds((s // t, C_QK, t), BF16), sds((s, C_QK), BF16), sds((s // t, C_QK, t), BF16)],
        compiler_params=_params("arbitrary"),
        name="prep_c",
    )(x, wq, wk, wv)


def _post_attn_kernel(*refs, n_in):
    x_ref = refs[0]
    o_refs = refs[1:1 + n_in]
    w_refs = refs[1 + n_in:1 + 2 * n_in]
    g_ref, b_ref, out_ref = refs[1 + 2 * n_in:]
    y = DEEPNORM_ALPHA * x_ref[...]
    for o_ref, w_ref in zip(o_refs, w_refs):
        y = y + jnp.dot(o_ref[...], w_ref[...], preferred_element_type=F32)
    _store_row_tiles(out_ref, _layer_norm(y, g_ref[...], b_ref[...]))


def _post_attn(x, outs, weights, g, b):
    s = x.shape[0]
    t = ATT_TILE
    row = lambda w: pl.BlockSpec((t, w), lambda i: (i, 0))
    ws = [w.astype(BF16) for w in weights]
    return pl.pallas_call(
        functools.partial(_post_attn_kernel, n_in=len(outs)),
        grid=(s // t,),
        in_specs=[row(D_MODEL)] + [row(o.shape[1]) for o in outs] + [_full(w.shape) for w in ws]
        + [_full((1, D_MODEL))] * 2,
        out_specs=pl.BlockSpec((t, ROW_SUBLANES, LANES), lambda i: (i, 0, 0)),
        out_shape=jax.ShapeDtypeStruct((s, ROW_SUBLANES, LANES), F32),
        compiler_params=_params("arbitrary"),
        name="post_attn",
    )(x, *outs, *ws, g.reshape(1, -1), b.reshape(1, -1))


def _router_kernel(x_ref, wr_ref, br_ref, tri_ref, idx_ref, rank_ref, w_ref, cnt_ref, carry):
    step = pl.program_id(0)

    @pl.when(step == 0)
    def _():
        carry[...] = jnp.zeros(carry.shape, F32)

    t = ROUTER_TILE
    logits = lax.dot_general(wr_ref[...], _load_row_tiles(x_ref), NT_DIMS, preferred_element_type=F32,
                             precision=lax.Precision.HIGHEST) + br_ref[...]
    eid = lax.broadcasted_iota(I32, (N_EXPERTS, t), 0).astype(F32)
    work = logits
    vals, sels = [], []
    for _ in range(TOP_K):
        m = jnp.max(work, axis=0, keepdims=True)
        sel = jnp.min(jnp.where(work == m, eid, float(N_EXPERTS)), axis=0, keepdims=True)
        vals.append(m)
        sels.append(sel)
        work = jnp.where(eid == sel, -jnp.inf, work)
    exps = [jnp.exp(v - vals[0]) for v in vals]
    denom = exps[0] + exps[1] + exps[2] + exps[3]

    chosen = jnp.zeros((N_EXPERTS, t), F32)
    for sel in sels:
        chosen = chosen + (eid == sel).astype(F32)
    before = jnp.dot(chosen.astype(BF16), tri_ref[...], preferred_element_type=F32) + carry[...]
    for kk, sel in enumerate(sels):
        rank = jnp.sum(jnp.where(eid == sel, before, 0.0), axis=0, keepdims=True)
        rank_ref[kk:kk + 1, :] = rank.astype(I32)
        idx_ref[kk:kk + 1, :] = sel.astype(I32)
    carry[...] = carry[...] + jnp.sum(chosen, axis=1, keepdims=True)
    cnt_ref[...] = jnp.broadcast_to(carry[...], cnt_ref.shape)

    sub = lax.broadcasted_iota(I32, (LANES, t), 0)
    wrows = jnp.zeros((LANES, t), F32)
    for kk in range(TOP_K):
        wrows = jnp.where(sub == kk, exps[kk] / denom, wrows)
    w_ref[...] = wrows.T


def _router(x1, router_w, router_b):
    s = x1.shape[0]
    t = ROUTER_TILE
    tri = jnp.asarray(np.triu(np.ones((t, t), np.float32), k=1), BF16)
    tok = pl.BlockSpec((TOP_K, t), lambda i: (0, i))
    return pl.pallas_call(
        _router_kernel,
        grid=(s // t,),
        in_specs=[pl.BlockSpec((t, ROW_SUBLANES, LANES), lambda i: (i, 0, 0)),
                  _full((N_EXPERTS, D_MODEL)), _full((N_EXPERTS, 1)), _full((t, t))],
        out_specs=[tok, tok, pl.BlockSpec((t, LANES), lambda i: (i, 0)), _full((N_EXPERTS, LANES))],
        out_shape=[jax.ShapeDtypeStruct((TOP_K, s), I32), jax.ShapeDtypeStruct((TOP_K, s), I32),
                   jax.ShapeDtypeStruct((s, LANES), F32), jax.ShapeDtypeStruct((N_EXPERTS, LANES), F32)],
        scratch_shapes=[pltpu.VMEM((N_EXPERTS, 1), F32)],
        compiler_params=_params("arbitrary"),
        name="router",
    )(x1, router_w.T, router_b.reshape(-1, 1), tri)


def _dispatch_kernel(pad_ref, pos_hbm, x_ref, xs_hbm, pos_smem, zeros, pos_sem, row_sem, zero_sem):
    i = pl.program_id(0)
    t = DISPATCH_TILE
    n = t * TOP_K
    tm = EXPERT_TILE

    @pl.when(i == 0)
    def _():
        zeros[...] = jnp.zeros(zeros.shape, F32)

        def fill(e):
            row = pl.multiple_of(jnp.maximum(pad_ref[e], 0), tm)
            return pltpu.make_async_copy(zeros, xs_hbm.at[pl.ds(row, tm)], zero_sem)

        for e in range(2 * N_EXPERTS):
            pl.when(pad_ref[e] >= 0)(lambda e=e: fill(e).start())
        for e in range(2 * N_EXPERTS):
            pl.when(pad_ref[e] >= 0)(lambda e=e: fill(e).wait())

    cp = pltpu.make_async_copy(pos_hbm.at[pl.ds(pl.multiple_of(i * n, n), n)], pos_smem, pos_sem)
    cp.start()
    cp.wait()

    def body(g, carry):
        for j in range(ISSUE_UNROLL // TOP_K):
            tok = g * (ISSUE_UNROLL // TOP_K) + j
            for kk in range(TOP_K):
                pltpu.make_async_copy(x_ref.at[tok], xs_hbm.at[pos_smem[tok * TOP_K + kk]],
                                      row_sem).start(priority=kk % 2)
        return carry

    lax.fori_loop(0, n // ISSUE_UNROLL, body, 0)
    for kk in range(TOP_K):
        pltpu.make_async_copy(x_ref, xs_hbm.at[pl.ds(0, t)], row_sem).wait()


def _dispatch(pad_rows, pos_flat, x1, n_rows):
    s = x1.shape[0]
    t = DISPATCH_TILE
    any_spec = pl.BlockSpec(memory_space=pl.ANY)
    grid_spec = pltpu.PrefetchScalarGridSpec(
        num_scalar_prefetch=1,
        grid=(s // t,),
        in_specs=[any_spec, pl.BlockSpec((t, ROW_SUBLANES, LANES), lambda i, pad: (i, 0, 0))],
        out_specs=any_spec,
        scratch_shapes=[pltpu.SMEM((t * TOP_K,), I32), pltpu.VMEM((EXPERT_TILE, ROW_SUBLANES, LANES), F32),
                        pltpu.SemaphoreType.DMA, pltpu.SemaphoreType.DMA, pltpu.SemaphoreType.DMA],
    )
    return pl.pallas_call(
        _dispatch_kernel,
        grid_spec=grid_spec,
        out_shape=jax.ShapeDtypeStruct((n_rows, ROW_SUBLANES, LANES), F32),
        compiler_params=_params("arbitrary"),
        name="dispatch",
    )(pad_rows, pos_flat, x1)


def _gmm_kernel(te_ref, nt_ref, xs_ref, wi_ref, bi_ref, wo_ref, bo_ref, y_ref, wi_bf, wo_bf):
    i = pl.program_id(0)
    n_tiles = nt_ref[0]
    prev = te_ref[jnp.maximum(i - 1, 0)]
    fresh = jnp.logical_or(i == 0, te_ref[i] != prev)

    @pl.when(jnp.logical_and(i < n_tiles, fresh))
    def _():
        def cast(c, carry):
            rows = pl.ds(pl.multiple_of(c * CAST_ROWS, CAST_ROWS), CAST_ROWS)
            wi_bf[rows, :] = wi_ref[0, rows, :].astype(BF16)
            wo_bf[rows, :] = wo_ref[0, rows, :].astype(BF16)
            return carry

        lax.fori_loop(0, D_MODEL // CAST_ROWS, cast, 0)

    @pl.when(i < n_tiles)
    def _():
        xb = _load_row_tiles(xs_ref).astype(BF16)
        h = jnp.dot(xb, wi_bf[...], preferred_element_type=F32) + bi_ref[0]
        glu = jnp.minimum(h[:, :D_FF], SWIGLU_LIMIT)
        lin = jnp.clip(h[:, D_FF:], -SWIGLU_LIMIT, SWIGLU_LIMIT)
        act = glu * jax.nn.sigmoid(SWIGLU_ALPHA * glu) * (lin + 1.0)
        _store_row_tiles(y_ref, jnp.dot(act.astype(BF16), wo_bf[...], preferred_element_type=F32) + bo_ref[0])

    @pl.when(i >= n_tiles)
    def _():
        y_ref[...] = jnp.zeros(y_ref.shape, F32)


def _gmm(layer, tile_expert, n_tiles, xs, w_in, b_in, w_out, b_out):
    n_rows = xs.shape[0]
    tm = EXPERT_TILE
    last = lambda i, nt: jnp.maximum(jnp.minimum(i, nt[0] - 1), 0)
    expert = lambda i, te, nt: layer * N_EXPERTS + te[last(i, nt)]
    grid_spec = pltpu.PrefetchScalarGridSpec(
        num_scalar_prefetch=2,
        grid=(n_rows // tm,),
        in_specs=[pl.BlockSpec((tm, ROW_SUBLANES, LANES), lambda i, te, nt: (last(i, nt), 0, 0)),
                  pl.BlockSpec((1, D_MODEL, 2 * D_FF), lambda i, te, nt: (expert(i, te, nt), 0, 0)),
                  pl.BlockSpec((1, 1, 2 * D_FF), lambda i, te, nt: (expert(i, te, nt), 0, 0)),
                  pl.BlockSpec((1, D_FF, D_MODEL), lambda i, te, nt: (expert(i, te, nt), 0, 0)),
                  pl.BlockSpec((1, 1, D_MODEL), lambda i, te, nt: (expert(i, te, nt), 0, 0))],
        out_specs=pl.BlockSpec((tm, ROW_SUBLANES, LANES), lambda i, te, nt: (i, 0, 0)),
        scratch_shapes=[pltpu.VMEM((D_MODEL, 2 * D_FF), BF16), pltpu.VMEM((D_FF, D_MODEL), BF16)],
    )
    return pl.pallas_call(
        _gmm_kernel,
        grid_spec=grid_spec,
        out_shape=jax.ShapeDtypeStruct((n_rows, ROW_SUBLANES, LANES), F32),
        compiler_params=_params("arbitrary"),
        name="expert_mlp",
    )(tile_expert, n_tiles, xs, w_in, b_in, w_out, b_out)


def _combine_kernel(pos_hbm, y_hbm, x_ref, w_ref, p_ref, wg_ref, wp_ref, g_ref, b_ref, out_ref,
                    pos_smem, ybuf, pos_sem, row_sem):
    i = pl.program_id(0)
    t = COMBINE_TILE
    n = t * TOP_K

    def fetch_indices(step):
        base = pl.multiple_of((step & 1) * n, n)
        cp = pltpu.make_async_copy(pos_hbm.at[pl.ds(pl.multiple_of(step * n, n), n)],
                                   pos_smem.at[pl.ds(base, n)], pos_sem)
        cp.start()
        cp.wait()

    def row_copy(step, tok, kk):
        slot = step & 1
        return pltpu.make_async_copy(y_hbm.at[pos_smem[slot * n + tok * TOP_K + kk]], ybuf.at[slot, kk, tok],
                                     row_sem.at[slot])

    def wait_step(step):
        slot = step & 1
        for kk in range(TOP_K):
            pltpu.make_async_copy(y_hbm.at[pl.ds(0, t)], ybuf.at[slot, kk], row_sem.at[slot]).wait()

    @pl.when(i == 0)
    def _():
        fetch_indices(i)

        def issue(tok, carry):
            for kk in range(TOP_K):
                row_copy(i, tok, kk).start(priority=kk % 2)
            return carry

        lax.fori_loop(0, t, issue, 0, unroll=2)

    nxt = i + 1
    fetch_indices(nxt)
    x = _load_row_tiles(x_ref)
    pb = p_ref[...].astype(BF16)
    w = w_ref[...]
    for tok in range(t):
        for kk in range(TOP_K):
            row_copy(nxt, tok, kk).start(priority=kk % 2)
    gate = jax.nn.sigmoid(jnp.dot(x.astype(BF16), wg_ref[...], preferred_element_type=F32))
    proj = jnp.dot(pb, wp_ref[...], preferred_element_type=F32)
    y = DEEPNORM_ALPHA * x + gate * proj

    wait_step(i)
    for kk in range(TOP_K):
        y = y + w[:, kk:kk + 1] * _load_row_tiles(ybuf, i & 1, kk)
    out_ref[...] = _layer_norm(y, g_ref[...], b_ref[...])

    @pl.when(nxt == pl.num_programs(0))
    def _():
        wait_step(nxt)


def _combine(pos_flat, y, x1, wcol, p, gate_w, proj_w, g, b):
    s = x1.shape[0]
    t = COMBINE_TILE
    any_spec = pl.BlockSpec(memory_space=pl.ANY)
    row = lambda w: pl.BlockSpec((t, w), lambda i: (i, 0))
    return pl.pallas_call(
        _combine_kernel,
        grid=(s // t,),
        in_specs=[any_spec, any_spec, pl.BlockSpec((t, ROW_SUBLANES, LANES), lambda i: (i, 0, 0)),
                  row(LANES), row(PLE_DIM),
                  _full((D_MODEL, D_MODEL)), _full((PLE_DIM, D_MODEL)),
                  _full((1, D_MODEL)), _full((1, D_MODEL))],
        out_specs=row(D_MODEL),
        out_shape=jax.ShapeDtypeStruct((s, D_MODEL), F32),
        scratch_shapes=[pltpu.SMEM((2 * t * TOP_K,), I32),
                        pltpu.VMEM((2, TOP_K, t, ROW_SUBLANES, LANES), F32),
                        pltpu.SemaphoreType.DMA, pltpu.SemaphoreType.DMA((2,))],
        compiler_params=_params("arbitrary"),
        name="combine",
    )(pos_flat, y, x1, wcol, p, gate_w.astype(BF16), proj_w.astype(BF16), g.reshape(1, -1), b.reshape(1, -1))


def _moe_block(layer, x1, p, router_w, router_b, w_in, b_in, w_out, b_out, gate_w, proj_w, g, b):
    s = x1.shape[0]
    tm = EXPERT_TILE
    n_tiles_max = s * TOP_K // tm + N_EXPERTS
    idx, rank, wcol, cnt = _router(x1, router_w, router_b)
    counts = cnt[:, 0].astype(I32)
    tiles_per = (counts + tm - 1) // tm
    tile_end = jnp.cumsum(tiles_per)
    offsets = (tile_end - tiles_per) * tm
    experts = jnp.arange(N_EXPERTS, dtype=I32)
    pos = rank + jnp.sum(jnp.where(idx[..., None] == experts, offsets, 0), axis=-1)
    pos_flat = pos.T.reshape(-1)
    tile_ids = jnp.arange(n_tiles_max, dtype=I32)
    tile_expert = jnp.minimum(jnp.sum((tile_end[None, :] <= tile_ids[:, None]).astype(I32), axis=-1),
                              N_EXPERTS - 1)
    n_tiles = tile_end[-1:].astype(I32)
    last_tiles = jnp.where(tiles_per > 0, tile_end - 1, -1)
    tail_tiles = jnp.where(tile_end[-1] + experts < n_tiles_max, tile_end[-1] + experts, -1)
    fill_tiles = jnp.concatenate([last_tiles, tail_tiles])
    pad_rows = jnp.where(fill_tiles >= 0, fill_tiles * tm, -1).astype(I32)
    xs = _dispatch(pad_rows, pos_flat, x1, n_tiles_max * tm)
    y = _gmm(layer, tile_expert, n_tiles, xs, w_in, b_in, w_out, b_out)
    pos_ahead = jnp.concatenate([pos_flat, jnp.zeros((COMBINE_TILE * TOP_K,), I32)])
    return _combine(pos_ahead, y, x1, wcol, p, gate_w, proj_w, g, b)


def _lambda_init(layer_idx):
    return 0.8 - 0.6 * math.exp(-0.3 * layer_idx)


def kernel(x, p, ab_w_in, ab_rel_bias, ab_q_norm, ab_kv_norm, ab_w_uq, ab_w_ukv, ab_w_out, c_w_in, c_lambda, c_subln, c_w_out, t5_table, ln_mix_g, ln_mix_b, ln_ffn_g, ln_ffn_b, router_w, router_b, exp_w_in, exp_b_in, exp_w_out, exp_b_out, ple_gate_w, ple_proj_w):
    batch, s, _ = x.shape
    assert batch == 1 and s % DISPATCH_TILE == 0 and s % ATT_TILE == 0
    xr = x[0]
    w_in_all = exp_w_in.reshape(DEPTH * N_EXPERTS, D_MODEL, 2 * D_FF)
    b_in_all = exp_b_in.reshape(DEPTH * N_EXPERTS, 1, 2 * D_FF)
    w_out_all = exp_w_out.reshape(DEPTH * N_EXPERTS, D_FF, D_MODEL)
    b_out_all = exp_b_out.reshape(DEPTH * N_EXPERTS, 1, D_MODEL)
    for i in range(DEPTH):
        j = i // 2
        if i % 2 == 0:
            qa, ka, va, qbt, kb, vbt = _prep_ab(xr, ab_w_in[j], ab_q_norm[j], ab_kv_norm[j], ab_w_uq[j],
                                                ab_w_ukv[j])
            o_a = _attn_a(qa, ka, va, ab_rel_bias[j])
            o_b = _attn_b(qbt, kb, vbt)
            x1 = _post_attn(xr, [o_a, o_b], [ab_w_out[j][:A_WIDTH], ab_w_out[j][A_WIDTH:]],
                            ln_mix_g[i], ln_mix_b[i])
        else:
            qt, k, vt = _prep_c(xr, c_w_in[j])
            lp = c_lambda[j].astype(F32)
            lam_init = _lambda_init(i)
            lam = jnp.exp(jnp.sum(lp[0] * lp[1])) - jnp.exp(jnp.sum(lp[2] * lp[3])) + lam_init
            o_c = _attn_c(qt, k, vt, lam, c_subln[j], t5_table, lam_init)
            x1 = _post_attn(xr, [o_c], [c_w_out[j]], ln_mix_g[i], ln_mix_b[i])
        xr = _moe_block(i, x1, p[i, 0], router_w[i], router_b[i], w_in_all, b_in_all, w_out_all, b_out_all,
                        ple_gate_w[i], ple_proj_w[i], ln_ffn_g[i], ln_ffn_b[i])
    return xr[None]
```

```python
import functools
import math

import jax
import jax.numpy as jnp
import numpy as np
from jax import lax
from jax.experimental import pallas as pl
from jax.experimental.pallas import tpu as pltpu

F32 = jnp.float32
BF16 = jnp.bfloat16
I32 = jnp.int32

D_MODEL = 1024
DEPTH = 4
CHUNK = 64
PLE_DIM = 256

A_HEADS = 8
A_HEAD_DIM = 64
A_LEFT_CHUNKS = 8
A_REL_CLIP = 128
A_WIDTH = A_HEADS * A_HEAD_DIM

B_HEADS = 8
B_NOPE_DIM = 64
B_ROPE_DIM = 32
B_V_DIM = 64
B_Q_RANK = 512
B_KV_RANK = 256
ROPE_BASE = 10000.0

C_HEADS = 8
C_HEAD_DIM = 64
C_QK = 2 * C_HEADS * C_HEAD_DIM

T5_BUCKETS = 32
T5_MAX_DIST = 128

N_EXPERTS = 32
TOP_K = 4
D_FF = 1024
SWIGLU_ALPHA = 1.702
SWIGLU_LIMIT = 7.0

DEEPNORM_ALPHA = (2.0 * DEPTH) ** 0.25
LN_EPS = 1e-5
RMS_EPS = 1e-6
NEG_INF = -1e30

LANES = 128
ROW_SUBLANES = D_MODEL // LANES
VMEM_LIMIT = 56 * 1024 * 1024

ATT_TILE = 512
COL_CHUNK = 256
ROW_CHUNK = 64
SCORE_AHEAD = 2
SCORE_RING = 4
SUM_ROWS = 16
LOG2E = math.log2(math.e)
BAND_GROUP = 256
BAND_WINDOW = BAND_GROUP + A_LEFT_CHUNKS * CHUNK
ROUTER_TILE = 512
EXPERT_TILE = 512
DISPATCH_TILE = 512
COMBINE_TILE = 256
CAST_ROWS = 128
ISSUE_UNROLL = 8

NT_DIMS = (((1,), (1,)), ((), ()))


def _params(*sem):
    return pltpu.CompilerParams(dimension_semantics=sem, vmem_limit_bytes=VMEM_LIMIT)


def _full(shape):
    n = len(shape)
    return pl.BlockSpec(shape, lambda *_: (0,) * n)


def _layer_norm(y, g, b):
    mu = jnp.mean(y, axis=-1, keepdims=True)
    d = y - mu
    var = jnp.mean(d * d, axis=-1, keepdims=True)
    return d * lax.rsqrt(var + LN_EPS) * g + b


def _rms_norm(c, g):
    inv = lax.rsqrt(jnp.mean(c * c, axis=-1, keepdims=True) + RMS_EPS)
    return c * inv * g


def _load_row_tiles(ref, *lead):
    tiles = pltpu.einshape("ral->arl", ref[lead] if lead else ref[...])
    return jnp.concatenate([tiles[a] for a in range(ROW_SUBLANES)], axis=1)


def _store_row_tiles(ref, value):
    tiles = jnp.stack([value[:, a * LANES:(a + 1) * LANES] for a in range(ROW_SUBLANES)], axis=0)
    ref[...] = pltpu.einshape("arl->ral", tiles)


def _toeplitz(vals_pos, vals_neg, nr, nc):
    w = jnp.concatenate([vals_pos, vals_neg], axis=-1)
    lw = nc + nr
    assert w.shape[-1] == lw
    flat = jnp.tile(w, (1,) * (w.ndim - 1) + (nr,))[..., :nr * (lw - 1)]
    return flat.reshape(w.shape[:-1] + (nr, lw - 1))[..., :nc]


def _prep_ab_kernel(x_ref, wa_ref, wcq_ref, wckv_ref, wkr_ref, wkrr_ref, qn_ref, kvn_ref,
                    wuqa_ref, wuqb_ref, wukn_ref, wuv_ref, place_ref, ctab_ref, stab_ref,
                    cos_ref, sin_ref,
                    qa_ref, ka_ref, va_ref, qbt_ref, kb_ref, vbt_ref):
    xb = x_ref[...].astype(BF16)
    ha = jnp.dot(xb, wa_ref[...], preferred_element_type=F32)
    qa_ref[...] = (ha[:, :A_WIDTH] * (A_HEAD_DIM ** -0.5)).astype(BF16)
    ka_ref[...] = ha[:, A_WIDTH:2 * A_WIDTH].astype(BF16)
    va_ref[...] = ha[:, 2 * A_WIDTH:].astype(BF16)

    cq = jnp.dot(xb, wcq_ref[...], preferred_element_type=F32)
    cqn = _rms_norm(cq, qn_ref[...]).astype(BF16)
    qa_part = lax.dot_general(wuqa_ref[...], cqn, NT_DIMS, preferred_element_type=F32)
    qb_part = lax.dot_general(wuqb_ref[...], cqn, NT_DIMS, preferred_element_type=F32)
    ct = ctab_ref[...]
    st = stab_ref[...]
    scale = (B_NOPE_DIM + B_ROPE_DIM) ** -0.5 * LOG2E
    for h in range(B_HEADS):
        sl = slice(h * LANES, (h + 1) * LANES)
        qbt_ref[0, sl, :] = ((qa_part[sl, :] * ct + qb_part[sl, :] * st) * scale).astype(BF16)

    ckv = jnp.dot(xb, wckv_ref[...], preferred_element_type=F32)
    ckvn = _rms_norm(ckv, kvn_ref[...]).astype(BF16)
    kr = jnp.dot(xb, wkr_ref[...], preferred_element_type=F32)
    krr = jnp.dot(xb, wkrr_ref[...], preferred_element_type=F32)
    k_rope = (kr * cos_ref[...] + krr * sin_ref[...]).astype(BF16)
    kb = jnp.dot(ckvn, wukn_ref[...], preferred_element_type=F32)
    kb = kb + jnp.dot(k_rope, place_ref[...], preferred_element_type=F32)
    kb_ref[...] = kb.astype(BF16)
    vbt_ref[0] = lax.dot_general(wuv_ref[...], ckvn, NT_DIMS, preferred_element_type=F32).astype(BF16)


def _rot_half_cols(w):
    half = w.shape[-1] // 2
    return jnp.concatenate([-w[..., half:], w[..., :half]], axis=-1)


def _prep_ab(x, w_in, q_norm, kv_norm, w_uq, w_ukv):
    s = x.shape[0]
    t = ATT_TILE
    splits = [3 * A_WIDTH, 3 * A_WIDTH + B_Q_RANK, 3 * A_WIDTH + B_Q_RANK + B_KV_RANK]
    wa = w_in[:, :splits[0]].astype(BF16)
    wcq = w_in[:, splits[0]:splits[1]].astype(BF16)
    wckv = w_in[:, splits[1]:splits[2]].astype(BF16)
    wkr_f = w_in[:, splits[2]:]
    wkr = wkr_f.astype(BF16)
    wkrr = _rot_half_cols(wkr_f).astype(BF16)

    qd = B_NOPE_DIM + B_ROPE_DIM
    w_uq3 = w_uq.reshape(B_Q_RANK, B_HEADS, qd)
    zpad = jnp.zeros((B_Q_RANK, B_HEADS, LANES - qd), F32)
    wuqa = jnp.concatenate([w_uq3, zpad], axis=-1).reshape(B_Q_RANK, B_HEADS * LANES).T.astype(BF16)
    wuqb = jnp.concatenate([jnp.zeros((B_Q_RANK, B_HEADS, B_NOPE_DIM), F32),
                            _rot_half_cols(w_uq3[..., B_NOPE_DIM:]), zpad],
                           axis=-1).reshape(B_Q_RANK, B_HEADS * LANES).T.astype(BF16)
    w_ukv3 = w_ukv.reshape(B_KV_RANK, B_HEADS, B_NOPE_DIM + B_V_DIM)
    wukn = jnp.concatenate([w_ukv3[..., :B_NOPE_DIM],
                            jnp.zeros((B_KV_RANK, B_HEADS, LANES - B_NOPE_DIM), F32)],
                           axis=-1).reshape(B_KV_RANK, B_HEADS * LANES).astype(BF16)
    wuv = w_ukv3[..., B_NOPE_DIM:].reshape(B_KV_RANK, B_HEADS * B_V_DIM).T.astype(BF16)
    place = np.zeros((B_ROPE_DIM, B_HEADS * LANES), np.float32)
    for h in range(B_HEADS):
        for r in range(B_ROPE_DIM):
            place[r, h * LANES + B_NOPE_DIM + r] = 1.0
    place = jnp.asarray(place, BF16)

    pos = jnp.arange(s, dtype=F32)
    inv_freq = ROPE_BASE ** (-jnp.arange(0, B_ROPE_DIM, 2, dtype=F32) / B_ROPE_DIM)
    ang = pos[:, None] * inv_freq[None, :]
    cos = jnp.concatenate([jnp.cos(ang)] * 2, axis=-1)
    sin = jnp.concatenate([jnp.sin(ang)] * 2, axis=-1)
    ctab = jnp.concatenate([jnp.ones((s, B_NOPE_DIM), F32), cos, jnp.zeros((s, LANES - qd), F32)], axis=-1).T
    stab = jnp.concatenate([jnp.zeros((s, B_NOPE_DIM), F32), sin, jnp.zeros((s, LANES - qd), F32)], axis=-1).T

    row = lambda w: pl.BlockSpec((t, w), lambda i: (i, 0))
    col = pl.BlockSpec((LANES, t), lambda i: (0, i))
    tr = lambda w: pl.BlockSpec((1, w, t), lambda i: (i, 0, 0))
    consts = [wa, wcq, wckv, wkr, wkrr, q_norm.reshape(1, -1), kv_norm.reshape(1, -1),
              wuqa, wuqb, wukn, wuv, place]
    sds = jax.ShapeDtypeStruct
    return pl.pallas_call(
        _prep_ab_kernel,
        grid=(s // t,),
        in_specs=[row(D_MODEL)] + [_full(c.shape) for c in consts]
        + [col, col, row(B_ROPE_DIM), row(B_ROPE_DIM)],
        out_specs=[row(A_WIDTH), row(A_WIDTH), row(A_WIDTH),
                   tr(B_HEADS * LANES), row(B_HEADS * LANES), tr(B_HEADS * B_V_DIM)],
        out_shape=[sds((s, A_WIDTH), BF16)] * 3
        + [sds((s // t, B_HEADS * LANES, t), BF16), sds((s, B_HEADS * LANES), BF16),
           sds((s // t, B_HEADS * B_V_DIM, t), BF16)],
        compiler_params=_params("arbitrary"),
        name="prep_ab",
    )(x, *consts, ctab, stab, cos, sin)


def _attn_a_kernel(q_ref, kp_ref, kc_ref, vp_ref, vc_ref, bias_ref, o_ref, kbuf, vbuf):
    j = pl.program_id(1)
    t = ATT_TILE
    kbuf[0:t, :] = kp_ref[...]
    kbuf[t:2 * t, :] = kc_ref[...]
    vbuf[0:t, :] = vp_ref[...]
    vbuf[t:2 * t, :] = vc_ref[...]
    lo = lax.broadcasted_iota(I32, (1, LANES), 1) < A_HEAD_DIM
    col = lax.broadcasted_iota(I32, (1, BAND_WINDOW), 1)
    for g in range(t // BAND_GROUP):
        r0 = g * BAND_GROUP
        q = q_ref[r0:r0 + BAND_GROUP, :]
        k = kbuf[r0:r0 + BAND_WINDOW, :]
        v = vbuf[r0:r0 + BAND_WINDOW, :]
        valid = col + r0 >= jnp.where(j > 0, 0, t)
        outs = []
        for hh in range(2):
            qm = jnp.where(lo if hh == 0 else jnp.logical_not(lo), q, jnp.zeros_like(q))
            sc = lax.dot_general(qm, k, NT_DIMS, preferred_element_type=F32)
            sc = jnp.where(valid, sc + bias_ref[hh], NEG_INF)
            m = jnp.max(sc, axis=-1, keepdims=True)
            p = jnp.exp(sc - m)
            l = jnp.sum(p, axis=-1, keepdims=True)
            o = jnp.dot(p.astype(BF16), v, preferred_element_type=F32)
            outs.append(o / l)
        o_ref[r0:r0 + BAND_GROUP, :] = jnp.where(lo, outs[0], outs[1]).astype(BF16)


def _band_bias(rel_bias):
    qi = np.arange(BAND_GROUP)[:, None]
    kr = np.arange(BAND_WINDOW)[None, :]
    dchunk = (qi // CHUNK + A_LEFT_CHUNKS) - kr // CHUNK
    allowed = (dchunk >= 0) & (dchunk <= A_LEFT_CHUNKS)
    d_pos = np.arange(BAND_WINDOW)
    d_neg = np.arange(-BAND_GROUP, 0)
    to_idx = lambda d: np.clip(A_LEFT_CHUNKS * CHUNK - d, -A_REL_CLIP, A_REL_CLIP) + A_REL_CLIP
    tab = rel_bias.astype(F32).T
    table = _toeplitz(tab[:, to_idx(d_pos)], tab[:, to_idx(d_neg)], BAND_GROUP, BAND_WINDOW)
    return jnp.where(jnp.asarray(allowed)[None], table, NEG_INF)


def _attn_a(qa, ka, va, rel_bias):
    s = qa.shape[0]
    t = ATT_TILE
    bias = _band_bias(rel_bias)
    cur = pl.BlockSpec((t, LANES), lambda p, j: (j, p))
    prev = pl.BlockSpec((t, LANES), lambda p, j: (jnp.maximum(j - 1, 0), p))
    return pl.pallas_call(
        _attn_a_kernel,
        grid=(A_HEADS // 2, s // t),
        in_specs=[cur, prev, cur, prev, cur,
                  pl.BlockSpec((2, BAND_GROUP, BAND_WINDOW), lambda p, j: (p, 0, 0))],
        out_specs=cur,
        out_shape=jax.ShapeDtypeStruct((s, A_WIDTH), BF16),
        scratch_shapes=[pltpu.VMEM((2 * t, LANES), BF16), pltpu.VMEM((2 * t, LANES), BF16)],
        compiler_params=_params("arbitrary", "arbitrary"),
        name="attn_a",
    )(qa, ka, ka, va, va, bias)


class _FlashState:
    def __init__(self, m, l, acc, p0, p1, a0, a1, *s):
        self.m, self.l, self.acc = m, l, acc
        self.s, self.p, self.a = s, (p0, p1), (a0, a1)
        self.n_chunks = m.shape[1] // COL_CHUNK


def _flash_scratch(n_cols):
    t, w = ATT_TILE, COL_CHUNK
    return [pltpu.VMEM((1, n_cols), F32), pltpu.VMEM((1, n_cols), F32), pltpu.VMEM((LANES, n_cols), F32),
            pltpu.VMEM((t, w), BF16), pltpu.VMEM((t, w), BF16),
            pltpu.VMEM((1, w), F32), pltpu.VMEM((1, w), F32)] + [pltpu.VMEM((t, w), F32)] * SCORE_RING


def _flash_start(st, score_fn):
    st.m[...] = jnp.full(st.m.shape, NEG_INF, F32)
    st.l[...] = jnp.zeros(st.l.shape, F32)
    st.acc[...] = jnp.zeros(st.acc.shape, F32)
    st.p[1][...] = jnp.zeros(st.p[1].shape, BF16)
    st.a[1][...] = jnp.ones(st.a[1].shape, F32)
    for c in range(SCORE_AHEAD):
        st.s[c][...] = score_fn(0, c)


def _flash_values(st, vt_fn, kt, c, b):
    cs = slice(c * COL_CHUNK, (c + 1) * COL_CHUNK)
    lhs = jnp.concatenate([vt_fn(kt), jnp.ones((SUM_ROWS, ATT_TILE), BF16)], axis=0)
    pv = jnp.dot(lhs, st.p[b][...], preferred_element_type=F32)
    alpha = st.a[b][...]
    st.acc[:, cs] = alpha * st.acc[:, cs] + pv[:LANES]
    st.l[:, cs] = alpha * st.l[:, cs] + pv[LANES:LANES + 1]


def _flash_region(st, score_fn, vt_fn, tiles, has_next):
    units = [(ti, c) for ti in range(len(tiles)) for c in range(st.n_chunks)]
    assert len(units) % SCORE_RING == 0 and SCORE_AHEAD <= st.n_chunks
    for u, (ti, c) in enumerate(units):
        b = u % 2
        kt, bias_fn = tiles[ti]
        ahead = u + SCORE_AHEAD
        if ahead < len(units):
            nti, nc = units[ahead]
            st.s[ahead % SCORE_RING][...] = score_fn(tiles[nti][0], nc)
        elif has_next:
            st.s[ahead % SCORE_RING][...] = score_fn(tiles[-1][0] + 1, ahead - len(units))
        if u > 0:
            pti, pc = units[u - 1]
            _flash_values(st, vt_fn, tiles[pti][0], pc, 1 - b)
        else:
            _flash_values(st, vt_fn, jnp.maximum(kt - 1, 0), st.n_chunks - 1, 1 - b)
        cs = slice(c * COL_CHUNK, (c + 1) * COL_CHUNK)
        c0 = (c * COL_CHUNK) % ATT_TILE

        def rows(r, s_ref=st.s[u % SCORE_RING], bias_fn=bias_fn, c0=c0):
            blk = s_ref[r * ROW_CHUNK:(r + 1) * ROW_CHUNK, :]
            bias = None if bias_fn is None else bias_fn(c0, r)
            return blk if bias is None else blk + bias

        n_rows = ATT_TILE // ROW_CHUNK
        part = rows(0)
        for r in range(1, n_rows):
            part = jnp.maximum(part, rows(r))
        m_prev = st.m[:, cs]
        m_new = jnp.maximum(m_prev, jnp.max(part, axis=0, keepdims=True))
        st.a[b][...] = jnp.exp2(m_prev - m_new)
        st.m[:, cs] = m_new
        for r in range(n_rows):
            st.p[b][r * ROW_CHUNK:(r + 1) * ROW_CHUNK, :] = jnp.exp2(rows(r) - m_new).astype(BF16)


def _flash_sweep(st, score_fn, vt_fn, n_plain, tail_tiles, last_tile):
    _flash_start(st, score_fn)

    def group(first, count):
        _flash_region(st, score_fn, vt_fn, [(first + d, None) for d in range(count)], True)

    n_octs = lax.shift_right_logical(n_plain, 3)

    def octo(j, carry):
        group(8 * j, 8)
        return carry

    lax.fori_loop(0, n_octs, octo, 0)

    @pl.when((n_plain & 4) == 4)
    def _():
        group(n_octs * 8, 4)

    done = lax.shift_right_logical(n_plain, 2) * 4
    for left in range(4):
        @pl.when((n_plain & 3) == left)
        def _(left=left):
            plain = [(done + d, None) for d in range(left)]
            _flash_region(st, score_fn, vt_fn, plain + tail_tiles(done + left), False)

    _flash_values(st, vt_fn, last_tile, st.n_chunks - 1, 1)


def _chunk_causal_bias_t(t):
    r = np.arange(t)
    ok = (r[:, None] // CHUNK) <= (r[None, :] // CHUNK)
    return jnp.asarray(np.where(ok, 0.0, NEG_INF), F32)


def _attn_b_kernel(qt_ref, k_ref, vt_ref, diag_ref, o_ref, *scratch):
    i = pl.program_id(1)
    t = ATT_TILE
    st = _FlashState(*scratch)
    per_head = t // COL_CHUNK

    def score(kt, c):
        hh = c // per_head
        c0 = (c % per_head) * COL_CHUNK
        k = k_ref[pl.ds(pl.multiple_of(kt * t, t), t), hh * LANES:(hh + 1) * LANES]
        return jnp.dot(k, qt_ref[0, hh * LANES:(hh + 1) * LANES, c0:c0 + COL_CHUNK],
                       preferred_element_type=F32)

    vt_fn = lambda kt: vt_ref[kt]

    def diag(c0, r):
        if (r + 1) * ROW_CHUNK - 1 < (c0 // CHUNK + 1) * CHUNK:
            return None
        return diag_ref[r * ROW_CHUNK:(r + 1) * ROW_CHUNK, c0:c0 + COL_CHUNK]

    _flash_sweep(st, score, vt_fn, i, lambda kt: [(kt, diag)], i)
    o2 = st.acc[...] / st.l[...]
    ot = jnp.concatenate([o2[:B_V_DIM, :t], o2[B_V_DIM:, t:]], axis=0)
    o_ref[...] = ot.T.astype(BF16)


def _attn_b(qbt, kb, vbt):
    nb, _, t = qbt.shape
    s = nb * t
    return pl.pallas_call(
        _attn_b_kernel,
        grid=(B_HEADS // 2, nb),
        in_specs=[pl.BlockSpec((1, 2 * LANES, t), lambda p, i: (i, p, 0)),
                  pl.BlockSpec((s, 2 * LANES), lambda p, i: (0, p)),
                  pl.BlockSpec((nb, LANES, t), lambda p, i: (0, p, 0)),
                  _full((t, t))],
        out_specs=pl.BlockSpec((t, LANES), lambda p, i: (i, p)),
        out_shape=jax.ShapeDtypeStruct((s, B_HEADS * B_V_DIM), BF16),
        scratch_shapes=_flash_scratch(2 * t),
        compiler_params=_params("arbitrary", "arbitrary"),
        name="attn_b",
    )(qbt, kb, vbt, _chunk_causal_bias_t(t))


def _attn_c_kernel(lam_ref, qt_ref, k_ref, vt_ref, d0_ref, d1_ref, g_ref, o_ref, q2_ref, *scratch, out_scale):
    i = pl.program_id(1)
    t = ATT_TILE
    st = _FlashState(*scratch)
    d = C_HEAD_DIM
    zero = jnp.zeros((d, t), BF16)
    q2_ref[:d, :t] = qt_ref[0, :d, :]
    q2_ref[d:, :t] = zero
    q2_ref[:d, t:] = zero
    q2_ref[d:, t:] = qt_ref[0, d:, :]

    def score(kt, c):
        return jnp.dot(k_ref[pl.ds(pl.multiple_of(kt * t, t), t), :],
                       q2_ref[:, c * COL_CHUNK:(c + 1) * COL_CHUNK], preferred_element_type=F32)

    vt_fn = lambda kt: vt_ref[kt]

    tile_of = lambda ref: lambda c0, r: ref[0, r * ROW_CHUNK:(r + 1) * ROW_CHUNK, c0:c0 + COL_CHUNK]

    def diag_bias(c0, r):
        newest_key = (r + 1) * ROW_CHUNK - 1
        unmasked = newest_key < (c0 // CHUNK + 1) * CHUNK
        far = newest_key - c0 <= -T5_MAX_DIST
        return None if unmasked and far else tile_of(d0_ref)(c0, r)

    def below_bias(c0, r):
        if (r + 1) * ROW_CHUNK - 1 - t - c0 <= -T5_MAX_DIST:
            return None
        return tile_of(d1_ref)(c0, r)

    @pl.when(i > 0)
    def _():
        _flash_sweep(st, score, vt_fn, i - 1, lambda kt: [(kt, below_bias), (kt + 1, diag_bias)], i)

    @pl.when(i == 0)
    def _():
        _flash_start(st, score)
        _flash_region(st, score, vt_fn, [(i, diag_bias)], False)
        _flash_values(st, vt_fn, i, st.n_chunks - 1, 1)

    o2 = st.acc[...] / st.l[...]
    ot = o2[:, :t] - lam_ref[0] * o2[:, t:]
    inv = lax.rsqrt(jnp.mean(ot * ot, axis=0, keepdims=True) + RMS_EPS)
    o_ref[...] = (ot * inv * g_ref[...] * out_scale).T.astype(BF16)


def _t5_bucket(rel):
    half = T5_BUCKETS // 2
    max_exact = half // 2
    ret = jnp.where(rel > 0, half, 0)
    n = jnp.abs(rel)
    large = max_exact + (jnp.log(jnp.maximum(n, max_exact).astype(F32) / max_exact)
                         / math.log(T5_MAX_DIST / max_exact) * (half - max_exact)).astype(I32)
    large = jnp.minimum(large, half - 1)
    return ret + jnp.where(n < max_exact, n, large)


def _t5_tiles_t(t5_table, t):
    assert t >= T5_MAX_DIST
    far_bucket = T5_BUCKETS // 2 - 1
    tab = (t5_table.astype(F32) - t5_table[far_bucket].astype(F32)[None, :]).T
    by_rel = lambda rel: tab[:, _t5_bucket(rel)]
    d_pos = jnp.arange(t, dtype=I32)
    d_neg = jnp.arange(-t, 0, dtype=I32)
    d0 = _toeplitz(by_rel(-d_pos), by_rel(-d_neg), t, t)
    d1 = _toeplitz(by_rel(-d_pos - t), by_rel(-d_neg - t), t, t)
    r = np.arange(t)
    ok = (r[:, None] // CHUNK) <= (r[None, :] // CHUNK)
    d0 = jnp.where(jnp.asarray(ok)[None], d0 * LOG2E, NEG_INF)
    return d0, d1 * LOG2E


def _attn_c(qt, k, vt, lam, subln_g, t5_table, lam_init):
    nb, _, t = qt.shape
    s = nb * t
    d0, d1 = _t5_tiles_t(t5_table, t)
    return pl.pallas_call(
        functools.partial(_attn_c_kernel, out_scale=1.0 - lam_init),
        grid=(C_HEADS, nb),
        in_specs=[pl.BlockSpec(memory_space=pltpu.SMEM),
                  pl.BlockSpec((1, LANES, t), lambda h, i: (i, h, 0)),
                  pl.BlockSpec((s, LANES), lambda h, i: (0, h)),
                  pl.BlockSpec((nb, LANES, t), lambda h, i: (0, h, 0)),
                  pl.BlockSpec((1, t, t), lambda h, i: (h, 0, 0)),
                  pl.BlockSpec((1, t, t), lambda h, i: (h, 0, 0)),
                  _full((LANES, 1))],
        out_specs=pl.BlockSpec((t, LANES), lambda h, i: (i, h)),
        out_shape=jax.ShapeDtypeStruct((s, C_HEADS * 2 * C_HEAD_DIM), BF16),
        scratch_shapes=[pltpu.VMEM((LANES, 2 * t), BF16)] + _flash_scratch(2 * t),
        compiler_params=_params("arbitrary", "arbitrary"),
        name="attn_c",
    )(lam.reshape(1), qt, k, vt, d0, d1, subln_g.reshape(-1, 1))


def _prep_c_kernel(x_ref, wq_ref, wk_ref, wv_ref, qt_ref, k_ref, vt_ref):
    xb = x_ref[...].astype(BF16)
    qt = lax.dot_general(wq_ref[...], xb, NT_DIMS, preferred_element_type=F32)
    qt_ref[0] = (qt * (C_HEAD_DIM ** -0.5 * LOG2E)).astype(BF16)
    k_ref[...] = jnp.dot(xb, wk_ref[...], preferred_element_type=F32).astype(BF16)
    vt_ref[0] = lax.dot_general(wv_ref[...], xb, NT_DIMS, preferred_element_type=F32).astype(BF16)


def _prep_c(x, w_in):
    s = x.shape[0]
    t = ATT_TILE
    wq = w_in[:, :C_QK].T.astype(BF16)
    wk = w_in[:, C_QK:2 * C_QK].astype(BF16)
    wv = w_in[:, 2 * C_QK:].T.astype(BF16)
    row = pl.BlockSpec((t, D_MODEL), lambda i: (i, 0))
    tr = pl.BlockSpec((1, C_QK, t), lambda i: (i, 0, 0))
    sds = jax.ShapeDtypeStruct
    return pl.pallas_call(
        _prep_c_kernel,
        grid=(s // t,),
        in_specs=[row, _full(wq.shape), _full(wk.shape), _full(wv.shape)],
        out_specs=[tr, row, tr],
        out_shape=[sds((s // t, C_QK, t), BF16), sds((s, C_QK), BF16), sds((s // t, C_QK, t), BF16)],
        compiler_params=_params("arbitrary"),
        name="prep_c",
    )(x, wq, wk, wv)


def _post_attn_kernel(*refs, n_in):
    x_ref = refs[0]
    o_refs = refs[1:1 + n_in]
    w_refs = refs[1 + n_in:1 + 2 * n_in]
    g_ref, b_ref, out_ref = refs[1 + 2 * n_in:]
    y = DEEPNORM_ALPHA * x_ref[...]
    for o_ref, w_ref in zip(o_refs, w_refs):
        y = y + jnp.dot(o_ref[...], w_ref[...], preferred_element_type=F32)
    _store_row_tiles(out_ref, _layer_norm(y, g_ref[...], b_ref[...]))


def _post_attn(x, outs, weights, g, b):
    s = x.shape[0]
    t = ATT_TILE
    row = lambda w: pl.BlockSpec((t, w), lambda i: (i, 0))
    ws = [w.astype(BF16) for w in weights]
    return pl.pallas_call(
        functools.partial(_post_attn_kernel, n_in=len(outs)),
        grid=(s // t,),
        in_specs=[row(D_MODEL)] + [row(o.shape[1]) for o in outs] + [_full(w.shape) for w in ws]
        + [_full((1, D_MODEL))] * 2,
        out_specs=pl.BlockSpec((t, ROW_SUBLANES, LANES), lambda i: (i, 0, 0)),
        out_shape=jax.ShapeDtypeStruct((s, ROW_SUBLANES, LANES), F32),
        compiler_params=_params("arbitrary"),
        name="post_attn",
    )(x, *outs, *ws, g.reshape(1, -1), b.reshape(1, -1))


def _router_kernel(x_ref, wr_ref, br_ref, tri_ref, idx_ref, rank_ref, w_ref, cnt_ref, carry):
    step = pl.program_id(0)

    @pl.when(step == 0)
    def _():
        carry[...] = jnp.zeros(carry.shape, F32)

    t = ROUTER_TILE
    logits = lax.dot_general(wr_ref[...], _load_row_tiles(x_ref), NT_DIMS, preferred_element_type=F32,
                             precision=lax.Precision.HIGHEST) + br_ref[...]
    eid = lax.broadcasted_iota(I32, (N_EXPERTS, t), 0).astype(F32)
    work = logits
    vals, sels = [], []
    for _ in range(TOP_K):
        m = jnp.max(work, axis=0, keepdims=True)
        sel = jnp.min(jnp.where(work == m, eid, float(N_EXPERTS)), axis=0, keepdims=True)
        vals.append(m)
        sels.append(sel)
        work = jnp.where(eid == sel, -jnp.inf, work)
    exps = [jnp.exp(v - vals[0]) for v in vals]
    denom = exps[0] + exps[1] + exps[2] + exps[3]

    chosen = jnp.zeros((N_EXPERTS, t), F32)
    for sel in sels:
        chosen = chosen + (eid == sel).astype(F32)
    before = jnp.dot(chosen.astype(BF16), tri_ref[...], preferred_element_type=F32) + carry[...]
    for kk, sel in enumerate(sels):
        rank = jnp.sum(jnp.where(eid == sel, before, 0.0), axis=0, keepdims=True)
        rank_ref[kk:kk + 1, :] = rank.astype(I32)
        idx_ref[kk:kk + 1, :] = sel.astype(I32)
    carry[...] = carry[...] + jnp.sum(chosen, axis=1, keepdims=True)
    cnt_ref[...] = jnp.broadcast_to(carry[...], cnt_ref.shape)

    sub = lax.broadcasted_iota(I32, (LANES, t), 0)
    wrows = jnp.zeros((LANES, t), F32)
    for kk in range(TOP_K):
        wrows = jnp.where(sub == kk, exps[kk] / denom, wrows)
    w_ref[...] = wrows.T


def _router(x1, router_w, router_b):
    s = x1.shape[0]
    t = ROUTER_TILE
    tri = jnp.asarray(np.triu(np.ones((t, t), np.float32), k=1), BF16)
    tok = pl.BlockSpec((TOP_K, t), lambda i: (0, i))
    return pl.pallas_call(
        _router_kernel,
        grid=(s // t,),
        in_specs=[pl.BlockSpec((t, ROW_SUBLANES, LANES), lambda i: (i, 0, 0)),
                  _full((N_EXPERTS, D_MODEL)), _full((N_EXPERTS, 1)), _full((t, t))],
        out_specs=[tok, tok, pl.BlockSpec((t, LANES), lambda i: (i, 0)), _full((N_EXPERTS, LANES))],
        out_shape=[jax.ShapeDtypeStruct((TOP_K, s), I32), jax.ShapeDtypeStruct((TOP_K, s), I32),
                   jax.ShapeDtypeStruct((s, LANES), F32), jax.ShapeDtypeStruct((N_EXPERTS, LANES), F32)],
        scratch_shapes=[pltpu.VMEM((N_EXPERTS, 1), F32)],
        compiler_params=_params("arbitrary"),
        name="router",
    )(x1, router_w.T, router_b.reshape(-1, 1), tri)


def _dispatch_kernel(pad_ref, pos_hbm, x_ref, xs_hbm, pos_smem, zeros, pos_sem, row_sem, zero_sem):
    i = pl.program_id(0)
    t = DISPATCH_TILE
    n = t * TOP_K
    tm = EXPERT_TILE

    @pl.when(i == 0)
    def _():
        zeros[...] = jnp.zeros(zeros.shape, F32)

        def fill(e):
            row = pl.multiple_of(jnp.maximum(pad_ref[e], 0), tm)
            return pltpu.make_async_copy(zeros, xs_hbm.at[pl.ds(row, tm)], zero_sem)

        for e in range(2 * N_EXPERTS):
            pl.when(pad_ref[e] >= 0)(lambda e=e: fill(e).start())
        for e in range(2 * N_EXPERTS):
            pl.when(pad_ref[e] >= 0)(lambda e=e: fill(e).wait())

    cp = pltpu.make_async_copy(pos_hbm.at[pl.ds(pl.multiple_of(i * n, n), n)], pos_smem, pos_sem)
    cp.start()
    cp.wait()

    def body(g, carry):
        for j in range(ISSUE_UNROLL // TOP_K):
            tok = g * (ISSUE_UNROLL // TOP_K) + j
            for kk in range(TOP_K):
                pltpu.make_async_copy(x_ref.at[tok], xs_hbm.at[pos_smem[tok * TOP_K + kk]],
                                      row_sem).start(priority=kk % 2)
        return carry

    lax.fori_loop(0, n // ISSUE_UNROLL, body, 0)
    for kk in range(TOP_K):
        pltpu.make_async_copy(x_ref, xs_hbm.at[pl.ds(0, t)], row_sem).wait()


def _dispatch(pad_rows, pos_flat, x1, n_rows):
    s = x1.shape[0]
    t = DISPATCH_TILE
    any_spec = pl.BlockSpec(memory_space=pl.ANY)
    grid_spec = pltpu.PrefetchScalarGridSpec(
        num_scalar_prefetch=1,
        grid=(s // t,),
        in_specs=[any_spec, pl.BlockSpec((t, ROW_SUBLANES, LANES), lambda i, pad: (i, 0, 0))],
        out_specs=any_spec,
        scratch_shapes=[pltpu.SMEM((t * TOP_K,), I32), pltpu.VMEM((EXPERT_TILE, ROW_SUBLANES, LANES), F32),
                        pltpu.SemaphoreType.DMA, pltpu.SemaphoreType.DMA, pltpu.SemaphoreType.DMA],
    )
    return pl.pallas_call(
        _dispatch_kernel,
        grid_spec=grid_spec,
        out_shape=jax.ShapeDtypeStruct((n_rows, ROW_SUBLANES, LANES), F32),
        compiler_params=_params("arbitrary"),
        name="dispatch",
    )(pad_rows, pos_flat, x1)


def _gmm_kernel(te_ref, nt_ref, xs_ref, wi_ref, bi_ref, wo_ref, bo_ref, y_ref, wi_bf, wo_bf):
    i = pl.program_id(0)
    n_tiles = nt_ref[0]
    prev = te_ref[jnp.maximum(i - 1, 0)]
    fresh = jnp.logical_or(i == 0, te_ref[i] != prev)

    @pl.when(jnp.logical_and(i < n_tiles, fresh))
    def _():
        def cast(c, carry):
            rows = pl.ds(pl.multiple_of(c * CAST_ROWS, CAST_ROWS), CAST_ROWS)
            wi_bf[rows, :] = wi_ref[0, rows, :].astype(BF16)
            wo_bf[rows, :] = wo_ref[0, rows, :].astype(BF16)
            return carry

        lax.fori_loop(0, D_MODEL // CAST_ROWS, cast, 0)

    @pl.when(i < n_tiles)
    def _():
        xb = _load_row_tiles(xs_ref).astype(BF16)
        h = jnp.dot(xb, wi_bf[...], preferred_element_type=F32) + bi_ref[0]
        glu = jnp.minimum(h[:, :D_FF], SWIGLU_LIMIT)
        lin = jnp.clip(h[:, D_FF:], -SWIGLU_LIMIT, SWIGLU_LIMIT)
        act = glu * jax.nn.sigmoid(SWIGLU_ALPHA * glu) * (lin + 1.0)
        _store_row_tiles(y_ref, jnp.dot(act.astype(BF16), wo_bf[...], preferred_element_type=F32) + bo_ref[0])

    @pl.when(i >= n_tiles)
    def _():
        y_ref[...] = jnp.zeros(y_ref.shape, F32)


def _gmm(layer, tile_expert, n_tiles, xs, w_in, b_in, w_out, b_out):
    n_rows = xs.shape[0]
    tm = EXPERT_TILE
    last = lambda i, nt: jnp.maximum(jnp.minimum(i, nt[0] - 1), 0)
    expert = lambda i, te, nt: layer * N_EXPERTS + te[last(i, nt)]
    grid_spec = pltpu.PrefetchScalarGridSpec(
        num_scalar_prefetch=2,
        grid=(n_rows // tm,),
        in_specs=[pl.BlockSpec((tm, ROW_SUBLANES, LANES), lambda i, te, nt: (last(i, nt), 0, 0)),
                  pl.BlockSpec((1, D_MODEL, 2 * D_FF), lambda i, te, nt: (expert(i, te, nt), 0, 0)),
                  pl.BlockSpec((1, 1, 2 * D_FF), lambda i, te, nt: (expert(i, te, nt), 0, 0)),
                  pl.BlockSpec((1, D_FF, D_MODEL), lambda i, te, nt: (expert(i, te, nt), 0, 0)),
                  pl.BlockSpec((1, 1, D_MODEL), lambda i, te, nt: (expert(i, te, nt), 0, 0))],
        out_specs=pl.BlockSpec((tm, ROW_SUBLANES, LANES), lambda i, te, nt: (i, 0, 0)),
        scratch_shapes=[pltpu.VMEM((D_MODEL, 2 * D_FF), BF16), pltpu.VMEM((D_FF, D_MODEL), BF16)],
    )
    return pl.pallas_call(
        _gmm_kernel,
        grid_spec=grid_spec,
        out_shape=jax.ShapeDtypeStruct((n_rows, ROW_SUBLANES, LANES), F32),
        compiler_params=_params("arbitrary"),
        name="expert_mlp",
    )(tile_expert, n_tiles, xs, w_in, b_in, w_out, b_out)


def _combine_kernel(pos_hbm, y_hbm, x_ref, w_ref, p_ref, wg_ref, wp_ref, g_ref, b_ref, out_ref,
                    pos_smem, ybuf, pos_sem, row_sem):
    i = pl.program_id(0)
    t = COMBINE_TILE
    n = t * TOP_K

    def fetch_indices(step):
        base = pl.multiple_of((step & 1) * n, n)
        cp = pltpu.make_async_copy(pos_hbm.at[pl.ds(pl.multiple_of(step * n, n), n)],
                                   pos_smem.at[pl.ds(base, n)], pos_sem)
        cp.start()
        cp.wait()

    def row_copy(step, tok, kk):
        slot = step & 1
        return pltpu.make_async_copy(y_hbm.at[pos_smem[slot * n + tok * TOP_K + kk]], ybuf.at[slot, kk, tok],
                                     row_sem.at[slot])

    def wait_step(step):
        slot = step & 1
        for kk in range(TOP_K):
            pltpu.make_async_copy(y_hbm.at[pl.ds(0, t)], ybuf.at[slot, kk], row_sem.at[slot]).wait()

    @pl.when(i == 0)
    def _():
        fetch_indices(i)

        def issue(tok, carry):
            for kk in range(TOP_K):
                row_copy(i, tok, kk).start(priority=kk % 2)
            return carry

        lax.fori_loop(0, t, issue, 0, unroll=2)

    nxt = i + 1
    fetch_indices(nxt)
    x = _load_row_tiles(x_ref)
    pb = p_ref[...].astype(BF16)
    w = w_ref[...]
    for tok in range(t):
        for kk in range(TOP_K):
            row_copy(nxt, tok, kk).start(priority=kk % 2)
    gate = jax.nn.sigmoid(jnp.dot(x.astype(BF16), wg_ref[...], preferred_element_type=F32))
    proj = jnp.dot(pb, wp_ref[...], preferred_element_type=F32)
    y = DEEPNORM_ALPHA * x + gate * proj

    wait_step(i)
    for kk in range(TOP_K):
        y = y + w[:, kk:kk + 1] * _load_row_tiles(ybuf, i & 1, kk)
    out_ref[...] = _layer_norm(y, g_ref[...], b_ref[...])

    @pl.when(nxt == pl.num_programs(0))
    def _():
        wait_step(nxt)


def _combine(pos_flat, y, x1, wcol, p, gate_w, proj_w, g, b):
    s = x1.shape[0]
    t = COMBINE_TILE
    any_spec = pl.BlockSpec(memory_space=pl.ANY)
    row = lambda w: pl.BlockSpec((t, w), lambda i: (i, 0))
    return pl.pallas_call(
        _combine_kernel,
        grid=(s // t,),
        in_specs=[any_spec, any_spec, pl.BlockSpec((t, ROW_SUBLANES, LANES), lambda i: (i, 0, 0)),
                  row(LANES), row(PLE_DIM),
                  _full((D_MODEL, D_MODEL)), _full((PLE_DIM, D_MODEL)),
                  _full((1, D_MODEL)), _full((1, D_MODEL))],
        out_specs=row(D_MODEL),
        out_shape=jax.ShapeDtypeStruct((s, D_MODEL), F32),
        scratch_shapes=[pltpu.SMEM((2 * t * TOP_K,), I32),
                        pltpu.VMEM((2, TOP_K, t, ROW_SUBLANES, LANES), F32),
                        pltpu.SemaphoreType.DMA, pltpu.SemaphoreType.DMA((2,))],
        compiler_params=_params("arbitrary"),
        name="combine",
    )(pos_flat, y, x1, wcol, p, gate_w.astype(BF16), proj_w.astype(BF16), g.reshape(1, -1), b.reshape(1, -1))


def _moe_block(layer, x1, p, router_w, router_b, w_in, b_in, w_out, b_out, gate_w, proj_w, g, b):
    s = x1.shape[0]
    tm = EXPERT_TILE
    n_tiles_max = s * TOP_K // tm + N_EXPERTS
    idx, rank, wcol, cnt = _router(x1, router_w, router_b)
    counts = cnt[:, 0].astype(I32)
    tiles_per = (counts + tm - 1) // tm
    tile_end = jnp.cumsum(tiles_per)
    offsets = (tile_end - tiles_per) * tm
    experts = jnp.arange(N_EXPERTS, dtype=I32)
    pos = rank + jnp.sum(jnp.where(idx[..., None] == experts, offsets, 0), axis=-1)
    pos_flat = pos.T.reshape(-1)
    tile_ids = jnp.arange(n_tiles_max, dtype=I32)
    tile_expert = jnp.minimum(jnp.sum((tile_end[None, :] <= tile_ids[:, None]).astype(I32), axis=-1),
                              N_EXPERTS - 1)
    n_tiles = tile_end[-1:].astype(I32)
    last_tiles = jnp.where(tiles_per > 0, tile_end - 1, -1)
    tail_tiles = jnp.where(tile_end[-1] + experts < n_tiles_max, tile_end[-1] + experts, -1)
    fill_tiles = jnp.concatenate([last_tiles, tail_tiles])
    pad_rows = jnp.where(fill_tiles >= 0, fill_tiles * tm, -1).astype(I32)
    xs = _dispatch(pad_rows, pos_flat, x1, n_tiles_max * tm)
    y = _gmm(layer, tile_expert, n_tiles, xs, w_in, b_in, w_out, b_out)
    pos_ahead = jnp.concatenate([pos_flat, jnp.zeros((COMBINE_TILE * TOP_K,), I32)])
    return _combine(pos_ahead, y, x1, wcol, p, gate_w, proj_w, g, b)


def _lambda_init(layer_idx):
    return 0.8 - 0.6 * math.exp(-0.3 * layer_idx)


def kernel(x, p, ab_w_in, ab_rel_bias, ab_q_norm, ab_kv_norm, ab_w_uq, ab_w_ukv, ab_w_out, c_w_in, c_lambda, c_subln, c_w_out, t5_table, ln_mix_g, ln_mix_b, ln_ffn_g, ln_ffn_b, router_w, router_b, exp_w_in, exp_b_in, exp_w_out, exp_b_out, ple_gate_w, ple_proj_w):
    batch, s, _ = x.shape
    assert batch == 1 and s % DISPATCH_TILE == 0 and s % ATT_TILE == 0
    xr = x[0]
    w_in_all = exp_w_in.reshape(DEPTH * N_EXPERTS, D_MODEL, 2 * D_FF)
    b_in_all = exp_b_in.reshape(DEPTH * N_EXPERTS, 1, 2 * D_FF)
    w_out_all = exp_w_out.reshape(DEPTH * N_EXPERTS, D_FF, D_MODEL)
    b_out_all = exp_b_out.reshape(DEPTH * N_EXPERTS, 1, D_MODEL)
    for i in range(DEPTH):
        j = i // 2
        if i % 2 == 0:
            qa, ka, va, qbt, kb, vbt = _prep_ab(xr, ab_w_in[j], ab_q_norm[j], ab_kv_norm[j], ab_w_uq[j],
                                                ab_w_ukv[j])
            o_a = _attn_a(qa, ka, va, ab_rel_bias[j])
            o_b = _attn_b(qbt, kb, vbt)
            x1 = _post_attn(xr, [o_a, o_b], [ab_w_out[j][:A_WIDTH], ab_w_out[j][A_WIDTH:]],
                            ln_mix_g[i], ln_mix_b[i])
        else:
            qt, k, vt = _prep_c(xr, c_w_in[j])
            lp = c_lambda[j].astype(F32)
            lam_init = _lambda_init(i)
            lam = jnp.exp(jnp.sum(lp[0] * lp[1])) - jnp.exp(jnp.sum(lp[2] * lp[3])) + lam_init
            o_c = _attn_c(qt, k, vt, lam, c_subln[j], t5_table, lam_init)
            x1 = _post_attn(xr, [o_c], [c_w_out[j]], ln_mix_g[i], ln_mix_b[i])
        xr = _moe_block(i, x1, p[i, 0], router_w[i], router_b[i], w_in_all, b_in_all, w_out_all, b_out_all,
                        ple_gate_w[i], ple_proj_w[i], ln_ffn_g[i], ln_ffn_b[i])
    return xr[None]
```
